```python
import math
import jax, jax.numpy as jnp
from jax import lax
import numpy as np

D_MODEL = 1024
BATCH = 2
SEQ = 8192
DEPTH = 2

HEAD_DIM = 64
N_MIX_HEADS = D_MODEL // HEAD_DIM
MIX_WIDTH = N_MIX_HEADS * HEAD_DIM
FOX_HEADS = N_MIX_HEADS // 4
NSA_HEADS = N_MIX_HEADS // 2
DIFF_HEADS = N_MIX_HEADS - FOX_HEADS - NSA_HEADS
FOX_W = FOX_HEADS * HEAD_DIM
NSA_W = NSA_HEADS * HEAD_DIM
DIFF_QK_DIM = HEAD_DIM // 2
DIFF_V_DIM = HEAD_DIM
DIFF_W = DIFF_HEADS * DIFF_V_DIM
NSA_KV_GROUPS = 2
NSA_HPG = NSA_HEADS // NSA_KV_GROUPS
NSA_KV_W = NSA_KV_GROUPS * HEAD_DIM
CMP_LEN = 32
CMP_STRIDE = 16
CMP_HIDDEN = 2 * HEAD_DIM
SEL_BLOCK = 64
SEL_TOPK = 16
WINDOW = 512
Q_BLOCK = 128
ROPE_THETA = 10000.0
LN_EPS = 1e-5
RMS_EPS = 1e-5
NEG_INF = -1e30
FORCE_SCORE = 1e30
DEEPNORM_ALPHA = (2 * DEPTH) ** 0.25
DEEPNORM_BETA = (8 * DEPTH) ** -0.25

SPLITS = (
    ('fox_q', FOX_W, False), ('fox_k', FOX_W, False), ('fox_v', FOX_W, True),
    ('fox_f', FOX_HEADS, False), ('fox_z', FOX_W, False),
    ('nsa_q', NSA_W, False),
    ('nsa_k_cmp', NSA_KV_W, False), ('nsa_v_cmp', NSA_KV_W, True),
    ('nsa_k_sel', NSA_KV_W, False), ('nsa_v_sel', NSA_KV_W, True),
    ('nsa_k_win', NSA_KV_W, False), ('nsa_v_win', NSA_KV_W, True),
    ('nsa_gate', 3 * NSA_HEADS, False), ('nsa_z', NSA_W, False),
    ('diff_q', DIFF_HEADS * 2 * DIFF_QK_DIM, False), ('diff_k', DIFF_HEADS * 2 * DIFF_QK_DIM, False),
    ('diff_v', DIFF_W, True), ('diff_z', DIFF_W, False),
)
IN_WIDTH = sum(w for _, w, _ in SPLITS)

kernel_name = 'hymba_fox_nsa_diff_deepnorm'


def split_columns(proj):
    offsets, acc = [], 0
    for _, w, _ in SPLITS[:-1]:
        acc += w
        offsets.append(acc)
    parts = jnp.split(proj, offsets, axis=-1)
    return {name: part for (name, _, _), part in zip(SPLITS, parts)}


def masked_softmax(logits, mask):
    logits = jnp.where(mask, logits.astype(jnp.float32), NEG_INF)
    return jax.nn.softmax(logits, axis=-1) * mask


def rope(x):
    S, d = x.shape[1], x.shape[-1]
    half = d // 2
    inv = ROPE_THETA ** (-(jnp.arange(half, dtype=jnp.float32) * 2.0 / d))
    ang = jnp.arange(S, dtype=jnp.float32)[:, None] * inv[None, :]
    shape = (1, S) + (1,) * (x.ndim - 3) + (half,)
    cos, sin = jnp.cos(ang).reshape(shape), jnp.sin(ang).reshape(shape)
    x1 = x[..., :half].astype(jnp.float32)
    x2 = x[..., half:].astype(jnp.float32)
    return jnp.concatenate([x1 * cos - x2 * sin, x2 * cos + x1 * sin], axis=-1).astype(x.dtype)


def layer_norm(x, g, b):
    xf = x.astype(jnp.float32)
    mu = jnp.mean(xf, axis=-1, keepdims=True)
    var = jnp.mean(jnp.square(xf - mu), axis=-1, keepdims=True)
    return ((xf - mu) * lax.rsqrt(var + LN_EPS) * g + b).astype(x.dtype)


def rms_norm(x, g):
    xf = x.astype(jnp.float32)
    return (xf * lax.rsqrt(jnp.mean(jnp.square(xf), axis=-1, keepdims=True) + RMS_EPS) * g).astype(x.dtype)


def fox_attention(q, k, v, log_f):
    B, H, S, dk = q.shape
    c = jnp.cumsum(log_f, axis=-1)
    scale = dk ** -0.5
    key_pos = jnp.arange(S)

    def block(i):
        s0 = i * Q_BLOCK
        t = s0 + jnp.arange(Q_BLOCK)
        qb = lax.dynamic_slice_in_dim(q, s0, Q_BLOCK, axis=2)
        cq = lax.dynamic_slice_in_dim(c, s0, Q_BLOCK, axis=2)
        logits = (jnp.einsum('bhqd,bhkd->bhqk', qb, k).astype(jnp.float32) * scale
                  + cq[..., None] - c[:, :, None, :])
        p = masked_softmax(logits, key_pos[None, :] <= t[:, None]).astype(v.dtype)
        return jnp.einsum('bhqk,bhkd->bhqd', p, v)

    out = lax.map(block, jnp.arange(S // Q_BLOCK))
    return out.transpose(1, 0, 3, 2, 4).reshape(B, S, H * dk)


def diff_attention(q, k, v, lam):
    B, H, _, S, dq = q.shape
    scale = dq ** -0.5
    key_pos = jnp.arange(S)

    def block(i):
        s0 = i * Q_BLOCK
        t = s0 + jnp.arange(Q_BLOCK)
        qb = lax.dynamic_slice_in_dim(q, s0, Q_BLOCK, axis=3)
        logits = jnp.einsum('bhcqd,bhckd->bhcqk', qb, k).astype(jnp.float32) * scale
        p = masked_softmax(logits, key_pos[None, :] <= t[:, None])
        a = (p[:, :, 0] - lam * p[:, :, 1]).astype(v.dtype)
        return jnp.einsum('bhqk,bhkd->bhqd', a, v)

    out = lax.map(block, jnp.arange(S // Q_BLOCK))
    return out.transpose(1, 0, 3, 2, 4).reshape(B, S, H, v.shape[-1])


def compress(tok, pos_emb, w1, w2):
    B, S, G, dk = tok.shape
    n_cmp = (S - CMP_LEN) // CMP_STRIDE + 1
    idx = CMP_STRIDE * jnp.arange(n_cmp)[:, None] + jnp.arange(CMP_LEN)[None, :]
    blocks = tok[:, idx] + pos_emb[None, None, :, None, :]
    blocks = blocks.transpose(0, 3, 1, 2, 4).reshape(B, G, n_cmp, CMP_LEN * dk)
    return jax.nn.silu(blocks @ w1) @ w2


def gather_blocks(blocks, idx):
    return jax.vmap(jax.vmap(lambda blk, ix: blk[ix]))(blocks, idx)


def nsa_attention(q, k_cmp, v_cmp, k_sel, v_sel, k_win, v_win, gates,
                  pos_k, pos_v, w1_k, w2_k, w1_v, w2_v):
    B, S, H, dk = q.shape
    G, hpg = NSA_KV_GROUPS, NSA_HPG
    scale = dk ** -0.5
    qg = q.reshape(B, S, G, hpg, dk).transpose(0, 2, 3, 1, 4)
    gg = gates.reshape(B, S, G, hpg, 3).transpose(0, 2, 3, 1, 4)
    kc = compress(k_cmp, pos_k, w1_k, w2_k)
    vc = compress(v_cmp, pos_v, w1_v, w2_v)
    n_cmp = kc.shape[2]
    cmp_start = CMP_STRIDE * jnp.arange(n_cmp)
    cmp_end = cmp_start + CMP_LEN - 1
    n_slc = S // SEL_BLOCK
    n_sel = min(SEL_TOPK, n_slc)
    ksb = k_sel.reshape(B, n_slc, SEL_BLOCK, G, dk).transpose(0, 3, 1, 2, 4)
    vsb = v_sel.reshape(B, n_slc, SEL_BLOCK, G, dk).transpose(0, 3, 1, 2, 4)
    sel_start = SEL_BLOCK * jnp.arange(n_slc)
    overlap = (jnp.clip(jnp.minimum(cmp_start[:, None] + CMP_LEN, sel_start[None, :] + SEL_BLOCK)
                        - jnp.maximum(cmp_start[:, None], sel_start[None, :]), 0)
               .astype(jnp.float32) / CMP_LEN)
    j = jnp.arange(n_slc)
    pad = ((0, 0), (0, 0), (WINDOW, 0), (0, 0))
    kwp = jnp.pad(k_win.transpose(0, 2, 1, 3), pad)
    vwp = jnp.pad(v_win.transpose(0, 2, 1, 3), pad)

    def block(i):
        s0 = i * Q_BLOCK
        t = s0 + jnp.arange(Q_BLOCK)
        qb = lax.dynamic_slice_in_dim(qg, s0, Q_BLOCK, axis=3)
        gb = lax.dynamic_slice_in_dim(gg, s0, Q_BLOCK, axis=3)
        s_c = jnp.einsum('bghqd,bgnd->bghqn', qb, kc).astype(jnp.float32) * scale
        p_c = masked_softmax(s_c, cmp_end[None, :] <= t[:, None])
        o_c = jnp.einsum('bghqn,bgnd->bghqd', p_c.astype(vc.dtype), vc)
        imp = jnp.einsum('bghqn,nj->bgqj', p_c, overlap)
        cur = t // SEL_BLOCK
        valid = j[None, :] * SEL_BLOCK <= t[:, None]
        forced = (j[None, :] == 0) | (j[None, :] == cur[:, None]) | (j[None, :] == cur[:, None] - 1)
        score = jnp.where(valid, jnp.where(forced, FORCE_SCORE, imp), NEG_INF)
        _, idx = lax.top_k(score, n_sel)
        ks = gather_blocks(ksb, idx)
        vs = gather_blocks(vsb, idx)
        tok_pos = idx[..., None] * SEL_BLOCK + jnp.arange(SEL_BLOCK)
        m_s = (tok_pos <= t[None, None, :, None, None]).reshape(B, G, 1, Q_BLOCK, n_sel * SEL_BLOCK)
        s_s = (jnp.einsum('bghqd,bgqnld->bghqnl', qb, ks).astype(jnp.float32)
               .reshape(B, G, hpg, Q_BLOCK, n_sel * SEL_BLOCK) * scale)
        p_s = masked_softmax(s_s, m_s)
        o_s = jnp.einsum('bghqm,bgqmd->bghqd', p_s.astype(vs.dtype),
                         vs.reshape(B, G, Q_BLOCK, n_sel * SEL_BLOCK, dk))
        kw = lax.dynamic_slice_in_dim(kwp, s0, Q_BLOCK + WINDOW, axis=2)
        vw = lax.dynamic_slice_in_dim(vwp, s0, Q_BLOCK + WINDOW, axis=2)
        kpos = s0 - WINDOW + jnp.arange(Q_BLOCK + WINDOW)
        m_w = ((kpos[None, :] <= t[:, None]) & (t[:, None] - kpos[None, :] < WINDOW)
               & (kpos[None, :] >= 0))
        s_w = jnp.einsum('bghqd,bgkd->bghqk', qb, kw).astype(jnp.float32) * scale
        p_w = masked_softmax(s_w, m_w)
        o_w = jnp.einsum('bghqk,bgkd->bghqd', p_w.astype(vw.dtype), vw)
        return gb[..., 0:1] * o_c + gb[..., 1:2] * o_s + gb[..., 2:3] * o_w

    out = lax.map(block, jnp.arange(S // Q_BLOCK))
    return out.transpose(1, 0, 4, 2, 3, 5).reshape(B, S, H * dk)


def setup_inputs(seed: int = 0) -> dict:
    key = jax.random.key(seed)
    ks = jax.random.split(key, 20)
    nrm = lambda k, shape: jax.random.normal(k, shape, jnp.float32)
    col_scale = np.concatenate([np.full((w,), DEEPNORM_BETA if is_v else 1.0, np.float32)
                                for _, w, is_v in SPLITS])
    x = nrm(ks[0], (BATCH, SEQ, D_MODEL))
    w_in = nrm(ks[1], (DEPTH, D_MODEL, IN_WIDTH)) * (D_MODEL ** -0.5) * jnp.asarray(col_scale)
    b_fox_f = 4.0 + 0.1 * nrm(ks[2], (DEPTH, FOX_HEADS))
    cmp_pos_k = 0.1 * nrm(ks[3], (DEPTH, CMP_LEN, HEAD_DIM))
    cmp_pos_v = 0.1 * nrm(ks[4], (DEPTH, CMP_LEN, HEAD_DIM))
    cmp_w1_k = nrm(ks[5], (DEPTH, CMP_LEN * HEAD_DIM, CMP_HIDDEN)) * (CMP_LEN * HEAD_DIM) ** -0.5
    cmp_w2_k = nrm(ks[6], (DEPTH, CMP_HIDDEN, HEAD_DIM)) * CMP_HIDDEN ** -0.5
    cmp_w1_v = nrm(ks[7], (DEPTH, CMP_LEN * HEAD_DIM, CMP_HIDDEN)) * (CMP_LEN * HEAD_DIM) ** -0.5
    cmp_w2_v = nrm(ks[8], (DEPTH, CMP_HIDDEN, HEAD_DIM)) * CMP_HIDDEN ** -0.5
    lam_q1 = 0.1 * nrm(ks[9], (DEPTH, DIFF_QK_DIM))
    lam_k1 = 0.1 * nrm(ks[10], (DEPTH, DIFF_QK_DIM))
    lam_q2 = 0.1 * nrm(ks[11], (DEPTH, DIFF_QK_DIM))
    lam_k2 = 0.1 * nrm(ks[12], (DEPTH, DIFF_QK_DIM))
    diff_subln_g = 1.0 + 0.02 * nrm(ks[13], (DEPTH, DIFF_V_DIM))
    w_out = nrm(ks[14], (DEPTH, MIX_WIDTH, D_MODEL)) * (MIX_WIDTH ** -0.5) * DEEPNORM_BETA
    ln_g = 1.0 + 0.02 * nrm(ks[15], (DEPTH, D_MODEL))
    ln_b = 0.02 * nrm(ks[16], (DEPTH, D_MODEL))
    return {'x': x, 'w_in': w_in, 'b_fox_f': b_fox_f, 'cmp_pos_k': cmp_pos_k, 'cmp_pos_v': cmp_pos_v,
            'cmp_w1_k': cmp_w1_k, 'cmp_w2_k': cmp_w2_k, 'cmp_w1_v': cmp_w1_v, 'cmp_w2_v': cmp_w2_v,
            'lam_q1': lam_q1, 'lam_k1': lam_k1, 'lam_q2': lam_q2, 'lam_k2': lam_k2,
            'diff_subln_g': diff_subln_g, 'w_out': w_out, 'ln_g': ln_g, 'ln_b': ln_b}


def reference(x, w_in, b_fox_f, cmp_pos_k, cmp_pos_v, cmp_w1_k, cmp_w2_k, cmp_w1_v, cmp_w2_v,
              lam_q1, lam_k1, lam_q2, lam_k2, diff_subln_g, w_out, ln_g, ln_b):
    B, S, _ = x.shape
    G = NSA_KV_GROUPS
    for l in range(DEPTH):
        p = split_columns(x @ w_in[l])
        heads_f = lambda a: a.reshape(B, S, FOX_HEADS, HEAD_DIM).transpose(0, 2, 1, 3)
        log_f = jax.nn.log_sigmoid((p['fox_f'] + b_fox_f[l]).astype(jnp.float32)).transpose(0, 2, 1)
        o_fox = fox_attention(heads_f(p['fox_q']), heads_f(p['fox_k']), heads_f(p['fox_v']), log_f)
        o_fox = o_fox * jax.nn.silu(p['fox_z'])
        kv = lambda a: a.reshape(B, S, G, HEAD_DIM)
        o_nsa = nsa_attention(
            rope(p['nsa_q'].reshape(B, S, NSA_HEADS, HEAD_DIM)),
            rope(kv(p['nsa_k_cmp'])), kv(p['nsa_v_cmp']),
            rope(kv(p['nsa_k_sel'])), kv(p['nsa_v_sel']),
            rope(kv(p['nsa_k_win'])), kv(p['nsa_v_win']),
            jax.nn.sigmoid(p['nsa_gate'].reshape(B, S, NSA_HEADS, 3)),
            cmp_pos_k[l], cmp_pos_v[l], cmp_w1_k[l], cmp_w2_k[l], cmp_w1_v[l], cmp_w2_v[l])
        o_nsa = o_nsa * jax.nn.silu(p['nsa_z'])
        qd = rope(p['diff_q'].reshape(B, S, DIFF_HEADS, 2, DIFF_QK_DIM)).transpose(0, 2, 3, 1, 4)
        kd = rope(p['diff_k'].reshape(B, S, DIFF_HEADS, 2, DIFF_QK_DIM)).transpose(0, 2, 3, 1, 4)
        vd = p['diff_v'].reshape(B, S, DIFF_HEADS, DIFF_V_DIM).transpose(0, 2, 1, 3)
        lam_init = 0.8 - 0.6 * math.exp(-0.3 * l)
        lam = (jnp.exp(jnp.sum(lam_q1[l].astype(jnp.float32) * lam_k1[l].astype(jnp.float32)))
               - jnp.exp(jnp.sum(lam_q2[l].astype(jnp.float32) * lam_k2[l].astype(jnp.float32))) + lam_init)
        od = rms_norm(diff_attention(qd, kd, vd, lam), diff_subln_g[l]) * (1.0 - lam_init)
        o_diff = od.reshape(B, S, DIFF_W) * jax.nn.silu(p['diff_z'])
        mix = jnp.concatenate([o_fox, o_nsa, o_diff], axis=-1) @ w_out[l]
        x = layer_norm(DEEPNORM_ALPHA * x + mix, ln_g[l], ln_b[l])
    return x
```

```python
import math
from functools import partial

import numpy as np
import jax
import jax.numpy as jnp
from jax import lax
from jax.experimental import pallas as pl
from jax.experimental.pallas import tpu as pltpu

D_MODEL = 1024
DEPTH = 2
HEAD_DIM = 64
FOX_HEADS = 4
NSA_HEADS = 8
DIFF_HEADS = 4
FOX_W = FOX_HEADS * HEAD_DIM
NSA_W = NSA_HEADS * HEAD_DIM
DIFF_QK_DIM = HEAD_DIM // 2
DIFF_W = DIFF_HEADS * HEAD_DIM
NSA_KV_GROUPS = 2
NSA_HPG = NSA_HEADS // NSA_KV_GROUPS
NSA_KV_W = NSA_KV_GROUPS * HEAD_DIM
CMP_LEN = 32
CMP_STRIDE = 16
CMP_HIDDEN = 2 * HEAD_DIM
SEL_BLOCK = 64
SEL_SHIFT = SEL_BLOCK.bit_length() - 1
SEL_TOPK = 16
WINDOW = 512
ROPE_THETA = 10000.0
LN_EPS = 1e-5
RMS_EPS = 1e-5
NEG_INF = -1e30
FORCE_SCORE = 1e30
DEEPNORM_ALPHA = (2 * DEPTH) ** 0.25

SPLITS = (
    ('fox_q', FOX_W), ('fox_k', FOX_W), ('fox_v', FOX_W), ('fox_f', FOX_HEADS), ('fox_z', FOX_W),
    ('nsa_q', NSA_W),
    ('nsa_k_cmp', NSA_KV_W), ('nsa_v_cmp', NSA_KV_W),
    ('nsa_k_sel', NSA_KV_W), ('nsa_v_sel', NSA_KV_W),
    ('nsa_k_win', NSA_KV_W), ('nsa_v_win', NSA_KV_W),
    ('nsa_gate', 3 * NSA_HEADS), ('nsa_z', NSA_W),
    ('diff_q', DIFF_W), ('diff_k', DIFF_W), ('diff_v', DIFF_W), ('diff_z', DIFF_W),
)
_OFF = {}
_acc = 0
for _n, _w in SPLITS:
    _OFF[_n] = _acc
    _acc += _w
IN_WIDTH = _acc

LANES = 128
VMEM_LIMIT = 56 * 1024 * 1024

PROJ_TM = 256
CUM_T = 512
CMP_TQ = 256
ATT_T = 512
N_CMP_PAD = 512


def _cparams(sem):
    return pltpu.CompilerParams(dimension_semantics=sem, vmem_limit_bytes=VMEM_LIMIT)


def _dot(a, b):
    return jnp.dot(a, b, preferred_element_type=jnp.float32)


def _dot_t(a, b):
    return lax.dot_general(a, b, (((1,), (1,)), ((), ())), preferred_element_type=jnp.float32)


def _dup(name):
    base = _OFF[name]
    cols = []
    for g in range(NSA_KV_GROUPS):
        one = list(range(base + g * HEAD_DIM, base + (g + 1) * HEAD_DIM))
        cols += one + one
    return cols


def _pad(cols, width):
    return list(cols) + [-1] * (width - len(cols))


def _rng(name, width):
    return list(range(_OFF[name], _OFF[name] + width))


PROJ_SEGS = (
    ('fox_q', _rng('fox_q', FOX_W), 'scale8', jnp.bfloat16),
    ('fox_k', _rng('fox_k', FOX_W), 'none', jnp.bfloat16),
    ('fox_v', _rng('fox_v', FOX_W), 'none', jnp.bfloat16),
    ('fox_z', _rng('fox_z', FOX_W), 'silu', jnp.float32),
    ('fox_logf', _pad(_rng('fox_f', FOX_HEADS), LANES), 'logsig', jnp.float32),
    ('nsa_q', _rng('nsa_q', NSA_W), 'rope64_scale8', jnp.bfloat16),
    ('nsa_k_cmp', _rng('nsa_k_cmp', NSA_KV_W), 'rope64', jnp.float32),
    ('nsa_v_cmp', _rng('nsa_v_cmp', NSA_KV_W), 'none', jnp.float32),
    ('nsa_k_sel', _dup('nsa_k_sel'), 'rope64', jnp.bfloat16),
    ('nsa_v_sel', _dup('nsa_v_sel'), 'none', jnp.bfloat16),
    ('nsa_k_win', _dup('nsa_k_win'), 'rope64', jnp.bfloat16),
    ('nsa_v_win', _dup('nsa_v_win'), 'none', jnp.bfloat16),
    ('nsa_gate', _pad(_rng('nsa_gate', 3 * NSA_HEADS), LANES), 'sigmoid', jnp.float32),
    ('nsa_z', _rng('nsa_z', NSA_W), 'silu', jnp.float32),
    ('diff_q', _rng('diff_q', DIFF_W), 'rope32', jnp.bfloat16),
    ('diff_k', _rng('diff_k', DIFF_W), 'rope32', jnp.bfloat16),
    ('diff_v', _rng('diff_v', DIFF_W), 'none', jnp.bfloat16),
    ('diff_z', _rng('diff_z', DIFF_W), 'silu', jnp.float32),
)
PROJ_COLS = np.concatenate([np.asarray(c, np.int32) for _, c, _, _ in PROJ_SEGS])
PROJ_N = int(PROJ_COLS.shape[0])


def _rope_chunk(y, cos, sin, half):
    lane = lax.broadcasted_iota(jnp.int32, y.shape, 1)
    first = (lane & (2 * half - 1)) < half
    partner = jnp.where(first, pltpu.roll(y, LANES - half, 1), pltpu.roll(y, half, 1))
    return y * cos + partner * sin


def _proj_kernel(x_ref, w_ref, bias_ref, cos64_ref, sin64_ref, cos32_ref, sin32_ref, *out_refs):
    xb = x_ref[...].astype(jnp.bfloat16)
    start = 0
    for (name, cols, epi, dtype), o_ref in zip(PROJ_SEGS, out_refs):
        width = len(cols)
        y = _dot(xb, w_ref[:, start:start + width])
        start += width
        if epi == 'scale8':
            y = y * (HEAD_DIM ** -0.5)
        elif epi == 'silu':
            y = y * jax.nn.sigmoid(y)
        elif epi == 'sigmoid':
            y = jax.nn.sigmoid(y)
        elif epi == 'logsig':
            y = y + bias_ref[...]
            y = jnp.minimum(y, 0.0) - jnp.log1p(jnp.exp(-jnp.abs(y)))
        elif epi in ('rope64', 'rope64_scale8', 'rope32'):
            half = DIFF_QK_DIM // 2 if epi == 'rope32' else HEAD_DIM // 2
            cos = cos32_ref[...] if epi == 'rope32' else cos64_ref[...]
            sin = sin32_ref[...] if epi == 'rope32' else sin64_ref[...]
            chunks = [_rope_chunk(y[:, c * LANES:(c + 1) * LANES], cos, sin, half)
                      for c in range(width // LANES)]
            y = chunks[0] if len(chunks) == 1 else jnp.concatenate(chunks, axis=1)
            if epi == 'rope64_scale8':
                y = y * (HEAD_DIM ** -0.5)
        o_ref[...] = y.astype(dtype)


def _rope_tables(seq, group):
    half = group // 2
    inv = ROPE_THETA ** (-(jnp.arange(half, dtype=jnp.float32) * 2.0 / group))
    ang = jnp.arange(seq, dtype=jnp.float32)[:, None] * inv[None, :]
    cos, sin = jnp.cos(ang), jnp.sin(ang)
    reps = LANES // group
    cos_t = jnp.tile(jnp.concatenate([cos, cos], axis=1), (1, reps))
    sin_t = jnp.tile(jnp.concatenate([-sin, sin], axis=1), (1, reps))
    return cos_t, sin_t


def _project(x2d, w_perm, bias_row, tables, seq):
    m = x2d.shape[0]
    nt = seq // PROJ_TM
    tab_spec = pl.BlockSpec((PROJ_TM, LANES), lambda i: (i % nt, 0))
    out_shape = [jax.ShapeDtypeStruct((m, len(cols)), dt) for _, cols, _, dt in PROJ_SEGS]
    out_specs = [pl.BlockSpec((PROJ_TM, len(cols)), lambda i: (i, 0)) for _, cols, _, _ in PROJ_SEGS]
    outs = pl.pallas_call(
        _proj_kernel,
        grid=(m // PROJ_TM,),
        in_specs=[pl.BlockSpec((PROJ_TM, D_MODEL), lambda i: (i, 0)),
                  pl.BlockSpec((D_MODEL, PROJ_N), lambda i: (0, 0)),
                  pl.BlockSpec((1, LANES), lambda i: (0, 0)),
                  tab_spec, tab_spec, tab_spec, tab_spec],
        out_specs=out_specs,
        out_shape=out_shape,
        compiler_params=_cparams(("parallel",)),
        name="in_proj",
    )(x2d, w_perm, bias_row, *tables)
    return {name: o for (name, _, _, _), o in zip(PROJ_SEGS, outs)}


def _cumsum_kernel(lf_ref, c_ref, carry_ref):
    @pl.when(pl.program_id(1) == 0)
    def _():
        carry_ref[...] = jnp.zeros_like(carry_ref)

    lf = lf_ref[...]
    row = lax.broadcasted_iota(jnp.int32, (CUM_T, CUM_T), 0)
    col = lax.broadcasted_iota(jnp.int32, (CUM_T, CUM_T), 1)
    tri = jnp.where(col <= row, 1.0, 0.0).astype(jnp.bfloat16)
    hi = lf.astype(jnp.bfloat16)
    r1 = lf - hi.astype(jnp.float32)
    mid = r1.astype(jnp.bfloat16)
    lo = (r1 - mid.astype(jnp.float32)).astype(jnp.bfloat16)
    c = _dot(tri, hi) + _dot(tri, mid) + _dot(tri, lo) + carry_ref[0:1, :]
    c_ref[...] = c
    carry_ref[0:1, :] = c[CUM_T - 1:CUM_T, :]


def _cumsum(logf, batch, seq):
    nt = seq // CUM_T
    return pl.pallas_call(
        _cumsum_kernel,
        grid=(batch, nt),
        in_specs=[pl.BlockSpec((CUM_T, LANES), lambda b, j: (b * nt + j, 0))],
        out_specs=pl.BlockSpec((CUM_T, LANES), lambda b, j: (b * nt + j, 0)),
        out_shape=jax.ShapeDtypeStruct(logf.shape, jnp.float32),
        scratch_shapes=[pltpu.VMEM((8, LANES), jnp.float32)],
        compiler_params=_cparams(("arbitrary", "arbitrary")),
        name="fox_cumsum",
    )(logf)


def _compress_kernel(ch_ref, pos_ref, w1_ref, w2_ref, o_ref):
    ch = ch_ref[...]
    half = CMP_STRIDE * HEAD_DIM
    a_in = (ch + pos_ref[:, :half]).astype(jnp.bfloat16)
    b_in = (ch + pos_ref[:, half:]).astype(jnp.bfloat16)
    a = _dot(a_in, w1_ref[:half, :])
    b = _dot(b_in, w1_ref[half:, :])
    n_chunks = ch.shape[0]
    h = a + pltpu.roll(b, n_chunks - 1, 0)
    act = (h * jax.nn.sigmoid(h)).astype(jnp.bfloat16)
    out = _dot(act, w2_ref[...])
    row = lax.broadcasted_iota(jnp.int32, out.shape, 0)
    o_ref[...] = jnp.where(row < n_chunks - 1, out, 0.0).astype(o_ref.dtype)


def _compress(chunks, pos, w1, w2d):
    _, batch, groups, n_chunks, width = chunks.shape
    sq = None
    return pl.pallas_call(
        _compress_kernel,
        grid=(2, batch, groups),
        in_specs=[pl.BlockSpec((sq, sq, sq, n_chunks, width), lambda t, b, g: (t, b, g, 0, 0)),
                  pl.BlockSpec((sq, 1, 2 * width), lambda t, b, g: (t, 0, 0)),
                  pl.BlockSpec((sq, 2 * width, CMP_HIDDEN), lambda t, b, g: (t, 0, 0)),
                  pl.BlockSpec((sq, CMP_HIDDEN, LANES), lambda t, b, g: (t, 0, 0))],
        out_specs=pl.BlockSpec((sq, sq, sq, n_chunks, LANES), lambda t, b, g: (t, b, g, 0, 0)),
        out_shape=jax.ShapeDtypeStruct((2, batch, groups, n_chunks, LANES), jnp.bfloat16),
        compiler_params=_cparams(("parallel", "parallel", "parallel")),
        name="nsa_compress",
    )(chunks, pos, w1, w2d)


def _half_mask(shape, hh):
    lane = lax.broadcasted_iota(jnp.int32, shape, 1)
    return (lane >= hh * HEAD_DIM) & (lane < (hh + 1) * HEAD_DIM)


def _cmp_topk_kernel(q_ref, kc_ref, vc_ref, ov_ref, oc_ref, negsel_ref):
    tq = q_ref.shape[0]
    t = pl.program_id(2) * tq + lax.broadcasted_iota(jnp.int32, (tq, 1), 0)
    n = lax.broadcasted_iota(jnp.int32, (1, N_CMP_PAD), 1)
    mask_c = (CMP_STRIDE * n + (CMP_LEN - 1)) <= t
    maskf = mask_c.astype(jnp.float32)
    kc = kc_ref[...]
    vc = vc_ref[...]
    lane = lax.broadcasted_iota(jnp.int32, (tq, LANES), 1)
    psum = jnp.zeros((tq, N_CMP_PAD), jnp.float32)
    for j in range(NSA_HPG // 2):
        q2 = q_ref[:, j * LANES:(j + 1) * LANES]
        pv = []
        for hh in range(2):
            qm = jnp.where(_half_mask(q2.shape, hh), q2, jnp.zeros_like(q2))
            s = jnp.where(mask_c, _dot_t(qm, kc), NEG_INF)
            e = jnp.exp(s - jnp.max(s, axis=1, keepdims=True))
            p = e / jnp.sum(e, axis=1, keepdims=True) * maskf
            psum = psum + p
            pv.append(_dot(p.astype(jnp.bfloat16), vc))
        oc_ref[:, j * LANES:(j + 1) * LANES] = jnp.where(lane < HEAD_DIM, pv[0], pv[1])
    p_hi = psum.astype(jnp.bfloat16)
    p_lo = (psum - p_hi.astype(jnp.float32)).astype(jnp.bfloat16)
    imp = _dot(p_hi, ov_ref[...]) + _dot(p_lo, ov_ref[...])
    cur = jnp.right_shift(t, SEL_SHIFT)
    valid = lane * SEL_BLOCK <= t
    forced = (lane == 0) | (lane == cur) | (lane == cur - 1)
    score = jnp.where(valid, jnp.where(forced, FORCE_SCORE, imp), NEG_INF)
    lane_f = lane.astype(jnp.float32)
    sel = jnp.zeros((tq, LANES), jnp.float32)
    for _ in range(SEL_TOPK):
        mx = jnp.max(score, axis=1, keepdims=True)
        first = jnp.min(jnp.where(score == mx, lane_f, float(LANES)), axis=1, keepdims=True)
        hit = lane_f == first
        sel = jnp.where(hit, 1.0, sel)
        score = jnp.where(hit, -jnp.inf, score)
    negsel_ref[...] = jnp.where(sel > 0.5, 0.0, NEG_INF).astype(negsel_ref.dtype)


def _cmp_topk(nsa_q, kvc, overlap, batch, seq):
    m = nsa_q.shape[0]
    nq = seq // CMP_TQ
    sq = None
    return pl.pallas_call(
        _cmp_topk_kernel,
        grid=(batch, NSA_KV_GROUPS, nq),
        in_specs=[pl.BlockSpec((CMP_TQ, NSA_HPG * HEAD_DIM), lambda b, g, i: (b * nq + i, g)),
                  pl.BlockSpec((sq, sq, sq, N_CMP_PAD, LANES), lambda b, g, i: (0, b, g, 0, 0)),
                  pl.BlockSpec((sq, sq, sq, N_CMP_PAD, LANES), lambda b, g, i: (1, b, g, 0, 0)),
                  pl.BlockSpec((N_CMP_PAD, LANES), lambda b, g, i: (0, 0))],
        out_specs=[pl.BlockSpec((CMP_TQ, NSA_HPG * HEAD_DIM), lambda b, g, i: (b * nq + i, g)),
                   pl.BlockSpec((sq, sq, CMP_TQ, LANES), lambda b, g, i: (b, g, i, 0))],
        out_shape=[jax.ShapeDtypeStruct((m, NSA_W), jnp.float32),
                   jax.ShapeDtypeStruct((batch, NSA_KV_GROUPS, seq, LANES), jnp.bfloat16)],
        compiler_params=_cparams(("parallel", "parallel", "parallel")),
        name="nsa_cmp_topk",
    )(nsa_q, kvc, kvc, overlap)


def _online_step(s, v, m_ref, l_ref, acc_ref, idx):
    m_old = m_ref[idx]
    m_new = jnp.maximum(m_old, jnp.max(s, axis=1, keepdims=True))
    alpha = jnp.exp(m_old - m_new)
    p = jnp.exp(s - m_new)
    l_ref[idx] = alpha * l_ref[idx] + jnp.sum(p, axis=1, keepdims=True)
    acc_ref[idx] = alpha * acc_ref[idx] + _dot(p.astype(jnp.bfloat16), v)
    m_ref[idx] = m_new


def _init_state(m_ref, l_ref, acc_ref):
    m_ref[...] = jnp.full(m_ref.shape, NEG_INF, jnp.float32)
    l_ref[...] = jnp.zeros(l_ref.shape, jnp.float32)
    acc_ref[...] = jnp.zeros(acc_ref.shape, jnp.float32)


def _causal(t):
    r = lax.broadcasted_iota(jnp.int32, (t, t), 0)
    c = lax.broadcasted_iota(jnp.int32, (t, t), 1)
    return c <= r


def _pick_col(x, col):
    lane = lax.broadcasted_iota(jnp.int32, x.shape, 1)
    return jnp.sum(jnp.where(lane == col, x, 0.0), axis=1, keepdims=True)


def _nsa_attn_kernel(q_ref, ksel_ref, vsel_ref, kwin_ref, vwin_ref, negsel_ref, oc_ref, gate_ref, z_ref,
                     o_ref, qcat_ref, m_ref, l_ref, acc_ref):
    t = ATT_T
    g = pl.program_id(1)
    j = pl.program_id(2)
    qi = pl.program_id(3)
    q2 = q_ref[...]
    negsel = negsel_ref[...]
    for hh in range(2):
        qm = jnp.where(_half_mask(q2.shape, hh), q2, jnp.zeros_like(q2))
        qcat_ref[hh] = jnp.concatenate([qm, negsel], axis=1)
    _init_state(m_ref, l_ref, acc_ref)
    causal = _causal(t)
    blocks_per_tile = t // SEL_BLOCK

    def sel_tile(kt, diag):
        start = pl.multiple_of(kt * t, t)
        ks = ksel_ref[pl.ds(start, t), :]
        vs = vsel_ref[pl.ds(start, t), :]
        key_block = (jnp.right_shift(lax.broadcasted_iota(jnp.int32, (t, LANES), 0), SEL_SHIFT)
                     + kt * blocks_per_tile)
        onehot = jnp.where(key_block == lax.broadcasted_iota(jnp.int32, (t, LANES), 1),
                           1.0, 0.0).astype(jnp.bfloat16)
        kcat = jnp.concatenate([ks, onehot], axis=1)
        for hh in range(2):
            s = _dot_t(qcat_ref[hh], kcat)
            if diag:
                s = jnp.where(causal, s, NEG_INF)
            _online_step(s, vs, m_ref, l_ref, acc_ref, hh)

    sel_tile(qi, True)

    def body(kt, carry):
        sel_tile(kt, False)
        return carry

    lax.fori_loop(0, qi, body, 0)
    o_sel = [acc_ref[hh] / l_ref[hh] for hh in range(2)]

    _init_state(m_ref, l_ref, acc_ref)
    d0 = pl.multiple_of(qi * t, t)
    p0 = pl.multiple_of(jnp.maximum(qi - 1, 0) * t, t)
    r_i = lax.broadcasted_iota(jnp.int32, (t, t), 0)
    c_i = lax.broadcasted_iota(jnp.int32, (t, t), 1)
    prev_ok = c_i > r_i + jnp.where(qi > 0, 0, t)
    for hh in range(2):
        qm = qcat_ref[hh][:, :LANES]
        s = jnp.where(causal, _dot_t(qm, kwin_ref[pl.ds(d0, t), :]), NEG_INF)
        _online_step(s, vwin_ref[pl.ds(d0, t), :], m_ref, l_ref, acc_ref, hh)
        s = jnp.where(prev_ok, _dot_t(qm, kwin_ref[pl.ds(p0, t), :]), NEG_INF)
        _online_step(s, vwin_ref[pl.ds(p0, t), :], m_ref, l_ref, acc_ref, hh)
    o_win = [acc_ref[hh] / l_ref[hh] for hh in range(2)]

    gates = gate_ref[...]
    head0 = (g * NSA_HPG + j * 2) * 3
    comb = []
    for hh in range(2):
        base = head0 + hh * 3
        comb.append((_pick_col(gates, base), _pick_col(gates, base + 1) * o_sel[hh]
                     + _pick_col(gates, base + 2) * o_win[hh]))
    lane = lax.broadcasted_iota(jnp.int32, (t, LANES), 1)
    low = lane < HEAD_DIM
    out = jnp.where(low, comb[0][0], comb[1][0]) * oc_ref[...] + jnp.where(low, comb[0][1], comb[1][1])
    o_ref[...] = (out * z_ref[...]).astype(o_ref.dtype)


def _nsa_attn(p, negsel, o_cmp, batch, seq):
    m = p['nsa_q'].shape[0]
    t = ATT_T
    nq = seq // t
    sq = None
    row_blk = lambda b, g, j, i: (b * nq + i, g * 2 + j)
    kv_spec = pl.BlockSpec((seq, LANES), lambda b, g, j, i: (b, g))
    return pl.pallas_call(
        _nsa_attn_kernel,
        grid=(batch, NSA_KV_GROUPS, NSA_HPG // 2, nq),
        in_specs=[pl.BlockSpec((t, LANES), row_blk),
                  kv_spec, kv_spec, kv_spec, kv_spec,
                  pl.BlockSpec((sq, sq, t, LANES), lambda b, g, j, i: (b, g, i, 0)),
                  pl.BlockSpec((t, LANES), row_blk),
                  pl.BlockSpec((t, LANES), lambda b, g, j, i: (b * nq + i, 0)),
                  pl.BlockSpec((t, LANES), row_blk)],
        out_specs=pl.BlockSpec((t, LANES), row_blk),
        out_shape=jax.ShapeDtypeStruct((m, NSA_W), jnp.bfloat16),
        scratch_shapes=[pltpu.VMEM((2, t, 2 * LANES), jnp.bfloat16),
                        pltpu.VMEM((2, t, 1), jnp.float32),
                        pltpu.VMEM((2, t, 1), jnp.float32),
                        pltpu.VMEM((2, t, LANES), jnp.float32)],
        compiler_params=_cparams(("parallel", "parallel", "parallel", "arbitrary")),
        name="nsa_sel_win_attn",
    )(p['nsa_q'], p['nsa_k_sel'], p['nsa_v_sel'], p['nsa_k_win'], p['nsa_v_win'], negsel, o_cmp,
      p['nsa_gate'], p['nsa_z'])


def _fox_kernel(q_ref, k_ref, v_ref, ccol_ref, crow_ref, z_ref, o_ref, m_ref, l_ref, acc_ref):
    t = ATT_T
    j = pl.program_id(1)
    qi = pl.program_id(2)
    q2 = q_ref[...]
    qm = [jnp.where(_half_mask(q2.shape, hh), q2, jnp.zeros_like(q2)) for hh in range(2)]
    ccol = ccol_ref[...]
    cq = [_pick_col(ccol, j * 2 + hh) for hh in range(2)]
    _init_state(m_ref, l_ref, acc_ref)
    causal = _causal(t)

    def tile(kt, diag):
        start = pl.multiple_of(kt * t, t)
        k = k_ref[pl.ds(start, t), :]
        v = v_ref[pl.ds(start, t), :]
        for hh in range(2):
            ck = crow_ref[hh:hh + 1, pl.ds(start, t)]
            s = _dot_t(qm[hh], k) + cq[hh] - ck
            if diag:
                s = jnp.where(causal, s, NEG_INF)
            _online_step(s, v, m_ref, l_ref, acc_ref, hh)

    tile(qi, True)

    def body(kt, carry):
        tile(kt, False)
        return carry

    lax.fori_loop(0, qi, body, 0)
    lane = lax.broadcasted_iota(jnp.int32, (t, LANES), 1)
    out = jnp.where(lane < HEAD_DIM, acc_ref[0] / l_ref[0], acc_ref[1] / l_ref[1])
    o_ref[...] = (out * z_ref[...]).astype(o_ref.dtype)


def _fox_attn(p, c_col, c_row, batch, seq):
    m = p['fox_q'].shape[0]
    t = ATT_T
    nq = seq // t
    sq = None
    row_blk = lambda b, j, i: (b * nq + i, j)
    kv_spec = pl.BlockSpec((seq, LANES), lambda b, j, i: (b, j))
    return pl.pallas_call(
        _fox_kernel,
        grid=(batch, FOX_HEADS // 2, nq),
        in_specs=[pl.BlockSpec((t, LANES), row_blk), kv_spec, kv_spec,
                  pl.BlockSpec((t, LANES), lambda b, j, i: (b * nq + i, 0)),
                  pl.BlockSpec((sq, sq, 8, seq), lambda b, j, i: (b, j, 0, 0)),
                  pl.BlockSpec((t, LANES), row_blk)],
        out_specs=pl.BlockSpec((t, LANES), row_blk),
        out_shape=jax.ShapeDtypeStruct((m, FOX_W), jnp.bfloat16),
        scratch_shapes=[pltpu.VMEM((2, t, 1), jnp.float32),
                        pltpu.VMEM((2, t, 1), jnp.float32),
                        pltpu.VMEM((2, t, LANES), jnp.float32)],
        compiler_params=_cparams(("parallel", "parallel", "arbitrary")),
        name="fox_attn",
    )(p['fox_q'], p['fox_k'], p['fox_v'], c_col, c_row, p['fox_z'])


def _diff_kernel(lam_init, q_ref, k_ref, v_ref, z_ref, lam_ref, g_ref, o_ref, m_ref, l_ref, acc_ref):
    t = ATT_T
    qi = pl.program_id(2)
    q2 = q_ref[...]
    lane = lax.broadcasted_iota(jnp.int32, (t, LANES), 1)
    qm = []
    for hh in range(2):
        for c in range(2):
            lo = hh * HEAD_DIM + c * DIFF_QK_DIM
            qm.append(jnp.where((lane >= lo) & (lane < lo + DIFF_QK_DIM), q2, jnp.zeros_like(q2)))
    _init_state(m_ref, l_ref, acc_ref)
    causal = _causal(t)
    scale = DIFF_QK_DIM ** -0.5

    def tile(kt, diag):
        start = pl.multiple_of(kt * t, t)
        k = k_ref[pl.ds(start, t), :]
        v = v_ref[pl.ds(start, t), :]
        for idx in range(4):
            s = _dot_t(qm[idx], k) * scale
            if diag:
                s = jnp.where(causal, s, NEG_INF)
            _online_step(s, v, m_ref, l_ref, acc_ref, idx)

    tile(qi, True)

    def body(kt, carry):
        tile(kt, False)
        return carry

    lax.fori_loop(0, qi, body, 0)

    lam_rows = lam_ref[...]
    lam = (jnp.exp(jnp.sum(lam_rows[0:1, :] * lam_rows[1:2, :], axis=1, keepdims=True))
           - jnp.exp(jnp.sum(lam_rows[2:3, :] * lam_rows[3:4, :], axis=1, keepdims=True)) + lam_init)
    o = [acc_ref[idx] / l_ref[idx] for idx in range(4)]
    low = lane < HEAD_DIM
    d = jnp.where(low, o[0] - lam * o[1], o[2] - lam * o[3])
    dsq = d * d
    ms0 = jnp.sum(jnp.where(low, dsq, 0.0), axis=1, keepdims=True) / HEAD_DIM
    ms1 = jnp.sum(jnp.where(low, 0.0, dsq), axis=1, keepdims=True) / HEAD_DIM
    inv = jnp.where(low, lax.rsqrt(ms0 + RMS_EPS), lax.rsqrt(ms1 + RMS_EPS))
    out = d * inv * g_ref[...] * (1.0 - lam_init)
    o_ref[...] = (out * z_ref[...]).astype(o_ref.dtype)


def _diff_attn(p, lam_rows, g_row, lam_init, batch, seq):
    m = p['diff_q'].shape[0]
    t = ATT_T
    nq = seq // t
    row_blk = lambda b, j, i: (b * nq + i, j)
    kv_spec = pl.BlockSpec((seq, LANES), lambda b, j, i: (b, j))
    return pl.pallas_call(
        partial(_diff_kernel, lam_init),
        grid=(batch, DIFF_HEADS // 2, nq),
        in_specs=[pl.BlockSpec((t, LANES), row_blk), kv_spec, kv_spec,
                  pl.BlockSpec((t, LANES), row_blk),
                  pl.BlockSpec((8, LANES), lambda b, j, i: (0, 0)),
                  pl.BlockSpec((1, LANES), lambda b, j, i: (0, 0))],
        out_specs=pl.BlockSpec((t, LANES), row_blk),
        out_shape=jax.ShapeDtypeStruct((m, DIFF_W), jnp.bfloat16),
        scratch_shapes=[pltpu.VMEM((4, t, 1), jnp.float32),
                        pltpu.VMEM((4, t, 1), jnp.float32),
                        pltpu.VMEM((4, t, LANES), jnp.float32)],
        compiler_params=_cparams(("parallel", "parallel", "arbitrary")),
        name="diff_attn",
    )(p['diff_q'], p['diff_k'], p['diff_v'], p['diff_z'], lam_rows, g_row)


def _out_kernel(of_ref, on_ref, od_ref, w_ref, x_ref, g_ref, b_ref, o_ref):
    y = (DEEPNORM_ALPHA * x_ref[...]
         + _dot(of_ref[...], w_ref[0:FOX_W, :])
         + _dot(on_ref[...], w_ref[FOX_W:FOX_W + NSA_W, :])
         + _dot(od_ref[...], w_ref[FOX_W + NSA_W:, :]))
    mu = jnp.mean(y, axis=1, keepdims=True)
    yc = y - mu
    var = jnp.mean(yc * yc, axis=1, keepdims=True)
    o_ref[...] = yc * lax.rsqrt(var + LN_EPS) * g_ref[...] + b_ref[...]


def _out_proj(o_fox, o_nsa, o_diff, w_out, x2d, g_row, b_row):
    m = x2d.shape[0]
    tm = PROJ_TM
    row = lambda i: (i, 0)
    const = lambda i: (0, 0)
    return pl.pallas_call(
        _out_kernel,
        grid=(m // tm,),
        in_specs=[pl.BlockSpec((tm, FOX_W), row), pl.BlockSpec((tm, NSA_W), row),
                  pl.BlockSpec((tm, DIFF_W), row),
                  pl.BlockSpec((D_MODEL, D_MODEL), const),
                  pl.BlockSpec((tm, D_MODEL), row),
                  pl.BlockSpec((1, D_MODEL), const), pl.BlockSpec((1, D_MODEL), const)],
        out_specs=pl.BlockSpec((tm, D_MODEL), row),
        out_shape=jax.ShapeDtypeStruct((m, D_MODEL), jnp.float32),
        compiler_params=_cparams(("parallel",)),
        name="out_proj_ln",
    )(o_fox, o_nsa, o_diff, w_out, x2d, g_row, b_row)


def _overlap_matrix(seq):
    n_slc = seq // SEL_BLOCK
    cmp_start = CMP_STRIDE * np.arange(N_CMP_PAD)
    sel_start = SEL_BLOCK * np.arange(n_slc)
    ov = np.clip(np.minimum(cmp_start[:, None] + CMP_LEN, sel_start[None, :] + SEL_BLOCK)
                 - np.maximum(cmp_start[:, None], sel_start[None, :]), 0, None).astype(np.float32) / CMP_LEN
    return ov


def kernel(x, w_in, b_fox_f, cmp_pos_k, cmp_pos_v, cmp_w1_k, cmp_w2_k, cmp_w1_v, cmp_w2_v,
           lam_q1, lam_k1, lam_q2, lam_k2, diff_subln_g, w_out, ln_g, ln_b):
    batch, seq, _ = x.shape
    assert seq % ATT_T == 0 and seq // SEL_BLOCK == LANES and (seq - CMP_LEN) // CMP_STRIDE + 1 < N_CMP_PAD
    assert WINDOW == ATT_T
    m = batch * seq
    tables = _rope_tables(seq, HEAD_DIM) + _rope_tables(seq, DIFF_QK_DIM)
    overlap = jnp.asarray(_overlap_matrix(seq), jnp.bfloat16)
    col_idx = jnp.asarray(np.maximum(PROJ_COLS, 0))
    col_live = jnp.asarray(PROJ_COLS >= 0)
    n_chunks = seq // CMP_STRIDE

    x2d = x.reshape(m, D_MODEL)
    for l in range(DEPTH):
        w_perm = jnp.where(col_live[None, :], jnp.take(w_in[l], col_idx, axis=1), 0.0).astype(jnp.bfloat16)
        bias_row = jnp.zeros((1, LANES), jnp.float32).at[0, :FOX_HEADS].set(b_fox_f[l])
        p = _project(x2d, w_perm, bias_row, tables, seq)

        c_col = _cumsum(p['fox_logf'], batch, seq)
        c_row = (c_col[:, :FOX_HEADS].reshape(batch, seq, FOX_HEADS // 2, 2).transpose(0, 2, 3, 1))
        c_row = jnp.pad(c_row, ((0, 0), (0, 0), (0, 6), (0, 0)))
        o_fox = _fox_attn(p, c_col, c_row, batch, seq)

        def chunked(a):
            a = a.reshape(batch, n_chunks, CMP_STRIDE, NSA_KV_GROUPS, HEAD_DIM)
            return a.transpose(0, 3, 1, 2, 4).reshape(batch, NSA_KV_GROUPS, n_chunks, CMP_STRIDE * HEAD_DIM)

        chunks = jnp.stack([chunked(p['nsa_k_cmp']), chunked(p['nsa_v_cmp'])])
        pos = jnp.stack([cmp_pos_k[l].reshape(1, -1), cmp_pos_v[l].reshape(1, -1)])
        w1 = jnp.stack([cmp_w1_k[l], cmp_w1_v[l]]).astype(jnp.bfloat16)
        w2d = jnp.stack([jnp.concatenate([cmp_w2_k[l]] * 2, axis=1),
                         jnp.concatenate([cmp_w2_v[l]] * 2, axis=1)]).astype(jnp.bfloat16)
        kvc = _compress(chunks, pos, w1, w2d)
        o_cmp, negsel = _cmp_topk(p['nsa_q'], kvc, overlap, batch, seq)
        o_nsa = _nsa_attn(p, negsel, o_cmp, batch, seq)

        lam_init = 0.8 - 0.6 * math.exp(-0.3 * l)
        lam_rows = jnp.zeros((8, LANES), jnp.float32)
        for r, v in enumerate((lam_q1[l], lam_k1[l], lam_q2[l], lam_k2[l])):
            lam_rows = lam_rows.at[r, :DIFF_QK_DIM].set(v.astype(jnp.float32))
        g_row = jnp.tile(diff_subln_g[l].reshape(1, HEAD_DIM), (1, 2))
        o_diff = _diff_attn(p, lam_rows, g_row, lam_init, batch, seq)

        x2d = _out_proj(o_fox, o_nsa, o_diff, w_out[l].astype(jnp.bfloat16), x2d,
                        ln_g[l].reshape(1, D_MODEL), ln_b[l].reshape(1, D_MODEL))
    return x2d.reshape(batch, seq, D_MODEL)
```

```python
import math
from functools import partial

import numpy as np
import jax
import jax.numpy as jnp
from jax import lax
from jax.experimental import pallas as pl
from jax.experimental.pallas import tpu as pltpu

D_MODEL = 1024
DEPTH = 2
HEAD_DIM = 64
FOX_HEADS = 4
NSA_HEADS = 8
DIFF_HEADS = 4
FOX_W = FOX_HEADS * HEAD_DIM
NSA_W = NSA_HEADS * HEAD_DIM
DIFF_QK_DIM = HEAD_DIM // 2
DIFF_W = DIFF_HEADS * HEAD_DIM
NSA_KV_GROUPS = 2
NSA_HPG = NSA_HEADS // NSA_KV_GROUPS
NSA_KV_W = NSA_KV_GROUPS * HEAD_DIM
CMP_LEN = 32
CMP_STRIDE = 16
CMP_HIDDEN = 2 * HEAD_DIM
SEL_BLOCK = 64
SEL_SHIFT = SEL_BLOCK.bit_length() - 1
SEL_TOPK = 16
WINDOW = 512
ROPE_THETA = 10000.0
LN_EPS = 1e-5
RMS_EPS = 1e-5
NEG_INF = -1e30
FORCE_SCORE = 1e30
DEEPNORM_ALPHA = (2 * DEPTH) ** 0.25
LOG2E = math.log2(math.e)

SPLITS = (
    ('fox_q', FOX_W), ('fox_k', FOX_W), ('fox_v', FOX_W), ('fox_f', FOX_HEADS), ('fox_z', FOX_W),
    ('nsa_q', NSA_W),
    ('nsa_k_cmp', NSA_KV_W), ('nsa_v_cmp', NSA_KV_W),
    ('nsa_k_sel', NSA_KV_W), ('nsa_v_sel', NSA_KV_W),
    ('nsa_k_win', NSA_KV_W), ('nsa_v_win', NSA_KV_W),
    ('nsa_gate', 3 * NSA_HEADS), ('nsa_z', NSA_W),
    ('diff_q', DIFF_W), ('diff_k', DIFF_W), ('diff_v', DIFF_W), ('diff_z', DIFF_W),
)
_OFF = {}
_acc = 0
for _n, _w in SPLITS:
    _OFF[_n] = _acc
    _acc += _w
IN_WIDTH = _acc

LANES = 128
VMEM_LIMIT = 56 * 1024 * 1024

PROJ_TM = 256
CUM_T = 512
CMP_TQ = 256
ATT_TQ = 1024
ATT_TK = 512
N_CMP_PAD = 512
FOX_CAT = 2 * LANES


def _cparams(sem):
    return pltpu.CompilerParams(dimension_semantics=sem, vmem_limit_bytes=VMEM_LIMIT)


def _dot(a, b):
    return jnp.dot(a, b, preferred_element_type=jnp.float32)


def _dot_t(a, b):
    return lax.dot_general(a, b, (((1,), (1,)), ((), ())), preferred_element_type=jnp.float32)


def _split3(x):
    hi = x.astype(jnp.bfloat16)
    r1 = x - hi.astype(jnp.float32)
    mid = r1.astype(jnp.bfloat16)
    lo = (r1 - mid.astype(jnp.float32)).astype(jnp.bfloat16)
    return hi, mid, lo


def _dup(name):
    base = _OFF[name]
    cols = []
    for g in range(NSA_KV_GROUPS):
        one = list(range(base + g * HEAD_DIM, base + (g + 1) * HEAD_DIM))
        cols += one + one
    return cols


def _v_ones(name, heads):
    base = _OFF[name]
    cols = []
    for h in range(heads):
        cols += list(range(base + h * HEAD_DIM, base + (h + 1) * HEAD_DIM)) + [-1] * HEAD_DIM
    return cols


def _pad(cols, width):
    return list(cols) + [-1] * (width - len(cols))


def _rng(name, width):
    return list(range(_OFF[name], _OFF[name] + width))


Q_SCALE = HEAD_DIM ** -0.5 * LOG2E
DIFF_Q_SCALE = DIFF_QK_DIM ** -0.5 * LOG2E

PROJ_SEGS = (
    ('fox_q', _rng('fox_q', FOX_W), 'none', jnp.bfloat16, Q_SCALE),
    ('fox_k', _rng('fox_k', FOX_W), 'none', jnp.bfloat16, None),
    ('fox_v', _v_ones('fox_v', FOX_HEADS), 'ones_hi', jnp.bfloat16, None),
    ('fox_z', _rng('fox_z', FOX_W), 'silu', jnp.float32, None),
    ('fox_logf', _pad(_rng('fox_f', FOX_HEADS), LANES), 'logsig', jnp.float32, None),
    ('nsa_q', _rng('nsa_q', NSA_W), 'rope64', jnp.bfloat16, Q_SCALE),
    ('nsa_k_cmp', _rng('nsa_k_cmp', NSA_KV_W), 'rope64', jnp.float32, None),
    ('nsa_v_cmp', _rng('nsa_v_cmp', NSA_KV_W), 'none', jnp.float32, None),
    ('nsa_k_sel', _dup('nsa_k_sel'), 'rope64', jnp.bfloat16, None),
    ('nsa_v_sel', _v_ones('nsa_v_sel', NSA_KV_GROUPS), 'ones_hi', jnp.bfloat16, None),
    ('nsa_k_win', _dup('nsa_k_win'), 'rope64', jnp.bfloat16, None),
    ('nsa_v_win', _v_ones('nsa_v_win', NSA_KV_GROUPS), 'ones_hi', jnp.bfloat16, None),
    ('nsa_gate', _pad(_rng('nsa_gate', 3 * NSA_HEADS), LANES), 'sigmoid', jnp.float32, None),
    ('nsa_z', _rng('nsa_z', NSA_W), 'silu', jnp.float32, None),
    ('diff_q', _rng('diff_q', DIFF_W), 'rope32', jnp.bfloat16, DIFF_Q_SCALE),
    ('diff_k', _rng('diff_k', DIFF_W), 'rope32', jnp.bfloat16, None),
    ('diff_v', _v_ones('diff_v', DIFF_HEADS), 'ones_hi', jnp.bfloat16, None),
    ('diff_z', _rng('diff_z', DIFF_W), 'silu', jnp.float32, None),
)
PROJ_COLS = np.concatenate([np.asarray(seg[1], np.int32) for seg in PROJ_SEGS])
PROJ_N = int(PROJ_COLS.shape[0])


def _rope_chunk(y, cos, sin, half):
    lane = lax.broadcasted_iota(jnp.int32, y.shape, 1)
    first = (lane & (2 * half - 1)) < half
    partner = jnp.where(first, pltpu.roll(y, LANES - half, 1), pltpu.roll(y, half, 1))
    return y * cos + partner * sin


def _proj_kernel(x_ref, w_ref, bias_ref, cos64_ref, sin64_ref, cos32_ref, sin32_ref, *out_refs):
    xb = x_ref[...].astype(jnp.bfloat16)
    start = 0
    for (name, cols, epi, dtype, scale), o_ref in zip(PROJ_SEGS, out_refs):
        width = len(cols)
        y = _dot(xb, w_ref[:, start:start + width])
        start += width
        if epi == 'ones_hi':
            lane = lax.broadcasted_iota(jnp.int32, y.shape, 1)
            y = jnp.where((lane & HEAD_DIM) != 0, 1.0, y)
        elif epi == 'silu':
            y = y * jax.nn.sigmoid(y)
        elif epi == 'sigmoid':
            y = jax.nn.sigmoid(y)
        elif epi == 'logsig':
            y = y + bias_ref[...]
            y = jnp.minimum(y, 0.0) - jnp.log1p(jnp.exp(-jnp.abs(y)))
        elif epi in ('rope64', 'rope32'):
            half = DIFF_QK_DIM // 2 if epi == 'rope32' else HEAD_DIM // 2
            cos = cos32_ref[...] if epi == 'rope32' else cos64_ref[...]
            sin = sin32_ref[...] if epi == 'rope32' else sin64_ref[...]
            chunks = [_rope_chunk(y[:, c * LANES:(c + 1) * LANES], cos, sin, half)
                      for c in range(width // LANES)]
            y = chunks[0] if len(chunks) == 1 else jnp.concatenate(chunks, axis=1)
        if scale is not None:
            y = y * scale
        o_ref[...] = y.astype(dtype)


def _rope_tables(seq, group):
    half = group // 2
    inv = ROPE_THETA ** (-(jnp.arange(half, dtype=jnp.float32) * 2.0 / group))
    ang = jnp.arange(seq, dtype=jnp.float32)[:, None] * inv[None, :]
    cos, sin = jnp.cos(ang), jnp.sin(ang)
    reps = LANES // group
    cos_t = jnp.tile(jnp.concatenate([cos, cos], axis=1), (1, reps))
    sin_t = jnp.tile(jnp.concatenate([-sin, sin], axis=1), (1, reps))
    return cos_t, sin_t


def _project(x2d, w_perm, bias_row, tables, seq):
    m = x2d.shape[0]
    nt = seq // PROJ_TM
    tab_spec = pl.BlockSpec((PROJ_TM, LANES), lambda i: (i % nt, 0))
    out_shape = [jax.ShapeDtypeStruct((m, len(seg[1])), seg[3]) for seg in PROJ_SEGS]
    out_specs = [pl.BlockSpec((PROJ_TM, len(seg[1])), lambda i: (i, 0)) for seg in PROJ_SEGS]
    outs = pl.pallas_call(
        _proj_kernel,
        grid=(m // PROJ_TM,),
        in_specs=[pl.BlockSpec((PROJ_TM, D_MODEL), lambda i: (i, 0)),
                  pl.BlockSpec((D_MODEL, PROJ_N), lambda i: (0, 0)),
                  pl.BlockSpec((1, LANES), lambda i: (0, 0)),
                  tab_spec, tab_spec, tab_spec, tab_spec],
        out_specs=out_specs,
        out_shape=out_shape,
        compiler_params=_cparams(("parallel",)),
        name="in_proj",
    )(x2d, w_perm, bias_row, *tables)
    return {seg[0]: o for seg, o in zip(PROJ_SEGS, outs)}


def _fox_prep_kernel(lf_ref, q_ref, k_ref, qcat_ref, kcat_ref, carry_ref):
    @pl.when(pl.program_id(1) == 0)
    def _():
        carry_ref[...] = jnp.zeros_like(carry_ref)

    t = CUM_T
    row = lax.broadcasted_iota(jnp.int32, (t, t), 0)
    col = lax.broadcasted_iota(jnp.int32, (t, t), 1)
    tri = jnp.where(col <= row, 1.0, 0.0).astype(jnp.bfloat16)
    hi, mid, lo = _split3(lf_ref[...])
    c = _dot(tri, hi) + _dot(tri, mid) + _dot(tri, lo) + carry_ref[0:1, :]
    carry_ref[0:1, :] = c[t - 1:t, :]

    terms = _split3(c * LOG2E)
    src = lax.broadcasted_iota(jnp.int32, (LANES, FOX_HEADS * LANES), 0)
    dst = lax.broadcasted_iota(jnp.int32, (LANES, FOX_HEADS * LANES), 1)

    def spread(x, slot):
        place = jnp.where(dst == src * LANES + slot, 1.0, 0.0).astype(jnp.bfloat16)
        return _dot(x, place)

    n_terms = len(terms)
    qa = sum(spread(x, i) for i, x in enumerate(terms))
    ka = -sum(spread(x, n_terms + i) for i, x in enumerate(terms))
    lane = lax.broadcasted_iota(jnp.int32, qa.shape, 1) & (LANES - 1)
    qa = jnp.where((lane >= n_terms) & (lane < 2 * n_terms), 1.0, qa).astype(jnp.bfloat16)
    ka = jnp.where(lane < n_terms, 1.0, ka).astype(jnp.bfloat16)
    q = q_ref[...]
    k = k_ref[...]
    for h in range(FOX_HEADS):
        pair, hh = divmod(h, 2)
        q2 = q[:, pair * LANES:(pair + 1) * LANES]
        base = h * FOX_CAT
        qcat_ref[:, base:base + LANES] = jnp.where(_half_mask(q2.shape, hh), q2, jnp.zeros_like(q2))
        qcat_ref[:, base + LANES:base + FOX_CAT] = qa[:, h * LANES:(h + 1) * LANES]
        kcat_ref[:, base:base + LANES] = k[:, pair * LANES:(pair + 1) * LANES]
        kcat_ref[:, base + LANES:base + FOX_CAT] = ka[:, h * LANES:(h + 1) * LANES]


def _fox_prep(p, batch, seq):
    m = p['fox_q'].shape[0]
    nt = seq // CUM_T
    row = lambda b, j: (b * nt + j, 0)
    cat = jax.ShapeDtypeStruct((m, FOX_HEADS * FOX_CAT), jnp.bfloat16)
    return pl.pallas_call(
        _fox_prep_kernel,
        grid=(batch, nt),
        in_specs=[pl.BlockSpec((CUM_T, LANES), row), pl.BlockSpec((CUM_T, FOX_W), row),
                  pl.BlockSpec((CUM_T, FOX_W), row)],
        out_specs=[pl.BlockSpec((CUM_T, FOX_HEADS * FOX_CAT), row)] * 2,
        out_shape=[cat, cat],
        scratch_shapes=[pltpu.VMEM((8, LANES), jnp.float32)],
        compiler_params=_cparams(("arbitrary", "arbitrary")),
        name="fox_prep",
    )(p['fox_logf'], p['fox_q'], p['fox_k'])


def _compress_kernel(ch_ref, pos_ref, w1_ref, w2_ref, o_ref):
    ch = ch_ref[...]
    half = CMP_STRIDE * HEAD_DIM
    a_in = (ch + pos_ref[:, :half]).astype(jnp.bfloat16)
    b_in = (ch + pos_ref[:, half:]).astype(jnp.bfloat16)
    a = _dot(a_in, w1_ref[:half, :])
    b = _dot(b_in, w1_ref[half:, :])
    n_chunks = ch.shape[0]
    h = a + pltpu.roll(b, n_chunks - 1, 0)
    act = (h * jax.nn.sigmoid(h)).astype(jnp.bfloat16)
    out = _dot(act, w2_ref[...])
    row = lax.broadcasted_iota(jnp.int32, out.shape, 0)
    o_ref[...] = jnp.where(row < n_chunks - 1, out, 0.0).astype(o_ref.dtype)


def _compress(chunks, pos, w1, w2d):
    _, batch, groups, n_chunks, width = chunks.shape
    sq = None
    return pl.pallas_call(
        _compress_kernel,
        grid=(2, batch, groups),
        in_specs=[pl.BlockSpec((sq, sq, sq, n_chunks, width), lambda t, b, g: (t, b, g, 0, 0)),
                  pl.BlockSpec((sq, 1, 2 * width), lambda t, b, g: (t, 0, 0)),
                  pl.BlockSpec((sq, 2 * width, CMP_HIDDEN), lambda t, b, g: (t, 0, 0)),
                  pl.BlockSpec((sq, CMP_HIDDEN, LANES), lambda t, b, g: (t, 0, 0))],
        out_specs=pl.BlockSpec((sq, sq, sq, n_chunks, LANES), lambda t, b, g: (t, b, g, 0, 0)),
        out_shape=jax.ShapeDtypeStruct((2, batch, groups, n_chunks, LANES), jnp.bfloat16),
        compiler_params=_cparams(("parallel", "parallel", "parallel")),
        name="nsa_compress",
    )(chunks, pos, w1, w2d)


def _half_mask(shape, hh):
    lane = lax.broadcasted_iota(jnp.int32, shape, 1)
    return (lane >= hh * HEAD_DIM) & (lane < (hh + 1) * HEAD_DIM)


def _cmp_topk_kernel(q_ref, kc_ref, vc_ref, ov_ref, oc_ref, negsel_ref):
    tq = q_ref.shape[0]
    t = pl.program_id(2) * tq + lax.broadcasted_iota(jnp.int32, (tq, 1), 0)
    n = lax.broadcasted_iota(jnp.int32, (1, N_CMP_PAD), 1)
    mask_c = (CMP_STRIDE * n + (CMP_LEN - 1)) <= t
    maskf = mask_c.astype(jnp.float32)
    kc = kc_ref[...]
    vc = vc_ref[...]
    lane = lax.broadcasted_iota(jnp.int32, (tq, LANES), 1)
    psum = jnp.zeros((tq, N_CMP_PAD), jnp.float32)
    for j in range(NSA_HPG // 2):
        q2 = q_ref[:, j * LANES:(j + 1) * LANES]
        pv = []
        for hh in range(2):
            qm = jnp.where(_half_mask(q2.shape, hh), q2, jnp.zeros_like(q2))
            s = jnp.where(mask_c, _dot_t(qm, kc), NEG_INF)
            e = jnp.exp2(s - jnp.max(s, axis=1, keepdims=True))
            p = e / jnp.sum(e, axis=1, keepdims=True) * maskf
            psum = psum + p
            pv.append(_dot(p.astype(jnp.bfloat16), vc))
        oc_ref[:, j * LANES:(j + 1) * LANES] = jnp.where(lane < HEAD_DIM, pv[0], pv[1])
    p_hi = psum.astype(jnp.bfloat16)
    p_lo = (psum - p_hi.astype(jnp.float32)).astype(jnp.bfloat16)
    imp = _dot(p_hi, ov_ref[...]) + _dot(p_lo, ov_ref[...])
    cur = jnp.right_shift(t, SEL_SHIFT)
    valid = lane * SEL_BLOCK <= t
    forced = (lane == 0) | (lane == cur) | (lane == cur - 1)
    score = jnp.where(valid, jnp.where(forced, FORCE_SCORE, imp), NEG_INF)
    lane_f = lane.astype(jnp.float32)
    sel = jnp.zeros((tq, LANES), jnp.float32)
    for _ in range(SEL_TOPK):
        mx = jnp.max(score, axis=1, keepdims=True)
        first = jnp.min(jnp.where(score == mx, lane_f, float(LANES)), axis=1, keepdims=True)
        hit = lane_f == first
        sel = jnp.where(hit, 1.0, sel)
        score = jnp.where(hit, -jnp.inf, score)
    negsel_ref[...] = jnp.where(sel > 0.5, 0.0, NEG_INF).astype(negsel_ref.dtype)


def _cmp_topk(nsa_q, kvc, overlap, batch, seq):
    m = nsa_q.shape[0]
    nq = seq // CMP_TQ
    sq = None
    return pl.pallas_call(
        _cmp_topk_kernel,
        grid=(batch, NSA_KV_GROUPS, nq),
        in_specs=[pl.BlockSpec((CMP_TQ, NSA_HPG * HEAD_DIM), lambda b, g, i: (b * nq + i, g)),
                  pl.BlockSpec((sq, sq, sq, N_CMP_PAD, LANES), lambda b, g, i: (0, b, g, 0, 0)),
                  pl.BlockSpec((sq, sq, sq, N_CMP_PAD, LANES), lambda b, g, i: (1, b, g, 0, 0)),
                  pl.BlockSpec((N_CMP_PAD, LANES), lambda b, g, i: (0, 0))],
        out_specs=[pl.BlockSpec((CMP_TQ, NSA_HPG * HEAD_DIM), lambda b, g, i: (b * nq + i, g)),
                   pl.BlockSpec((sq, sq, CMP_TQ, LANES), lambda b, g, i: (b, g, i, 0))],
        out_shape=[jax.ShapeDtypeStruct((m, NSA_W), jnp.float32),
                   jax.ShapeDtypeStruct((batch, NSA_KV_GROUPS, seq, LANES), jnp.bfloat16)],
        compiler_params=_cparams(("parallel", "parallel", "parallel")),
        name="nsa_cmp_topk",
    )(nsa_q, kvc, kvc, overlap)


def _online_step(s, v, m_ref, acc_ref, idx):
    m_old = m_ref[idx]
    m_new = jnp.maximum(m_old, jnp.max(s, axis=1, keepdims=True))
    alpha = jnp.exp2(m_old - m_new)
    p = jnp.exp2(s - pltpu.repeat(m_new, s.shape[1] // LANES, 1))
    acc_ref[idx] = alpha * acc_ref[idx] + _dot(p.astype(jnp.bfloat16), v)
    m_ref[idx] = m_new


def _init_state(m_ref, acc_ref):
    m_ref[...] = jnp.full(m_ref.shape, NEG_INF, jnp.float32)
    acc_ref[...] = jnp.zeros(acc_ref.shape, jnp.float32)


def _normalized(acc):
    return acc / pltpu.roll(acc, HEAD_DIM, 1)


def _col_minus_row(rows, cols):
    return (lax.broadcasted_iota(jnp.int32, (rows, cols), 1)
            - lax.broadcasted_iota(jnp.int32, (rows, cols), 0))


def _causal_sweep(qi, tile):
    ratio = ATT_TQ // ATT_TK
    for d in range(ratio):
        tile(qi * ratio + d, -d * ATT_TK)

    def body(kt, carry):
        tile(kt, None)
        return carry

    lax.fori_loop(0, qi * ratio, body, 0)


def _pick_col(x, col):
    lane = lax.broadcasted_iota(jnp.int32, x.shape, 1)
    return jnp.sum(jnp.where(lane == col, x, 0.0), axis=1, keepdims=True)


def _nsa_attn_kernel(q_ref, ksel_ref, vsel_ref, kwin_ref, vwin_ref, negsel_ref, oc_ref, gate_ref, z_ref,
                     o_ref, qcat_ref, m_ref, acc_ref):
    tq, tk = ATT_TQ, ATT_TK
    g = pl.program_id(1)
    j = pl.program_id(2)
    qi = pl.program_id(3)
    q2 = q_ref[...]
    negsel = negsel_ref[...]
    for hh in range(2):
        qm = jnp.where(_half_mask(q2.shape, hh), q2, jnp.zeros_like(q2))
        qcat_ref[hh] = jnp.concatenate([qm, negsel], axis=1)
    _init_state(m_ref, acc_ref)
    cmr = _col_minus_row(tq, tk)
    blocks_per_tile = tk // SEL_BLOCK

    def sel_tile(kt, off):
        start = pl.multiple_of(kt * tk, tk)
        ks = ksel_ref[pl.ds(start, tk), :]
        vs = vsel_ref[pl.ds(start, tk), :]
        key_block = (jnp.right_shift(lax.broadcasted_iota(jnp.int32, (tk, LANES), 0), SEL_SHIFT)
                     + kt * blocks_per_tile)
        onehot = jnp.where(key_block == lax.broadcasted_iota(jnp.int32, (tk, LANES), 1),
                           1.0, 0.0).astype(jnp.bfloat16)
        kcat = jnp.concatenate([ks, onehot], axis=1)
        for hh in range(2):
            s = _dot_t(qcat_ref[hh], kcat)
            if off is not None:
                s = jnp.where(cmr <= off, s, NEG_INF)
            _online_step(s, vs, m_ref, acc_ref, hh)

    _causal_sweep(qi, sel_tile)
    o_sel = [_normalized(acc_ref[hh]) for hh in range(2)]

    _init_state(m_ref, acc_ref)
    cmr_w = _col_minus_row(tk, tk)
    for half in range(tq // tk):
        rows = slice(half * tk, (half + 1) * tk)
        wt = qi * (tq // tk) + half
        d0 = pl.multiple_of(wt * tk, tk)
        p0 = pl.multiple_of(jnp.maximum(wt - 1, 0) * tk, tk)
        prev_ok = cmr_w > jnp.where(wt > 0, 0, tk)
        for hh in range(2):
            qm = qcat_ref[hh, rows, :LANES]
            s = jnp.where(cmr_w <= 0, _dot_t(qm, kwin_ref[pl.ds(d0, tk), :]), NEG_INF)
            _online_step(s, vwin_ref[pl.ds(d0, tk), :], m_ref, acc_ref, (hh, rows))
            s = jnp.where(prev_ok, _dot_t(qm, kwin_ref[pl.ds(p0, tk), :]), NEG_INF)
            _online_step(s, vwin_ref[pl.ds(p0, tk), :], m_ref, acc_ref, (hh, rows))
    o_win = [_normalized(acc_ref[hh]) for hh in range(2)]

    gates = gate_ref[...]
    head0 = (g * NSA_HPG + j * 2) * 3
    comb = []
    for hh in range(2):
        base = head0 + hh * 3
        comb.append((_pick_col(gates, base), _pick_col(gates, base + 1) * o_sel[hh]
                     + _pick_col(gates, base + 2) * o_win[hh]))
    lane = lax.broadcasted_iota(jnp.int32, (tq, LANES), 1)
    low = lane < HEAD_DIM
    out = (jnp.where(low, comb[0][0], comb[1][0]) * oc_ref[...]
           + jnp.where(low, comb[0][1], pltpu.roll(comb[1][1], HEAD_DIM, 1)))
    o_ref[...] = (out * z_ref[...]).astype(o_ref.dtype)


def _nsa_attn(p, negsel, o_cmp, batch, seq):
    m = p['nsa_q'].shape[0]
    t = ATT_TQ
    nq = seq // t
    sq = None
    row_blk = lambda b, g, j, i: (b * nq + i, g * 2 + j)
    kv_spec = pl.BlockSpec((seq, LANES), lambda b, g, j, i: (b, g))
    return pl.pallas_call(
        _nsa_attn_kernel,
        grid=(batch, NSA_KV_GROUPS, NSA_HPG // 2, nq),
        in_specs=[pl.BlockSpec((t, LANES), row_blk),
                  kv_spec, kv_spec, kv_spec, kv_spec,
                  pl.BlockSpec((sq, sq, t, LANES), lambda b, g, j, i: (b, g, i, 0)),
                  pl.BlockSpec((t, LANES), row_blk),
                  pl.BlockSpec((t, LANES), lambda b, g, j, i: (b * nq + i, 0)),
                  pl.BlockSpec((t, LANES), row_blk)],
        out_specs=pl.BlockSpec((t, LANES), row_blk),
        out_shape=jax.ShapeDtypeStruct((m, NSA_W), jnp.bfloat16),
        scratch_shapes=[pltpu.VMEM((2, t, 2 * LANES), jnp.bfloat16),
                        pltpu.VMEM((2, t, LANES), jnp.float32),
                        pltpu.VMEM((2, t, LANES), jnp.float32)],
        compiler_params=_cparams(("parallel", "parallel", "parallel", "arbitrary")),
        name="nsa_sel_win_attn",
    )(p['nsa_q'], p['nsa_k_sel'], p['nsa_v_sel'], p['nsa_k_win'], p['nsa_v_win'], negsel, o_cmp,
      p['nsa_gate'], p['nsa_z'])


def _fox_kernel(q_ref, k_ref, v_ref, z_ref, o_ref, m_ref, acc_ref):
    tq, tk = ATT_TQ, ATT_TK
    qi = pl.program_id(2)
    _init_state(m_ref, acc_ref)
    cmr = _col_minus_row(tq, tk)

    def tile(kt, off):
        start = pl.multiple_of(kt * tk, tk)
        for hh in range(2):
            cat = slice(hh * FOX_CAT, (hh + 1) * FOX_CAT)
            s = _dot_t(q_ref[:, cat], k_ref[pl.ds(start, tk), cat])
            if off is not None:
                s = jnp.where(cmr <= off, s, NEG_INF)
            _online_step(s, v_ref[pl.ds(start, tk), hh * LANES:(hh + 1) * LANES], m_ref, acc_ref, hh)

    _causal_sweep(qi, tile)
    lane = lax.broadcasted_iota(jnp.int32, (tq, LANES), 1)
    out = jnp.where(lane < HEAD_DIM, _normalized(acc_ref[0]), pltpu.roll(_normalized(acc_ref[1]), HEAD_DIM, 1))
    o_ref[...] = (out * z_ref[...]).astype(o_ref.dtype)


def _fox_attn(p, qcat, kcat, batch, seq):
    m = qcat.shape[0]
    t = ATT_TQ
    nq = seq // t
    row_blk = lambda b, j, i: (b * nq + i, j)
    return pl.pallas_call(
        _fox_kernel,
        grid=(batch, FOX_HEADS // 2, nq),
        in_specs=[pl.BlockSpec((t, 2 * FOX_CAT), row_blk),
                  pl.BlockSpec((seq, 2 * FOX_CAT), lambda b, j, i: (b, j)),
                  pl.BlockSpec((seq, 2 * LANES), lambda b, j, i: (b, j)),
                  pl.BlockSpec((t, LANES), row_blk)],
        out_specs=pl.BlockSpec((t, LANES), row_blk),
        out_shape=jax.ShapeDtypeStruct((m, FOX_W), jnp.bfloat16),
        scratch_shapes=[pltpu.VMEM((2, t, LANES), jnp.float32),
                        pltpu.VMEM((2, t, LANES), jnp.float32)],
        compiler_params=_cparams(("parallel", "parallel", "arbitrary")),
        name="fox_attn",
    )(qcat, kcat, p['fox_v'], p['fox_z'])


def _diff_kernel(lam_init, q_ref, k_ref, v_ref, z_ref, lam_ref, g_ref, o_ref, qm_ref, m_ref, acc_ref):
    tq, tk = ATT_TQ, ATT_TK
    qi = pl.program_id(2)
    q2 = q_ref[...]
    lane = lax.broadcasted_iota(jnp.int32, (tq, LANES), 1)
    for idx in range(4):
        lo = idx * DIFF_QK_DIM
        qm_ref[idx] = jnp.where((lane >= lo) & (lane < lo + DIFF_QK_DIM), q2, jnp.zeros_like(q2))
    _init_state(m_ref, acc_ref)
    cmr = _col_minus_row(tq, tk)

    def tile(kt, off):
        start = pl.multiple_of(kt * tk, tk)
        k = k_ref[pl.ds(start, tk), :]
        for idx in range(4):
            hh = idx // 2
            s = _dot_t(qm_ref[idx], k)
            if off is not None:
                s = jnp.where(cmr <= off, s, NEG_INF)
            _online_step(s, v_ref[pl.ds(start, tk), hh * LANES:(hh + 1) * LANES], m_ref, acc_ref, idx)

    _causal_sweep(qi, tile)

    lam_rows = lam_ref[...]
    lam = (jnp.exp(jnp.sum(lam_rows[0:1, :] * lam_rows[1:2, :], axis=1, keepdims=True))
           - jnp.exp(jnp.sum(lam_rows[2:3, :] * lam_rows[3:4, :], axis=1, keepdims=True)) + lam_init)
    o = [_normalized(acc_ref[idx]) for idx in range(4)]
    low = lane < HEAD_DIM
    normed = []
    for hh in range(2):
        d = jnp.where(low, o[2 * hh] - lam * o[2 * hh + 1], 0.0)
        ms = jnp.sum(d * d, axis=1, keepdims=True) / HEAD_DIM
        normed.append(d * lax.rsqrt(ms + RMS_EPS))
    out = jnp.where(low, normed[0], pltpu.roll(normed[1], HEAD_DIM, 1)) * g_ref[...] * (1.0 - lam_init)
    o_ref[...] = (out * z_ref[...]).astype(o_ref.dtype)


def _diff_attn(p, lam_rows, g_row, lam_init, batch, seq):
    m = p['diff_q'].shape[0]
    t = ATT_TQ
    nq = seq // t
    row_blk = lambda b, j, i: (b * nq + i, j)
    return pl.pallas_call(
        partial(_diff_kernel, lam_init),
        grid=(batch, DIFF_HEADS // 2, nq),
        in_specs=[pl.BlockSpec((t, LANES), row_blk),
                  pl.BlockSpec((seq, LANES), lambda b, j, i: (b, j)),
                  pl.BlockSpec((seq, 2 * LANES), lambda b, j, i: (b, j)),
                  pl.BlockSpec((t, LANES), row_blk),
                  pl.BlockSpec((8, LANES), lambda b, j, i: (0, 0)),
                  pl.BlockSpec((1, LANES), lambda b, j, i: (0, 0))],
        out_specs=pl.BlockSpec((t, LANES), row_blk),
        out_shape=jax.ShapeDtypeStruct((m, DIFF_W), jnp.bfloat16),
        scratch_shapes=[pltpu.VMEM((4, t, LANES), jnp.bfloat16),
                        pltpu.VMEM((4, t, LANES), jnp.float32),
                        pltpu.VMEM((4, t, LANES), jnp.float32)],
        compiler_params=_cparams(("parallel", "parallel", "arbitrary")),
        name="diff_attn",
    )(p['diff_q'], p['diff_k'], p['diff_v'], p['diff_z'], lam_rows, g_row)


def _out_kernel(of_ref, on_ref, od_ref, w_ref, x_ref, g_ref, b_ref, o_ref):
    y = (DEEPNORM_ALPHA * x_ref[...]
         + _dot(of_ref[...], w_ref[0:FOX_W, :])
         + _dot(on_ref[...], w_ref[FOX_W:FOX_W + NSA_W, :])
         + _dot(od_ref[...], w_ref[FOX_W + NSA_W:, :]))
    mu = jnp.mean(y, axis=1, keepdims=True)
    yc = y - mu
    var = jnp.mean(yc * yc, axis=1, keepdims=True)
    o_ref[...] = yc * lax.rsqrt(var + LN_EPS) * g_ref[...] + b_ref[...]


def _out_proj(o_fox, o_nsa, o_diff, w_out, x2d, g_row, b_row):
    m = x2d.shape[0]
    tm = PROJ_TM
    row = lambda i: (i, 0)
    const = lambda i: (0, 0)
    return pl.pallas_call(
        _out_kernel,
        grid=(m // tm,),
        in_specs=[pl.BlockSpec((tm, FOX_W), row), pl.BlockSpec((tm, NSA_W), row),
                  pl.BlockSpec((tm, DIFF_W), row),
                  pl.BlockSpec((D_MODEL, D_MODEL), const),
                  pl.BlockSpec((tm, D_MODEL), row),
                  pl.BlockSpec((1, D_MODEL), const), pl.BlockSpec((1, D_MODEL), const)],
        out_specs=pl.BlockSpec((tm, D_MODEL), row),
        out_shape=jax.ShapeDtypeStruct((m, D_MODEL), jnp.float32),
        compiler_params=_cparams(("parallel",)),
        name="out_proj_ln",
    )(o_fox, o_nsa, o_diff, w_out, x2d, g_row, b_row)


def _overlap_matrix(seq):
    n_slc = seq // SEL_BLOCK
    cmp_start = CMP_STRIDE * np.arange(N_CMP_PAD)
    sel_start = SEL_BLOCK * np.arange(n_slc)
    ov = np.clip(np.minimum(cmp_start[:, None] + CMP_LEN, sel_start[None, :] + SEL_BLOCK)
                 - np.maximum(cmp_start[:, None], sel_start[None, :]), 0, None).astype(np.float32) / CMP_LEN
    return ov


def kernel(x, w_in, b_fox_f, cmp_pos_k, cmp_pos_v, cmp_w1_k, cmp_w2_k, cmp_w1_v, cmp_w2_v,
           lam_q1, lam_k1, lam_q2, lam_k2, diff_subln_g, w_out, ln_g, ln_b):
    batch, seq, _ = x.shape
    assert seq % ATT_TQ == 0 and seq // SEL_BLOCK == LANES and (seq - CMP_LEN) // CMP_STRIDE + 1 < N_CMP_PAD
    assert WINDOW == ATT_TK and ATT_TQ % ATT_TK == 0
    m = batch * seq
    tables = _rope_tables(seq, HEAD_DIM) + _rope_tables(seq, DIFF_QK_DIM)
    overlap = jnp.asarray(_overlap_matrix(seq), jnp.bfloat16)
    col_idx = jnp.asarray(np.maximum(PROJ_COLS, 0))
    col_live = jnp.asarray(PROJ_COLS >= 0)
    n_chunks = seq // CMP_STRIDE

    x2d = x.reshape(m, D_MODEL)
    for l in range(DEPTH):
        w_perm = jnp.where(col_live[None, :], jnp.take(w_in[l], col_idx, axis=1), 0.0).astype(jnp.bfloat16)
        bias_row = jnp.zeros((1, LANES), jnp.float32).at[0, :FOX_HEADS].set(b_fox_f[l])
        p = _project(x2d, w_perm, bias_row, tables, seq)

        fox_qcat, fox_kcat = _fox_prep(p, batch, seq)
        o_fox = _fox_attn(p, fox_qcat, fox_kcat, batch, seq)

        def chunked(a):
            a = a.reshape(batch, n_chunks, CMP_STRIDE, NSA_KV_GROUPS, HEAD_DIM)
            return a.transpose(0, 3, 1, 2, 4).reshape(batch, NSA_KV_GROUPS, n_chunks, CMP_STRIDE * HEAD_DIM)

        chunks = jnp.stack([chunked(p['nsa_k_cmp']), chunked(p['nsa_v_cmp'])])
        pos = jnp.stack([cmp_pos_k[l].reshape(1, -1), cmp_pos_v[l].reshape(1, -1)])
        w1 = jnp.stack([cmp_w1_k[l], cmp_w1_v[l]]).astype(jnp.bfloat16)
        w2d = jnp.stack([jnp.concatenate([cmp_w2_k[l]] * 2, axis=1),
                         jnp.concatenate([cmp_w2_v[l]] * 2, axis=1)]).astype(jnp.bfloat16)
        kvc = _compress(chunks, pos, w1, w2d)
        o_cmp, negsel = _cmp_topk(p['nsa_q'], kvc, overlap, batch, seq)
        o_nsa = _nsa_attn(p, negsel, o_cmp, batch, seq)

        lam_init = 0.8 - 0.6 * math.exp(-0.3 * l)
        lam_rows = jnp.zeros((8, LANES), jnp.float32)
        for r, v in enumerate((lam_q1[l], lam_k1[l], lam_q2[l], lam_k2[l])):
            lam_rows = lam_rows.at[r, :DIFF_QK_DIM].set(v.astype(jnp.float32))
        g_row = jnp.tile(diff_subln_g[l].reshape(1, HEAD_DIM), (1, 2))
        o_diff = _diff_attn(p, lam_rows, g_row, lam_init, batch, seq)

        x2d = _out_proj(o_fox, o_nsa, o_diff, w_out[l].astype(jnp.bfloat16), x2d,
                        ln_g[l].reshape(1, D_MODEL), ln_b[l].reshape(1, D_MODEL))
    return x2d.reshape(batch, seq, D_MODEL)
```

```python
import math
from functools import partial

import numpy as np
import jax
import jax.numpy as jnp
from jax import lax
from jax.experimental import pallas as pl
from jax.experimental.pallas import tpu as pltpu

D_MODEL = 1024
DEPTH = 2
HEAD_DIM = 64
FOX_HEADS = 4
NSA_HEADS = 8
DIFF_HEADS = 4
FOX_W = FOX_HEADS * HEAD_DIM
NSA_W = NSA_HEADS * HEAD_DIM
DIFF_QK_DIM = HEAD_DIM // 2
DIFF_W = DIFF_HEADS * HEAD_DIM
NSA_KV_GROUPS = 2
NSA_HPG = NSA_HEADS // NSA_KV_GROUPS
NSA_KV_W = NSA_KV_GROUPS * HEAD_DIM
CMP_LEN = 32
CMP_STRIDE = 16
CMP_HIDDEN = 2 * HEAD_DIM
SEL_BLOCK = 64
SEL_SHIFT = SEL_BLOCK.bit_length() - 1
SEL_TOPK = 16
WINDOW = 512
ROPE_THETA = 10000.0
LN_EPS = 1e-5
RMS_EPS = 1e-5
NEG_INF = -1e30
FORCE_SCORE = 1e30
DEEPNORM_ALPHA = (2 * DEPTH) ** 0.25
LOG2E = math.log2(math.e)

SPLITS = (
    ('fox_q', FOX_W), ('fox_k', FOX_W), ('fox_v', FOX_W), ('fox_f', FOX_HEADS), ('fox_z', FOX_W),
    ('nsa_q', NSA_W),
    ('nsa_k_cmp', NSA_KV_W), ('nsa_v_cmp', NSA_KV_W),
    ('nsa_k_sel', NSA_KV_W), ('nsa_v_sel', NSA_KV_W),
    ('nsa_k_win', NSA_KV_W), ('nsa_v_win', NSA_KV_W),
    ('nsa_gate', 3 * NSA_HEADS), ('nsa_z', NSA_W),
    ('diff_q', DIFF_W), ('diff_k', DIFF_W), ('diff_v', DIFF_W), ('diff_z', DIFF_W),
)
_OFF = {}
_acc = 0
for _n, _w in SPLITS:
    _OFF[_n] = _acc
    _acc += _w
IN_WIDTH = _acc

LANES = 128
VMEM_LIMIT = 56 * 1024 * 1024

PROJ_TM = 256
CUM_T = 512
CMP_TQ = 512
ATT_TQ = 1024
ATT_TK = 512
N_CMP_PAD = 512
FOX_CAT = 2 * LANES


def _cparams(sem):
    return pltpu.CompilerParams(dimension_semantics=sem, vmem_limit_bytes=VMEM_LIMIT)


def _dot(a, b):
    return jnp.dot(a, b, preferred_element_type=jnp.float32)


def _dot_t(a, b):
    return lax.dot_general(a, b, (((1,), (1,)), ((), ())), preferred_element_type=jnp.float32)


def _split3(x):
    hi = x.astype(jnp.bfloat16)
    r1 = x - hi.astype(jnp.float32)
    mid = r1.astype(jnp.bfloat16)
    lo = (r1 - mid.astype(jnp.float32)).astype(jnp.bfloat16)
    return hi, mid, lo


def _dup(name):
    base = _OFF[name]
    cols = []
    for g in range(NSA_KV_GROUPS):
        one = list(range(base + g * HEAD_DIM, base + (g + 1) * HEAD_DIM))
        cols += one + one
    return cols


def _v_ones(name, heads):
    base = _OFF[name]
    cols = []
    for h in range(heads):
        cols += list(range(base + h * HEAD_DIM, base + (h + 1) * HEAD_DIM)) + [-1] * HEAD_DIM
    return cols


def _pad(cols, width):
    return list(cols) + [-1] * (width - len(cols))


def _rng(name, width):
    return list(range(_OFF[name], _OFF[name] + width))


Q_SCALE = HEAD_DIM ** -0.5 * LOG2E
DIFF_Q_SCALE = DIFF_QK_DIM ** -0.5 * LOG2E

PROJ_SEGS = (
    ('fox_q', _rng('fox_q', FOX_W), 'none', jnp.bfloat16, Q_SCALE),
    ('fox_k', _rng('fox_k', FOX_W), 'none', jnp.bfloat16, None),
    ('fox_v', _v_ones('fox_v', FOX_HEADS), 'ones_hi', jnp.bfloat16, None),
    ('fox_z', _rng('fox_z', FOX_W), 'silu', jnp.float32, None),
    ('fox_logf', _pad(_rng('fox_f', FOX_HEADS), LANES), 'logsig', jnp.float32, None),
    ('nsa_q', _rng('nsa_q', NSA_W), 'rope64', jnp.bfloat16, Q_SCALE),
    ('nsa_k_cmp', _rng('nsa_k_cmp', NSA_KV_W), 'rope64', jnp.float32, None),
    ('nsa_v_cmp', _rng('nsa_v_cmp', NSA_KV_W), 'none', jnp.float32, None),
    ('nsa_k_sel', _dup('nsa_k_sel'), 'rope64', jnp.bfloat16, None),
    ('nsa_v_sel', _v_ones('nsa_v_sel', NSA_KV_GROUPS), 'ones_hi', jnp.bfloat16, None),
    ('nsa_k_win', _dup('nsa_k_win'), 'rope64', jnp.bfloat16, None),
    ('nsa_v_win', _v_ones('nsa_v_win', NSA_KV_GROUPS), 'ones_hi', jnp.bfloat16, None),
    ('nsa_gate', _pad(_rng('nsa_gate', 3 * NSA_HEADS), LANES), 'sigmoid', jnp.float32, None),
    ('nsa_z', _rng('nsa_z', NSA_W), 'silu', jnp.float32, None),
    ('diff_q', _rng('diff_q', DIFF_W), 'rope32', jnp.bfloat16, DIFF_Q_SCALE),
    ('diff_k', _rng('diff_k', DIFF_W), 'rope32', jnp.bfloat16, None),
    ('diff_v', _v_ones('diff_v', DIFF_HEADS), 'ones_hi', jnp.bfloat16, None),
    ('diff_z', _rng('diff_z', DIFF_W), 'silu', jnp.float32, None),
)
PROJ_COLS = np.concatenate([np.asarray(seg[1], np.int32) for seg in PROJ_SEGS])
PROJ_N = int(PROJ_COLS.shape[0])


def _permute_columns(w):
    pieces, start = [], 0
    cols = PROJ_COLS.tolist()
    while start < len(cols):
        end = start + 1
        if cols[start] < 0:
            while end < len(cols) and cols[end] < 0:
                end += 1
            pieces.append(jnp.zeros((w.shape[0], end - start), w.dtype))
        else:
            while end < len(cols) and cols[end] == cols[end - 1] + 1:
                end += 1
            pieces.append(w[:, cols[start]:cols[end - 1] + 1])
        start = end
    return jnp.concatenate(pieces, axis=1)


def _rope_chunk(y, cos, sin, half):
    lane = lax.broadcasted_iota(jnp.int32, y.shape, 1)
    first = (lane & (2 * half - 1)) < half
    partner = jnp.where(first, pltpu.roll(y, LANES - half, 1), pltpu.roll(y, half, 1))
    return y * cos + partner * sin


def _proj_kernel(x_ref, w_ref, bias_ref, cos64_ref, sin64_ref, cos32_ref, sin32_ref, *out_refs):
    xb = x_ref[...].astype(jnp.bfloat16)
    start = 0
    for (name, cols, epi, dtype, scale), o_ref in zip(PROJ_SEGS, out_refs):
        width = len(cols)
        y = _dot(xb, w_ref[:, start:start + width])
        start += width
        if epi == 'ones_hi':
            lane = lax.broadcasted_iota(jnp.int32, y.shape, 1)
            y = jnp.where((lane & HEAD_DIM) != 0, 1.0, y)
        elif epi == 'silu':
            y = y * jax.nn.sigmoid(y)
        elif epi == 'sigmoid':
            y = jax.nn.sigmoid(y)
        elif epi == 'logsig':
            y = y + bias_ref[...]
            y = jnp.minimum(y, 0.0) - jnp.log1p(jnp.exp(-jnp.abs(y)))
        elif epi in ('rope64', 'rope32'):
            half = DIFF_QK_DIM // 2 if epi == 'rope32' else HEAD_DIM // 2
            cos = cos32_ref[...] if epi == 'rope32' else cos64_ref[...]
            sin = sin32_ref[...] if epi == 'rope32' else sin64_ref[...]
            chunks = [_rope_chunk(y[:, c * LANES:(c + 1) * LANES], cos, sin, half)
                      for c in range(width // LANES)]
            y = chunks[0] if len(chunks) == 1 else jnp.concatenate(chunks, axis=1)
        if scale is not None:
            y = y * scale
        o_ref[...] = y.astype(dtype)


def _rope_tables(seq, group):
    half = group // 2
    inv = ROPE_THETA ** (-(jnp.arange(half, dtype=jnp.float32) * 2.0 / group))
    ang = jnp.arange(seq, dtype=jnp.float32)[:, None] * inv[None, :]
    cos, sin = jnp.cos(ang), jnp.sin(ang)
    reps = LANES // group
    cos_t = jnp.tile(jnp.concatenate([cos, cos], axis=1), (1, reps))
    sin_t = jnp.tile(jnp.concatenate([-sin, sin], axis=1), (1, reps))
    return cos_t, sin_t


def _project(x2d, w_perm, bias_row, tables, seq):
    m = x2d.shape[0]
    nt = seq // PROJ_TM
    tab_spec = pl.BlockSpec((PROJ_TM, LANES), lambda i: (i % nt, 0))
    out_shape = [jax.ShapeDtypeStruct((m, len(seg[1])), seg[3]) for seg in PROJ_SEGS]
    out_specs = [pl.BlockSpec((PROJ_TM, len(seg[1])), lambda i: (i, 0)) for seg in PROJ_SEGS]
    outs = pl.pallas_call(
        _proj_kernel,
        grid=(m // PROJ_TM,),
        in_specs=[pl.BlockSpec((PROJ_TM, D_MODEL), lambda i: (i, 0)),
                  pl.BlockSpec((D_MODEL, PROJ_N), lambda i: (0, 0)),
                  pl.BlockSpec((1, LANES), lambda i: (0, 0)),
                  tab_spec, tab_spec, tab_spec, tab_spec],
        out_specs=out_specs,
        out_shape=out_shape,
        compiler_params=_cparams(("parallel",)),
        name="in_proj",
    )(x2d, w_perm, bias_row, *tables)
    return {seg[0]: o for seg, o in zip(PROJ_SEGS, outs)}


def _fox_prep_kernel(lf_ref, q_ref, k_ref, qcat_ref, kcat_ref, carry_ref):
    @pl.when(pl.program_id(1) == 0)
    def _():
        carry_ref[...] = jnp.zeros_like(carry_ref)

    t = CUM_T
    row = lax.broadcasted_iota(jnp.int32, (t, t), 0)
    col = lax.broadcasted_iota(jnp.int32, (t, t), 1)
    tri = jnp.where(col <= row, 1.0, 0.0).astype(jnp.bfloat16)
    hi, mid, lo = _split3(lf_ref[...])
    c = _dot(tri, hi) + _dot(tri, mid) + _dot(tri, lo) + carry_ref[0:1, :]
    carry_ref[0:1, :] = c[t - 1:t, :]

    terms = _split3(c * LOG2E)
    src = lax.broadcasted_iota(jnp.int32, (LANES, FOX_HEADS * LANES), 0)
    dst = lax.broadcasted_iota(jnp.int32, (LANES, FOX_HEADS * LANES), 1)

    def spread(x, slot):
        place = jnp.where(dst == src * LANES + slot, 1.0, 0.0).astype(jnp.bfloat16)
        return _dot(x, place)

    n_terms = len(terms)
    qa = sum(spread(x, i) for i, x in enumerate(terms))
    ka = -sum(spread(x, n_terms + i) for i, x in enumerate(terms))
    lane = lax.broadcasted_iota(jnp.int32, qa.shape, 1) & (LANES - 1)
    qa = jnp.where((lane >= n_terms) & (lane < 2 * n_terms), 1.0, qa).astype(jnp.bfloat16)
    ka = jnp.where(lane < n_terms, 1.0, ka).astype(jnp.bfloat16)
    q = q_ref[...]
    k = k_ref[...]
    for h in range(FOX_HEADS):
        pair, hh = divmod(h, 2)
        q2 = q[:, pair * LANES:(pair + 1) * LANES]
        base = h * FOX_CAT
        qcat_ref[:, base:base + LANES] = jnp.where(_half_mask(q2.shape, hh), q2, jnp.zeros_like(q2))
        qcat_ref[:, base + LANES:base + FOX_CAT] = qa[:, h * LANES:(h + 1) * LANES]
        kcat_ref[:, base:base + LANES] = k[:, pair * LANES:(pair + 1) * LANES]
        kcat_ref[:, base + LANES:base + FOX_CAT] = ka[:, h * LANES:(h + 1) * LANES]


def _fox_prep(p, batch, seq):
    m = p['fox_q'].shape[0]
    nt = seq // CUM_T
    row = lambda b, j: (b * nt + j, 0)
    cat = jax.ShapeDtypeStruct((m, FOX_HEADS * FOX_CAT), jnp.bfloat16)
    return pl.pallas_call(
        _fox_prep_kernel,
        grid=(batch, nt),
        in_specs=[pl.BlockSpec((CUM_T, LANES), row), pl.BlockSpec((CUM_T, FOX_W), row),
                  pl.BlockSpec((CUM_T, FOX_W), row)],
        out_specs=[pl.BlockSpec((CUM_T, FOX_HEADS * FOX_CAT), row)] * 2,
        out_shape=[cat, cat],
        scratch_shapes=[pltpu.VMEM((8, LANES), jnp.float32)],
        compiler_params=_cparams(("arbitrary", "arbitrary")),
        name="fox_prep",
    )(p['fox_logf'], p['fox_q'], p['fox_k'])


def _compress_kernel(ones_half, ch_ref, pos_ref, w1_ref, w2_ref, o_ref):
    ch = ch_ref[...]
    a_in = (ch + pos_ref[0:1, :]).astype(jnp.bfloat16)
    b_in = (ch + pos_ref[1:2, :]).astype(jnp.bfloat16)
    n_chunks = ch.shape[0]
    row = lax.broadcasted_iota(jnp.int32, (n_chunks, LANES), 0)
    lane = lax.broadcasted_iota(jnp.int32, (n_chunks, LANES), 1)
    for g in range(NSA_KV_GROUPS):
        a = _dot(a_in, w1_ref[g, 0])
        b = _dot(b_in, w1_ref[g, 1])
        h = a + pltpu.roll(b, n_chunks - 1, 0)
        act = (h * jax.nn.sigmoid(h)).astype(jnp.bfloat16)
        out = jnp.where(row < n_chunks - 1, _dot(act, w2_ref[...]), 0.0)
        if ones_half:
            out = jnp.where(lane >= HEAD_DIM, 1.0, out)
        o_ref[g] = out.astype(o_ref.dtype)


def _compress(tok, pos_emb, w1, w2, batch, seq, ones_half):
    n_chunks = seq // CMP_STRIDE
    width = CMP_STRIDE * NSA_KV_W
    halves = w1.reshape(2, CMP_STRIDE, HEAD_DIM, CMP_HIDDEN)
    w1s = jnp.zeros((NSA_KV_GROUPS, 2, CMP_STRIDE, NSA_KV_GROUPS, HEAD_DIM, CMP_HIDDEN), w1.dtype)
    for g in range(NSA_KV_GROUPS):
        w1s = w1s.at[g, :, :, g].set(halves)
    w1s = w1s.reshape(NSA_KV_GROUPS, 2, width, CMP_HIDDEN).astype(jnp.bfloat16)
    pos = jnp.broadcast_to(pos_emb.reshape(2, CMP_STRIDE, 1, HEAD_DIM),
                           (2, CMP_STRIDE, NSA_KV_GROUPS, HEAD_DIM)).reshape(2, width)
    w2d = jnp.concatenate([w2, jnp.zeros_like(w2) if ones_half else w2], axis=1).astype(jnp.bfloat16)
    sq = None
    return pl.pallas_call(
        partial(_compress_kernel, ones_half),
        grid=(batch,),
        in_specs=[pl.BlockSpec((sq, n_chunks, width), lambda b: (b, 0, 0)),
                  pl.BlockSpec((2, width), lambda b: (0, 0)),
                  pl.BlockSpec((NSA_KV_GROUPS, 2, width, CMP_HIDDEN), lambda b: (0, 0, 0, 0)),
                  pl.BlockSpec((CMP_HIDDEN, LANES), lambda b: (0, 0))],
        out_specs=pl.BlockSpec((sq, NSA_KV_GROUPS, n_chunks, LANES), lambda b: (b, 0, 0, 0)),
        out_shape=jax.ShapeDtypeStruct((batch, NSA_KV_GROUPS, n_chunks, LANES), jnp.bfloat16),
        compiler_params=_cparams(("parallel",)),
        name="nsa_compress_v" if ones_half else "nsa_compress_k",
    )(tok.reshape(batch, n_chunks, width), pos, w1s, w2d)


def _half_mask(shape, hh):
    lane = lax.broadcasted_iota(jnp.int32, shape, 1)
    return (lane >= hh * HEAD_DIM) & (lane < (hh + 1) * HEAD_DIM)


def _cmp_topk_kernel(q_ref, kc_ref, vc_ref, ov_ref, cnt_ref, oc_ref, negsel_ref):
    tq = q_ref.shape[0]
    t = pl.program_id(2) * tq + lax.broadcasted_iota(jnp.int32, (tq, 1), 0)
    n = lax.broadcasted_iota(jnp.int32, (1, N_CMP_PAD), 1)
    mask_c = (CMP_STRIDE * n + (CMP_LEN - 1)) <= t
    any_visible = jnp.where(t >= CMP_LEN - 1, 1.0, 0.0)
    kc = kc_ref[...]
    vc = vc_ref[...]
    lane = lax.broadcasted_iota(jnp.int32, (tq, LANES), 1)
    low = lane < HEAD_DIM
    psum = jnp.zeros((tq, N_CMP_PAD), jnp.float32)
    for j in range(NSA_HPG // 2):
        q2 = q_ref[:, j * LANES:(j + 1) * LANES]
        o_pair = []
        for hh in range(2):
            qm = jnp.where(_half_mask(q2.shape, hh), q2, jnp.zeros_like(q2))
            s = jnp.where(mask_c, _dot_t(qm, kc), NEG_INF)
            e = jnp.exp2(s - jnp.max(s, axis=1, keepdims=True))
            acc = _dot(e.astype(jnp.bfloat16), vc)
            rowsum = jnp.where(low, pltpu.roll(acc, HEAD_DIM, 1), acc)
            inv = any_visible / rowsum
            psum = psum + e * jnp.tile(inv, (1, N_CMP_PAD // LANES))
            o_pair.append(acc * inv)
        oc_ref[:, j * LANES:(j + 1) * LANES] = jnp.where(low, o_pair[0], pltpu.roll(o_pair[1], HEAD_DIM, 1))
    p_hi = psum.astype(jnp.bfloat16)
    p_lo = (psum - p_hi.astype(jnp.float32)).astype(jnp.bfloat16)
    imp = _dot(p_hi, ov_ref[...]) + _dot(p_lo, ov_ref[...])
    cur = jnp.right_shift(t, SEL_SHIFT)
    valid = lane * SEL_BLOCK <= t
    forced = (lane == 0) | (lane == cur) | (lane == cur - 1)
    score = jnp.where(valid, jnp.where(forced, FORCE_SCORE, imp), NEG_INF)
    cnt_mat = cnt_ref[...]
    budget = jnp.full((tq, LANES), float(SEL_TOPK), jnp.float32)
    sel = jnp.zeros((tq, LANES), jnp.float32)
    for _ in range(SEL_TOPK):
        level = score == jnp.max(score, axis=1, keepdims=True)
        score = jnp.where(level, -jnp.inf, score)
        counts = _dot(jnp.where(level, 1.0, 0.0).astype(jnp.bfloat16), cnt_mat)
        rank = counts[:, LANES:]
        sel = jnp.where(level & (rank <= budget), 1.0, sel)
        budget = jnp.maximum(budget - counts[:, :LANES], 0.0)
    negsel_ref[...] = jnp.where(sel > 0.5, 0.0, NEG_INF).astype(negsel_ref.dtype)


def _cmp_topk(nsa_q, kc, vc, overlap, batch, seq):
    m = nsa_q.shape[0]
    nq = seq // CMP_TQ
    sq = None
    idx = np.arange(LANES)
    cnt_mat = jnp.asarray(np.concatenate([np.ones((LANES, LANES), np.float32),
                                          (idx[:, None] <= idx[None, :]).astype(np.float32)], axis=1),
                          jnp.bfloat16)
    return pl.pallas_call(
        _cmp_topk_kernel,
        grid=(batch, NSA_KV_GROUPS, nq),
        in_specs=[pl.BlockSpec((CMP_TQ, NSA_HPG * HEAD_DIM), lambda b, g, i: (b * nq + i, g)),
                  pl.BlockSpec((sq, sq, N_CMP_PAD, LANES), lambda b, g, i: (b, g, 0, 0)),
                  pl.BlockSpec((sq, sq, N_CMP_PAD, LANES), lambda b, g, i: (b, g, 0, 0)),
                  pl.BlockSpec((N_CMP_PAD, LANES), lambda b, g, i: (0, 0)),
                  pl.BlockSpec((LANES, 2 * LANES), lambda b, g, i: (0, 0))],
        out_specs=[pl.BlockSpec((CMP_TQ, NSA_HPG * HEAD_DIM), lambda b, g, i: (b * nq + i, g)),
                   pl.BlockSpec((sq, sq, CMP_TQ, LANES), lambda b, g, i: (b, g, i, 0))],
        out_shape=[jax.ShapeDtypeStruct((m, NSA_W), jnp.float32),
                   jax.ShapeDtypeStruct((batch, NSA_KV_GROUPS, seq, LANES), jnp.bfloat16)],
        compiler_params=_cparams(("parallel", "parallel", "parallel")),
        name="nsa_cmp_topk",
    )(nsa_q, kc, vc, overlap, cnt_mat)


def _online_step(s, v, m_ref, acc_ref, idx):
    m_old = m_ref[idx]
    m_new = jnp.maximum(m_old, jnp.max(s, axis=1, keepdims=True))
    alpha = jnp.exp2(m_old - m_new)
    p = jnp.exp2(s - jnp.tile(m_new, (1, s.shape[1] // LANES)))
    acc_ref[idx] = alpha * acc_ref[idx] + _dot(p.astype(jnp.bfloat16), v)
    m_ref[idx] = m_new


def _init_state(m_ref, acc_ref):
    m_ref[...] = jnp.full(m_ref.shape, NEG_INF, jnp.float32)
    acc_ref[...] = jnp.zeros(acc_ref.shape, jnp.float32)


def _normalized(acc):
    return acc / pltpu.roll(acc, HEAD_DIM, 1)


def _col_minus_row(rows, cols):
    return (lax.broadcasted_iota(jnp.int32, (rows, cols), 1)
            - lax.broadcasted_iota(jnp.int32, (rows, cols), 0))


def _causal_sweep(qi, tile):
    ratio = ATT_TQ // ATT_TK
    for d in range(ratio):
        tile(qi * ratio + d, -d * ATT_TK)

    def body(kt, carry):
        tile(kt, None)
        return carry

    lax.fori_loop(0, qi * ratio, body, 0)


def _pick_col(x, col):
    lane = lax.broadcasted_iota(jnp.int32, x.shape, 1)
    return jnp.sum(jnp.where(lane == col, x, 0.0), axis=1, keepdims=True)


def _nsa_attn_kernel(q_ref, ksel_ref, vsel_ref, kwin_ref, vwin_ref, negsel_ref, oc_ref, gate_ref, z_ref,
                     o_ref, qcat_ref, m_ref, acc_ref):
    tq, tk = ATT_TQ, ATT_TK
    g = pl.program_id(1)
    j = pl.program_id(2)
    qi = pl.program_id(3)
    q2 = q_ref[...]
    negsel = negsel_ref[...]
    for hh in range(2):
        qm = jnp.where(_half_mask(q2.shape, hh), q2, jnp.zeros_like(q2))
        qcat_ref[hh] = jnp.concatenate([qm, negsel], axis=1)
    _init_state(m_ref, acc_ref)
    cmr = _col_minus_row(tq, tk)
    blocks_per_tile = tk // SEL_BLOCK

    def sel_tile(kt, off):
        start = pl.multiple_of(kt * tk, tk)
        ks = ksel_ref[pl.ds(start, tk), :]
        vs = vsel_ref[pl.ds(start, tk), :]
        key_block = (jnp.right_shift(lax.broadcasted_iota(jnp.int32, (tk, LANES), 0), SEL_SHIFT)
                     + kt * blocks_per_tile)
        onehot = jnp.where(key_block == lax.broadcasted_iota(jnp.int32, (tk, LANES), 1),
                           1.0, 0.0).astype(jnp.bfloat16)
        kcat = jnp.concatenate([ks, onehot], axis=1)
        for hh in range(2):
            s = _dot_t(qcat_ref[hh], kcat)
            if off is not None:
                s = jnp.where(cmr <= off, s, NEG_INF)
            _online_step(s, vs, m_ref, acc_ref, hh)

    _causal_sweep(qi, sel_tile)
    o_sel = [_normalized(acc_ref[hh]) for hh in range(2)]

    _init_state(m_ref, acc_ref)
    cmr_w = _col_minus_row(tk, tk)
    for half in range(tq // tk):
        rows = slice(half * tk, (half + 1) * tk)
        wt = qi * (tq // tk) + half
        d0 = pl.multiple_of(wt * tk, tk)
        p0 = pl.multiple_of(jnp.maximum(wt - 1, 0) * tk, tk)
        prev_ok = cmr_w > jnp.where(wt > 0, 0, tk)
        for hh in range(2):
            qm = qcat_ref[hh, rows, :LANES]
            s = jnp.where(cmr_w <= 0, _dot_t(qm, kwin_ref[pl.ds(d0, tk), :]), NEG_INF)
            _online_step(s, vwin_ref[pl.ds(d0, tk), :], m_ref, acc_ref, (hh, rows))
            s = jnp.where(prev_ok, _dot_t(qm, kwin_ref[pl.ds(p0, tk), :]), NEG_INF)
            _online_step(s, vwin_ref[pl.ds(p0, tk), :], m_ref, acc_ref, (hh, rows))
    o_win = [_normalized(acc_ref[hh]) for hh in range(2)]

    gates = gate_ref[...]
    head0 = (g * NSA_HPG + j * 2) * 3
    comb = []
    for hh in range(2):
        base = head0 + hh * 3
        comb.append((_pick_col(gates, base), _pick_col(gates, base + 1) * o_sel[hh]
                     + _pick_col(gates, base + 2) * o_win[hh]))
    lane = lax.broadcasted_iota(jnp.int32, (tq, LANES), 1)
    low = lane < HEAD_DIM
    out = (jnp.where(low, comb[0][0], comb[1][0]) * oc_ref[...]
           + jnp.where(low, comb[0][1], pltpu.roll(comb[1][1], HEAD_DIM, 1)))
    o_ref[...] = (out * z_ref[...]).astype(o_ref.dtype)


def _nsa_attn(p, negsel, o_cmp, batch, seq):
    m = p['nsa_q'].shape[0]
    t = ATT_TQ
    nq = seq // t
    sq = None
    row_blk = lambda b, g, j, i: (b * nq + i, g * 2 + j)
    kv_spec = pl.BlockSpec((seq, LANES), lambda b, g, j, i: (b, g))
    return pl.pallas_call(
        _nsa_attn_kernel,
        grid=(batch, NSA_KV_GROUPS, NSA_HPG // 2, nq),
        in_specs=[pl.BlockSpec((t, LANES), row_blk),
                  kv_spec, kv_spec, kv_spec, kv_spec,
                  pl.BlockSpec((sq, sq, t, LANES), lambda b, g, j, i: (b, g, i, 0)),
                  pl.BlockSpec((t, LANES), row_blk),
                  pl.BlockSpec((t, LANES), lambda b, g, j, i: (b * nq + i, 0)),
                  pl.BlockSpec((t, LANES), row_blk)],
        out_specs=pl.BlockSpec((t, LANES), row_blk),
        out_shape=jax.ShapeDtypeStruct((m, NSA_W), jnp.bfloat16),
        scratch_shapes=[pltpu.VMEM((2, t, 2 * LANES), jnp.bfloat16),
                        pltpu.VMEM((2, t, LANES), jnp.float32),
                        pltpu.VMEM((2, t, LANES), jnp.float32)],
        compiler_params=_cparams(("parallel", "parallel", "parallel", "arbitrary")),
        name="nsa_sel_win_attn",
    )(p['nsa_q'], p['nsa_k_sel'], p['nsa_v_sel'], p['nsa_k_win'], p['nsa_v_win'], negsel, o_cmp,
      p['nsa_gate'], p['nsa_z'])


def _fox_kernel(q_ref, k_ref, v_ref, z_ref, o_ref, m_ref, acc_ref):
    tq, tk = ATT_TQ, ATT_TK
    qi = pl.program_id(2)
    _init_state(m_ref, acc_ref)
    cmr = _col_minus_row(tq, tk)

    def tile(kt, off):
        start = pl.multiple_of(kt * tk, tk)
        for hh in range(2):
            cat = slice(hh * FOX_CAT, (hh + 1) * FOX_CAT)
            s = _dot_t(q_ref[:, cat], k_ref[pl.ds(start, tk), cat])
            if off is not None:
                s = jnp.where(cmr <= off, s, NEG_INF)
            _online_step(s, v_ref[pl.ds(start, tk), hh * LANES:(hh + 1) * LANES], m_ref, acc_ref, hh)

    _causal_sweep(qi, tile)
    lane = lax.broadcasted_iota(jnp.int32, (tq, LANES), 1)
    out = jnp.where(lane < HEAD_DIM, _normalized(acc_ref[0]), pltpu.roll(_normalized(acc_ref[1]), HEAD_DIM, 1))
    o_ref[...] = (out * z_ref[...]).astype(o_ref.dtype)


def _fox_attn(p, qcat, kcat, batch, seq):
    m = qcat.shape[0]
    t = ATT_TQ
    nq = seq // t
    row_blk = lambda b, j, i: (b * nq + i, j)
    return pl.pallas_call(
        _fox_kernel,
        grid=(batch, FOX_HEADS // 2, nq),
        in_specs=[pl.BlockSpec((t, 2 * FOX_CAT), row_blk),
                  pl.BlockSpec((seq, 2 * FOX_CAT), lambda b, j, i: (b, j)),
                  pl.BlockSpec((seq, 2 * LANES), lambda b, j, i: (b, j)),
                  pl.BlockSpec((t, LANES), row_blk)],
        out_specs=pl.BlockSpec((t, LANES), row_blk),
        out_shape=jax.ShapeDtypeStruct((m, FOX_W), jnp.bfloat16),
        scratch_shapes=[pltpu.VMEM((2, t, LANES), jnp.float32),
                        pltpu.VMEM((2, t, LANES), jnp.float32)],
        compiler_params=_cparams(("parallel", "parallel", "arbitrary")),
        name="fox_attn",
    )(qcat, kcat, p['fox_v'], p['fox_z'])


def _diff_kernel(lam_init, q_ref, k_ref, v_ref, z_ref, lam_ref, g_ref, o_ref, qm_ref, m_ref, acc_ref):
    tq, tk = ATT_TQ, ATT_TK
    qi = pl.program_id(2)
    q2 = q_ref[...]
    lane = lax.broadcasted_iota(jnp.int32, (tq, LANES), 1)
    for idx in range(4):
        lo = idx * DIFF_QK_DIM
        qm_ref[idx] = jnp.where((lane >= lo) & (lane < lo + DIFF_QK_DIM), q2, jnp.zeros_like(q2))
    _init_state(m_ref, acc_ref)
    cmr = _col_minus_row(tq, tk)

    def tile(kt, off):
        start = pl.multiple_of(kt * tk, tk)
        k = k_ref[pl.ds(start, tk), :]
        for idx in range(4):
            hh = idx // 2
            s = _dot_t(qm_ref[idx], k)
            if off is not None:
                s = jnp.where(cmr <= off, s, NEG_INF)
            _online_step(s, v_ref[pl.ds(start, tk), hh * LANES:(hh + 1) * LANES], m_ref, acc_ref, idx)

    _causal_sweep(qi, tile)

    lam_rows = lam_ref[...]
    lam = (jnp.exp(jnp.sum(lam_rows[0:1, :] * lam_rows[1:2, :], axis=1, keepdims=True))
           - jnp.exp(jnp.sum(lam_rows[2:3, :] * lam_rows[3:4, :], axis=1, keepdims=True)) + lam_init)
    o = [_normalized(acc_ref[idx]) for idx in range(4)]
    low = lane < HEAD_DIM
    normed = []
    for hh in range(2):
        d = jnp.where(low, o[2 * hh] - lam * o[2 * hh + 1], 0.0)
        ms = jnp.sum(d * d, axis=1, keepdims=True) / HEAD_DIM
        normed.append(d * lax.rsqrt(ms + RMS_EPS))
    out = jnp.where(low, normed[0], pltpu.roll(normed[1], HEAD_DIM, 1)) * g_ref[...] * (1.0 - lam_init)
    o_ref[...] = (out * z_ref[...]).astype(o_ref.dtype)


def _diff_attn(p, lam_rows, g_row, lam_init, batch, seq):
    m = p['diff_q'].shape[0]
    t = ATT_TQ
    nq = seq // t
    row_blk = lambda b, j, i: (b * nq + i, j)
    return pl.pallas_call(
        partial(_diff_kernel, lam_init),
        grid=(batch, DIFF_HEADS // 2, nq),
        in_specs=[pl.BlockSpec((t, LANES), row_blk),
                  pl.BlockSpec((seq, LANES), lambda b, j, i: (b, j)),
                  pl.BlockSpec((seq, 2 * LANES), lambda b, j, i: (b, j)),
                  pl.BlockSpec((t, LANES), row_blk),
                  pl.BlockSpec((8, LANES), lambda b, j, i: (0, 0)),
                  pl.BlockSpec((1, LANES), lambda b, j, i: (0, 0))],
        out_specs=pl.BlockSpec((t, LANES), row_blk),
        out_shape=jax.ShapeDtypeStruct((m, DIFF_W), jnp.bfloat16),
        scratch_shapes=[pltpu.VMEM((4, t, LANES), jnp.bfloat16),
                        pltpu.VMEM((4, t, LANES), jnp.float32),
                        pltpu.VMEM((4, t, LANES), jnp.float32)],
        compiler_params=_cparams(("parallel", "parallel", "arbitrary")),
        name="diff_attn",
    )(p['diff_q'], p['diff_k'], p['diff_v'], p['diff_z'], lam_rows, g_row)


def _out_kernel(of_ref, on_ref, od_ref, w_ref, x_ref, g_ref, b_ref, o_ref):
    y = (DEEPNORM_ALPHA * x_ref[...]
         + _dot(of_ref[...], w_ref[0:FOX_W, :])
         + _dot(on_ref[...], w_ref[FOX_W:FOX_W + NSA_W, :])
         + _dot(od_ref[...], w_ref[FOX_W + NSA_W:, :]))
    mu = jnp.mean(y, axis=1, keepdims=True)
    yc = y - mu
    var = jnp.mean(yc * yc, axis=1, keepdims=True)
    o_ref[...] = yc * lax.rsqrt(var + LN_EPS) * g_ref[...] + b_ref[...]


def _out_proj(o_fox, o_nsa, o_diff, w_out, x2d, g_row, b_row):
    m = x2d.shape[0]
    tm = PROJ_TM
    row = lambda i: (i, 0)
    const = lambda i: (0, 0)
    return pl.pallas_call(
        _out_kernel,
        grid=(m // tm,),
        in_specs=[pl.BlockSpec((tm, FOX_W), row), pl.BlockSpec((tm, NSA_W), row),
                  pl.BlockSpec((tm, DIFF_W), row),
                  pl.BlockSpec((D_MODEL, D_MODEL), const),
                  pl.BlockSpec((tm, D_MODEL), row),
                  pl.BlockSpec((1, D_MODEL), const), pl.BlockSpec((1, D_MODEL), const)],
        out_specs=pl.BlockSpec((tm, D_MODEL), row),
        out_shape=jax.ShapeDtypeStruct((m, D_MODEL), jnp.float32),
        compiler_params=_cparams(("parallel",)),
        name="out_proj_ln",
    )(o_fox, o_nsa, o_diff, w_out, x2d, g_row, b_row)


def _overlap_matrix(seq):
    n_slc = seq // SEL_BLOCK
    cmp_start = CMP_STRIDE * np.arange(N_CMP_PAD)
    sel_start = SEL_BLOCK * np.arange(n_slc)
    ov = np.clip(np.minimum(cmp_start[:, None] + CMP_LEN, sel_start[None, :] + SEL_BLOCK)
                 - np.maximum(cmp_start[:, None], sel_start[None, :]), 0, None).astype(np.float32) / CMP_LEN
    return ov


def kernel(x, w_in, b_fox_f, cmp_pos_k, cmp_pos_v, cmp_w1_k, cmp_w2_k, cmp_w1_v, cmp_w2_v,
           lam_q1, lam_k1, lam_q2, lam_k2, diff_subln_g, w_out, ln_g, ln_b):
    batch, seq, _ = x.shape
    assert seq % ATT_TQ == 0 and seq // SEL_BLOCK == LANES and (seq - CMP_LEN) // CMP_STRIDE + 1 < N_CMP_PAD
    assert WINDOW == ATT_TK and ATT_TQ % ATT_TK == 0
    m = batch * seq
    tables = _rope_tables(seq, HEAD_DIM) + _rope_tables(seq, DIFF_QK_DIM)
    overlap = jnp.asarray(_overlap_matrix(seq), jnp.bfloat16)

    x2d = x.reshape(m, D_MODEL)
    for l in range(DEPTH):
        w_perm = _permute_columns(w_in[l]).astype(jnp.bfloat16)
        bias_row = jnp.zeros((1, LANES), jnp.float32).at[0, :FOX_HEADS].set(b_fox_f[l])
        p = _project(x2d, w_perm, bias_row, tables, seq)

        fox_qcat, fox_kcat = _fox_prep(p, batch, seq)
        o_fox = _fox_attn(p, fox_qcat, fox_kcat, batch, seq)

        kc = _compress(p['nsa_k_cmp'], cmp_pos_k[l], cmp_w1_k[l], cmp_w2_k[l], batch, seq, False)
        vc = _compress(p['nsa_v_cmp'], cmp_pos_v[l], cmp_w1_v[l], cmp_w2_v[l], batch, seq, True)
        o_cmp, negsel = _cmp_topk(p['nsa_q'], kc, vc, overlap, batch, seq)
        o_nsa = _nsa_attn(p, negsel, o_cmp, batch, seq)

        lam_init = 0.8 - 0.6 * math.exp(-0.3 * l)
        lam_rows = jnp.zeros((8, LANES), jnp.float32)
        for r, v in enumerate((lam_q1[l], lam_k1[l], lam_q2[l], lam_k2[l])):
            lam_rows = lam_rows.at[r, :DIFF_QK_DIM].set(v.astype(jnp.float32))
        g_row = jnp.tile(diff_subln_g[l].reshape(1, HEAD_DIM), (1, 2))
        o_diff = _diff_attn(p, lam_rows, g_row, lam_init, batch, seq)

        x2d = _out_proj(o_fox, o_nsa, o_diff, w_out[l].astype(jnp.bfloat16), x2d,
                        ln_g[l].reshape(1, D_MODEL), ln_b[l].reshape(1, D_MODEL))
    return x2d.reshape(batch, seq, D_MODEL)
```

```python
import math
from functools import partial

import numpy as np
import jax
import jax.numpy as jnp
from jax import lax
from jax.experimental import pallas as pl
from jax.experimental.pallas import tpu as pltpu

D_MODEL = 1024
DEPTH = 2
HEAD_DIM = 64
FOX_HEADS = 4
NSA_HEADS = 8
DIFF_HEADS = 4
FOX_W = FOX_HEADS * HEAD_DIM
NSA_W = NSA_HEADS * HEAD_DIM
DIFF_QK_DIM = HEAD_DIM // 2
DIFF_W = DIFF_HEADS * HEAD_DIM
NSA_KV_GROUPS = 2
NSA_HPG = NSA_HEADS // NSA_KV_GROUPS
NSA_KV_W = NSA_KV_GROUPS * HEAD_DIM
CMP_LEN = 32
CMP_STRIDE = 16
CMP_HIDDEN = 2 * HEAD_DIM
SEL_BLOCK = 64
SEL_SHIFT = SEL_BLOCK.bit_length() - 1
SEL_TOPK = 16
WINDOW = 512
ROPE_THETA = 10000.0
LN_EPS = 1e-5
RMS_EPS = 1e-5
NEG_INF = -1e30
FORCE_SCORE = 1e30
DEEPNORM_ALPHA = (2 * DEPTH) ** 0.25
LOG2E = math.log2(math.e)

SPLITS = (
    ('fox_q', FOX_W), ('fox_k', FOX_W), ('fox_v', FOX_W), ('fox_f', FOX_HEADS), ('fox_z', FOX_W),
    ('nsa_q', NSA_W),
    ('nsa_k_cmp', NSA_KV_W), ('nsa_v_cmp', NSA_KV_W),
    ('nsa_k_sel', NSA_KV_W), ('nsa_v_sel', NSA_KV_W),
    ('nsa_k_win', NSA_KV_W), ('nsa_v_win', NSA_KV_W),
    ('nsa_gate', 3 * NSA_HEADS), ('nsa_z', NSA_W),
    ('diff_q', DIFF_W), ('diff_k', DIFF_W), ('diff_v', DIFF_W), ('diff_z', DIFF_W),
)
_OFF = {}
_acc = 0
for _n, _w in SPLITS:
    _OFF[_n] = _acc
    _acc += _w
IN_WIDTH = _acc

LANES = 128
VMEM_LIMIT = 56 * 1024 * 1024

PROJ_TM = 256
CUM_T = 512
CMP_TQ = 512
ATT_TQ = 1024
ATT_TK = 512
N_CMP_PAD = 512
FOX_CAT = 2 * LANES


def _cparams(sem):
    return pltpu.CompilerParams(dimension_semantics=sem, vmem_limit_bytes=VMEM_LIMIT)


def _dot(a, b):
    return jnp.dot(a, b, preferred_element_type=jnp.float32)


def _dot_t(a, b):
    return lax.dot_general(a, b, (((1,), (1,)), ((), ())), preferred_element_type=jnp.float32)


def _split3(x):
    hi = x.astype(jnp.bfloat16)
    r1 = x - hi.astype(jnp.float32)
    mid = r1.astype(jnp.bfloat16)
    lo = (r1 - mid.astype(jnp.float32)).astype(jnp.bfloat16)
    return hi, mid, lo


def _dup(name):
    base = _OFF[name]
    cols = []
    for g in range(NSA_KV_GROUPS):
        one = list(range(base + g * HEAD_DIM, base + (g + 1) * HEAD_DIM))
        cols += one + one
    return cols


def _v_ones(name, heads):
    base = _OFF[name]
    cols = []
    for h in range(heads):
        cols += list(range(base + h * HEAD_DIM, base + (h + 1) * HEAD_DIM)) + [-1] * HEAD_DIM
    return cols


def _pad(cols, width):
    return list(cols) + [-1] * (width - len(cols))


def _rng(name, width):
    return list(range(_OFF[name], _OFF[name] + width))


Q_SCALE = HEAD_DIM ** -0.5 * LOG2E
DIFF_Q_SCALE = DIFF_QK_DIM ** -0.5 * LOG2E

PROJ_SEGS = (
    ('fox_q', _rng('fox_q', FOX_W), 'none', jnp.bfloat16, Q_SCALE),
    ('fox_k', _rng('fox_k', FOX_W), 'none', jnp.bfloat16, None),
    ('fox_z', _rng('fox_z', FOX_W), 'silu', jnp.float32, None),
    ('fox_logf', _pad(_rng('fox_f', FOX_HEADS), LANES), 'logsig', jnp.float32, None),
    ('nsa_q', _rng('nsa_q', NSA_W), 'rope64', jnp.bfloat16, Q_SCALE),
    ('nsa_k_cmp', _rng('nsa_k_cmp', NSA_KV_W), 'rope64', jnp.float32, None),
    ('nsa_v_cmp', _rng('nsa_v_cmp', NSA_KV_W), 'none', jnp.float32, None),
    ('nsa_k_sel', _dup('nsa_k_sel'), 'rope64', jnp.bfloat16, None),
    ('nsa_k_win', _dup('nsa_k_win'), 'rope64', jnp.bfloat16, None),
    ('nsa_gate', _pad(_rng('nsa_gate', 3 * NSA_HEADS), LANES), 'sigmoid', jnp.float32, None),
    ('nsa_z', _rng('nsa_z', NSA_W), 'silu', jnp.float32, None),
    ('diff_q', _rng('diff_q', DIFF_W), 'rope32', jnp.bfloat16, DIFF_Q_SCALE),
    ('diff_k', _rng('diff_k', DIFF_W), 'rope32', jnp.bfloat16, None),
    ('diff_z', _rng('diff_z', DIFF_W), 'silu', jnp.float32, None),
)
PROJ_COLS = np.concatenate([np.asarray(seg[1], np.int32) for seg in PROJ_SEGS])
PROJ_N = int(PROJ_COLS.shape[0])

VT_SEGS = (
    ('fox_vt', _v_ones('fox_v', FOX_HEADS)),
    ('nsa_v_sel_t', _v_ones('nsa_v_sel', NSA_KV_GROUPS)),
    ('nsa_v_win_t', _v_ones('nsa_v_win', NSA_KV_GROUPS)),
    ('diff_vt', _v_ones('diff_v', DIFF_HEADS)),
)
VT_COLS = np.concatenate([np.asarray(seg[1], np.int32) for seg in VT_SEGS])
VT_N = int(VT_COLS.shape[0])


def _permute_columns(w, col_ids):
    pieces, start = [], 0
    cols = col_ids.tolist()
    while start < len(cols):
        end = start + 1
        if cols[start] < 0:
            while end < len(cols) and cols[end] < 0:
                end += 1
            pieces.append(jnp.zeros((w.shape[0], end - start), w.dtype))
        else:
            while end < len(cols) and cols[end] == cols[end - 1] + 1:
                end += 1
            pieces.append(w[:, cols[start]:cols[end - 1] + 1])
        start = end
    return jnp.concatenate(pieces, axis=1)


def _rope_chunk(y, cos, sin, half):
    lane = lax.broadcasted_iota(jnp.int32, y.shape, 1)
    first = (lane & (2 * half - 1)) < half
    partner = jnp.where(first, pltpu.roll(y, LANES - half, 1), pltpu.roll(y, half, 1))
    return y * cos + partner * sin


def _proj_kernel(x_ref, w_ref, wt_ref, bias_ref, cos64_ref, sin64_ref, cos32_ref, sin32_ref, *out_refs):
    xb = x_ref[...].astype(jnp.bfloat16)
    start = 0
    for (name, cols), o_ref in zip(VT_SEGS, out_refs[len(PROJ_SEGS):]):
        rows = len(cols)
        yt = _dot_t(wt_ref[start:start + rows, :], xb)
        start += rows
        row = lax.broadcasted_iota(jnp.int32, yt.shape, 0)
        o_ref[...] = jnp.where((row & HEAD_DIM) != 0, 1.0, yt).astype(o_ref.dtype)
    start = 0
    for (name, cols, epi, dtype, scale), o_ref in zip(PROJ_SEGS, out_refs):
        width = len(cols)
        y = _dot(xb, w_ref[:, start:start + width])
        start += width
        if epi == 'ones_hi':
            lane = lax.broadcasted_iota(jnp.int32, y.shape, 1)
            y = jnp.where((lane & HEAD_DIM) != 0, 1.0, y)
        elif epi == 'silu':
            y = y * jax.nn.sigmoid(y)
        elif epi == 'sigmoid':
            y = jax.nn.sigmoid(y)
        elif epi == 'logsig':
            y = y + bias_ref[...]
            y = jnp.minimum(y, 0.0) - jnp.log1p(jnp.exp(-jnp.abs(y)))
        elif epi in ('rope64', 'rope32'):
            half = DIFF_QK_DIM // 2 if epi == 'rope32' else HEAD_DIM // 2
            cos = cos32_ref[...] if epi == 'rope32' else cos64_ref[...]
            sin = sin32_ref[...] if epi == 'rope32' else sin64_ref[...]
            chunks = [_rope_chunk(y[:, c * LANES:(c + 1) * LANES], cos, sin, half)
                      for c in range(width // LANES)]
            y = chunks[0] if len(chunks) == 1 else jnp.concatenate(chunks, axis=1)
        if scale is not None:
            y = y * scale
        o_ref[...] = y.astype(dtype)


def _rope_tables(seq, group):
    half = group // 2
    inv = ROPE_THETA ** (-(jnp.arange(half, dtype=jnp.float32) * 2.0 / group))
    ang = jnp.arange(seq, dtype=jnp.float32)[:, None] * inv[None, :]
    cos, sin = jnp.cos(ang), jnp.sin(ang)
    reps = LANES // group
    cos_t = jnp.tile(jnp.concatenate([cos, cos], axis=1), (1, reps))
    sin_t = jnp.tile(jnp.concatenate([-sin, sin], axis=1), (1, reps))
    return cos_t, sin_t


def _project(x2d, w_perm, wt_perm, bias_row, tables, seq):
    m = x2d.shape[0]
    nt = seq // PROJ_TM
    tab_spec = pl.BlockSpec((PROJ_TM, LANES), lambda i: (i % nt, 0))
    out_shape = [jax.ShapeDtypeStruct((m, len(seg[1])), seg[3]) for seg in PROJ_SEGS]
    out_specs = [pl.BlockSpec((PROJ_TM, len(seg[1])), lambda i: (i, 0)) for seg in PROJ_SEGS]
    out_shape += [jax.ShapeDtypeStruct((m // seq, len(seg[1]), seq), jnp.bfloat16) for seg in VT_SEGS]
    out_specs += [pl.BlockSpec((None, len(seg[1]), PROJ_TM), lambda i: (i // nt, 0, i % nt)) for seg in VT_SEGS]
    outs = pl.pallas_call(
        _proj_kernel,
        grid=(m // PROJ_TM,),
        in_specs=[pl.BlockSpec((PROJ_TM, D_MODEL), lambda i: (i, 0)),
                  pl.BlockSpec((D_MODEL, PROJ_N), lambda i: (0, 0)),
                  pl.BlockSpec((VT_N, D_MODEL), lambda i: (0, 0)),
                  pl.BlockSpec((1, LANES), lambda i: (0, 0)),
                  tab_spec, tab_spec, tab_spec, tab_spec],
        out_specs=out_specs,
        out_shape=out_shape,
        compiler_params=_cparams(("parallel",)),
        name="in_proj",
    )(x2d, w_perm, wt_perm, bias_row, *tables)
    return {seg[0]: o for seg, o in zip(PROJ_SEGS + VT_SEGS, outs)}


def _fox_prep_kernel(lf_ref, q_ref, k_ref, qcat_ref, kcat_ref, carry_ref):
    @pl.when(pl.program_id(1) == 0)
    def _():
        carry_ref[...] = jnp.zeros_like(carry_ref)

    t = CUM_T
    row = lax.broadcasted_iota(jnp.int32, (t, t), 0)
    col = lax.broadcasted_iota(jnp.int32, (t, t), 1)
    tri = jnp.where(col <= row, 1.0, 0.0).astype(jnp.bfloat16)
    hi, mid, lo = _split3(lf_ref[...])
    c = _dot(tri, hi) + _dot(tri, mid) + _dot(tri, lo) + carry_ref[0:1, :]
    carry_ref[0:1, :] = c[t - 1:t, :]

    terms = _split3(c * LOG2E)
    src = lax.broadcasted_iota(jnp.int32, (LANES, FOX_HEADS * LANES), 0)
    dst = lax.broadcasted_iota(jnp.int32, (LANES, FOX_HEADS * LANES), 1)

    def spread(x, slot):
        place = jnp.where(dst == src * LANES + slot, 1.0, 0.0).astype(jnp.bfloat16)
        return _dot(x, place)

    n_terms = len(terms)
    qa = sum(spread(x, i) for i, x in enumerate(terms))
    ka = -sum(spread(x, n_terms + i) for i, x in enumerate(terms))
    lane = lax.broadcasted_iota(jnp.int32, qa.shape, 1) & (LANES - 1)
    qa = jnp.where((lane >= n_terms) & (lane < 2 * n_terms), 1.0, qa).astype(jnp.bfloat16)
    ka = jnp.where(lane < n_terms, 1.0, ka).astype(jnp.bfloat16)
    q = q_ref[...]
    k = k_ref[...]
    for h in range(FOX_HEADS):
        pair, hh = divmod(h, 2)
        q2 = q[:, pair * LANES:(pair + 1) * LANES]
        base = h * FOX_CAT
        qcat_ref[:, base:base + LANES] = jnp.where(_half_mask(q2.shape, hh), q2, jnp.zeros_like(q2))
        qcat_ref[:, base + LANES:base + FOX_CAT] = qa[:, h * LANES:(h + 1) * LANES]
        kcat_ref[:, base:base + LANES] = k[:, pair * LANES:(pair + 1) * LANES]
        kcat_ref[:, base + LANES:base + FOX_CAT] = ka[:, h * LANES:(h + 1) * LANES]


def _fox_prep(p, batch, seq):
    m = p['fox_q'].shape[0]
    nt = seq // CUM_T
    row = lambda b, j: (b * nt + j, 0)
    cat = jax.ShapeDtypeStruct((m, FOX_HEADS * FOX_CAT), jnp.bfloat16)
    return pl.pallas_call(
        _fox_prep_kernel,
        grid=(batch, nt),
        in_specs=[pl.BlockSpec((CUM_T, LANES), row), pl.BlockSpec((CUM_T, FOX_W), row),
                  pl.BlockSpec((CUM_T, FOX_W), row)],
        out_specs=[pl.BlockSpec((CUM_T, FOX_HEADS * FOX_CAT), row)] * 2,
        out_shape=[cat, cat],
        scratch_shapes=[pltpu.VMEM((8, LANES), jnp.float32)],
        compiler_params=_cparams(("arbitrary", "arbitrary")),
        name="fox_prep",
    )(p['fox_logf'], p['fox_q'], p['fox_k'])


def _compress_kernel(ones_half, ch_ref, pos_ref, w1_ref, w2_ref, o_ref):
    n_chunks = ch_ref.shape[0] // CMP_STRIDE
    ch = jnp.concatenate([ch_ref[pl.ds(i, n_chunks, stride=CMP_STRIDE), :] for i in range(CMP_STRIDE)], axis=1)
    a_in = (ch + pos_ref[0:1, :]).astype(jnp.bfloat16)
    b_in = (ch + pos_ref[1:2, :]).astype(jnp.bfloat16)
    row = lax.broadcasted_iota(jnp.int32, (n_chunks, LANES), 0)
    lane = lax.broadcasted_iota(jnp.int32, (n_chunks, LANES), 1)
    for g in range(NSA_KV_GROUPS):
        a = _dot(a_in, w1_ref[g, 0])
        b = _dot(b_in, w1_ref[g, 1])
        h = a + pltpu.roll(b, n_chunks - 1, 0)
        act = (h * jax.nn.sigmoid(h)).astype(jnp.bfloat16)
        out = jnp.where(row < n_chunks - 1, _dot(act, w2_ref[...]), 0.0)
        if ones_half:
            out = jnp.where(lane >= HEAD_DIM, 1.0, out)
        o_ref[g] = out.astype(o_ref.dtype)


def _compress(tok, pos_emb, w1, w2, batch, seq, ones_half):
    n_chunks = seq // CMP_STRIDE
    width = CMP_STRIDE * NSA_KV_W
    halves = w1.reshape(2, CMP_STRIDE, HEAD_DIM, CMP_HIDDEN)
    w1s = jnp.zeros((NSA_KV_GROUPS, 2, CMP_STRIDE, NSA_KV_GROUPS, HEAD_DIM, CMP_HIDDEN), w1.dtype)
    for g in range(NSA_KV_GROUPS):
        w1s = w1s.at[g, :, :, g].set(halves)
    w1s = w1s.reshape(NSA_KV_GROUPS, 2, width, CMP_HIDDEN).astype(jnp.bfloat16)
    pos = jnp.broadcast_to(pos_emb.reshape(2, CMP_STRIDE, 1, HEAD_DIM),
                           (2, CMP_STRIDE, NSA_KV_GROUPS, HEAD_DIM)).reshape(2, width)
    w2d = jnp.concatenate([w2, jnp.zeros_like(w2) if ones_half else w2], axis=1).astype(jnp.bfloat16)
    sq = None
    return pl.pallas_call(
        partial(_compress_kernel, ones_half),
        grid=(batch,),
        in_specs=[pl.BlockSpec((seq, NSA_KV_W), lambda b: (b, 0)),
                  pl.BlockSpec((2, width), lambda b: (0, 0)),
                  pl.BlockSpec((NSA_KV_GROUPS, 2, width, CMP_HIDDEN), lambda b: (0, 0, 0, 0)),
                  pl.BlockSpec((CMP_HIDDEN, LANES), lambda b: (0, 0))],
        out_specs=pl.BlockSpec((sq, NSA_KV_GROUPS, n_chunks, LANES), lambda b: (b, 0, 0, 0)),
        out_shape=jax.ShapeDtypeStruct((batch, NSA_KV_GROUPS, n_chunks, LANES), jnp.bfloat16),
        compiler_params=_cparams(("parallel",)),
        name="nsa_compress_v" if ones_half else "nsa_compress_k",
    )(tok, pos, w1s, w2d)


def _half_mask(shape, hh):
    lane = lax.broadcasted_iota(jnp.int32, shape, 1)
    return (lane >= hh * HEAD_DIM) & (lane < (hh + 1) * HEAD_DIM)


def _cmp_topk_kernel(q_ref, kc_ref, vc_ref, ov_ref, cnt_ref, oc_ref, negsel_ref):
    tq = q_ref.shape[0]
    t = pl.program_id(2) * tq + lax.broadcasted_iota(jnp.int32, (tq, 1), 0)
    n = lax.broadcasted_iota(jnp.int32, (1, N_CMP_PAD), 1)
    mask_c = (CMP_STRIDE * n + (CMP_LEN - 1)) <= t
    any_visible = jnp.where(t >= CMP_LEN - 1, 1.0, 0.0)
    kc = kc_ref[...]
    vc = vc_ref[...]
    lane = lax.broadcasted_iota(jnp.int32, (tq, LANES), 1)
    low = lane < HEAD_DIM
    psum = jnp.zeros((tq, N_CMP_PAD), jnp.float32)
    for j in range(NSA_HPG // 2):
        q2 = q_ref[:, j * LANES:(j + 1) * LANES]
        o_pair = []
        for hh in range(2):
            qm = jnp.where(_half_mask(q2.shape, hh), q2, jnp.zeros_like(q2))
            s = jnp.where(mask_c, _dot_t(qm, kc), NEG_INF)
            e = jnp.exp2(s - jnp.max(s, axis=1, keepdims=True))
            acc = _dot(e.astype(jnp.bfloat16), vc)
            rowsum = jnp.where(low, pltpu.roll(acc, HEAD_DIM, 1), acc)
            inv = any_visible / rowsum
            psum = psum + e * jnp.tile(inv, (1, N_CMP_PAD // LANES))
            o_pair.append(acc * inv)
        oc_ref[:, j * LANES:(j + 1) * LANES] = jnp.where(low, o_pair[0], pltpu.roll(o_pair[1], HEAD_DIM, 1))
    p_hi = psum.astype(jnp.bfloat16)
    p_lo = (psum - p_hi.astype(jnp.float32)).astype(jnp.bfloat16)
    imp = _dot(p_hi, ov_ref[...]) + _dot(p_lo, ov_ref[...])
    cur = jnp.right_shift(t, SEL_SHIFT)
    valid = lane * SEL_BLOCK <= t
    forced = (lane == 0) | (lane == cur) | (lane == cur - 1)
    score = jnp.where(valid, jnp.where(forced, FORCE_SCORE, imp), NEG_INF)
    cnt_mat = cnt_ref[...]
    budget = jnp.full((tq, LANES), float(SEL_TOPK), jnp.float32)
    sel = jnp.zeros((tq, LANES), jnp.float32)
    for _ in range(SEL_TOPK):
        level = score == jnp.max(score, axis=1, keepdims=True)
        score = jnp.where(level, -jnp.inf, score)
        counts = _dot(jnp.where(level, 1.0, 0.0).astype(jnp.bfloat16), cnt_mat)
        rank = counts[:, LANES:]
        sel = jnp.where(level & (rank <= budget), 1.0, sel)
        budget = jnp.maximum(budget - counts[:, :LANES], 0.0)
    negsel_ref[...] = jnp.where(sel > 0.5, 0.0, NEG_INF).astype(negsel_ref.dtype)


def _cmp_topk(nsa_q, kc, vc, overlap, batch, seq):
    m = nsa_q.shape[0]
    nq = seq // CMP_TQ
    sq = None
    idx = np.arange(LANES)
    cnt_mat = jnp.asarray(np.concatenate([np.ones((LANES, LANES), np.float32),
                                          (idx[:, None] <= idx[None, :]).astype(np.float32)], axis=1),
                          jnp.bfloat16)
    return pl.pallas_call(
        _cmp_topk_kernel,
        grid=(batch, NSA_KV_GROUPS, nq),
        in_specs=[pl.BlockSpec((CMP_TQ, NSA_HPG * HEAD_DIM), lambda b, g, i: (b * nq + i, g)),
                  pl.BlockSpec((sq, sq, N_CMP_PAD, LANES), lambda b, g, i: (b, g, 0, 0)),
                  pl.BlockSpec((sq, sq, N_CMP_PAD, LANES), lambda b, g, i: (b, g, 0, 0)),
                  pl.BlockSpec((N_CMP_PAD, LANES), lambda b, g, i: (0, 0)),
                  pl.BlockSpec((LANES, 2 * LANES), lambda b, g, i: (0, 0))],
        out_specs=[pl.BlockSpec((CMP_TQ, NSA_HPG * HEAD_DIM), lambda b, g, i: (b * nq + i, g)),
                   pl.BlockSpec((sq, sq, CMP_TQ, LANES), lambda b, g, i: (b, g, i, 0))],
        out_shape=[jax.ShapeDtypeStruct((m, NSA_W), jnp.float32),
                   jax.ShapeDtypeStruct((batch, NSA_KV_GROUPS, seq, LANES), jnp.bfloat16)],
        compiler_params=_cparams(("parallel", "parallel", "parallel")),
        name="nsa_cmp_topk",
    )(nsa_q, kc, vc, overlap, cnt_mat)


def _init_state(m_ref, acc_ref):
    m_ref[...] = jnp.full(m_ref.shape, NEG_INF, jnp.float32)
    acc_ref[...] = jnp.zeros(acc_ref.shape, jnp.float32)


def _key_max(st):
    rows = st.shape[0]
    while rows > 8:
        rows //= 2
        st = jnp.maximum(st[:rows], st[rows:])
    return jnp.max(st, axis=0, keepdims=True)


def _online_step_t(st, vt, m_ref, acc_ref, idx, off=None, lo=None):
    if off is not None:
        st = jnp.where(_key_minus_query(*st.shape) <= off, st, NEG_INF)
    if lo is not None:
        st = jnp.where(_key_minus_query(*st.shape) >= lo, st, NEG_INF)
    m_old = m_ref[idx]
    m_new = jnp.maximum(m_old, _key_max(st))
    alpha = jnp.exp2(m_old - m_new)
    pt = jnp.exp2(st - m_new).astype(jnp.bfloat16)
    acc_ref[idx] = alpha * acc_ref[idx] + _dot(vt, pt)
    m_ref[idx] = m_new


def _normalized_t(acc):
    return acc[:HEAD_DIM, :] / acc[HEAD_DIM:, :]


def _key_minus_query(keys, queries):
    return (lax.broadcasted_iota(jnp.int32, (keys, queries), 0)
            - lax.broadcasted_iota(jnp.int32, (keys, queries), 1))


def _pipelined_sweep(qi, qk, softmax_pv, s0_ref, s1_ref):
    assert ATT_TQ == 2 * ATT_TK
    qk(0, s0_ref)

    def body(i, carry):
        kt = 2 * i
        qk(kt + 1, s1_ref)
        softmax_pv(kt, s0_ref, None)
        qk(kt + 2, s0_ref)
        softmax_pv(kt + 1, s1_ref, None)
        return carry

    lax.fori_loop(0, qi, body, 0)
    kt = 2 * qi
    qk(kt + 1, s1_ref)
    softmax_pv(kt, s0_ref, 0)
    softmax_pv(kt + 1, s1_ref, -ATT_TK)


def _pair_rows(acc_ref):
    return jnp.concatenate([_normalized_t(acc_ref[hh]) for hh in range(2)], axis=0).T


def _pick_col(x, col):
    lane = lax.broadcasted_iota(jnp.int32, x.shape, 1)
    return jnp.sum(jnp.where(lane == col, x, 0.0), axis=1, keepdims=True)


def _nsa_attn_kernel(q_ref, ksel_ref, vselt_ref, kwin_ref, vwint_ref, negsel_ref, oc_ref, gate_ref, z_ref,
                     o_ref, qcat_ref, s0_ref, s1_ref, m_ref, acc_ref):
    tq, tk = ATT_TQ, ATT_TK
    g = pl.program_id(1)
    j = pl.program_id(2)
    qi = pl.program_id(3)
    q2 = q_ref[...]
    negsel = negsel_ref[...]
    for hh in range(2):
        qm = jnp.where(_half_mask(q2.shape, hh), q2, jnp.zeros_like(q2))
        qcat_ref[hh] = jnp.concatenate([qm, negsel], axis=1)
    _init_state(m_ref, acc_ref)
    blocks_per_tile = tk // SEL_BLOCK

    def qk(kt, s_ref):
        start = pl.multiple_of(kt * tk, tk)
        key_block = (jnp.right_shift(lax.broadcasted_iota(jnp.int32, (tk, LANES), 0), SEL_SHIFT)
                     + kt * blocks_per_tile)
        onehot = jnp.where(key_block == lax.broadcasted_iota(jnp.int32, (tk, LANES), 1),
                           1.0, 0.0).astype(jnp.bfloat16)
        kcat = jnp.concatenate([ksel_ref[pl.ds(start, tk), :], onehot], axis=1)
        for hh in range(2):
            s_ref[hh] = _dot_t(kcat, qcat_ref[hh])

    def softmax_pv(kt, s_ref, off):
        start = pl.multiple_of(kt * tk, tk)
        for hh in range(2):
            _online_step_t(s_ref[hh], vselt_ref[:, pl.ds(start, tk)], m_ref, acc_ref, hh, off)

    _pipelined_sweep(qi, qk, softmax_pv, s0_ref, s1_ref)
    o_sel = _pair_rows(acc_ref)

    _init_state(m_ref, acc_ref)
    for half in range(tq // tk):
        cols = slice(half * tk, (half + 1) * tk)
        wt = qi * (tq // tk) + half
        d0 = pl.multiple_of(wt * tk, tk)
        p0 = pl.multiple_of(jnp.maximum(wt - 1, 0) * tk, tk)
        prev_lo = jnp.where(wt > 0, 1, tk)
        for hh in range(2):
            qm = qcat_ref[hh, cols, :LANES]
            state = (hh, slice(None), cols)
            _online_step_t(_dot_t(kwin_ref[pl.ds(d0, tk), :], qm), vwint_ref[:, pl.ds(d0, tk)],
                           m_ref, acc_ref, state, off=0)
            _online_step_t(_dot_t(kwin_ref[pl.ds(p0, tk), :], qm), vwint_ref[:, pl.ds(p0, tk)],
                           m_ref, acc_ref, state, lo=prev_lo)
    o_win = _pair_rows(acc_ref)

    gates = gate_ref[...]
    head0 = (g * NSA_HPG + j * 2) * 3
    low = lax.broadcasted_iota(jnp.int32, (tq, LANES), 1) < HEAD_DIM
    mix = [jnp.where(low, _pick_col(gates, head0 + i), _pick_col(gates, head0 + 3 + i)) for i in range(3)]
    out = mix[0] * oc_ref[...] + mix[1] * o_sel + mix[2] * o_win
    o_ref[...] = (out * z_ref[...]).astype(o_ref.dtype)


def _nsa_attn(p, negsel, o_cmp, batch, seq):
    m = p['nsa_q'].shape[0]
    t = ATT_TQ
    nq = seq // t
    sq = None
    row_blk = lambda b, g, j, i: (b * nq + i, g * 2 + j)
    k_spec = pl.BlockSpec((seq, LANES), lambda b, g, j, i: (b, g))
    vt_spec = pl.BlockSpec((sq, LANES, seq), lambda b, g, j, i: (b, g, 0))
    return pl.pallas_call(
        _nsa_attn_kernel,
        grid=(batch, NSA_KV_GROUPS, NSA_HPG // 2, nq),
        in_specs=[pl.BlockSpec((t, LANES), row_blk),
                  k_spec, vt_spec, k_spec, vt_spec,
                  pl.BlockSpec((sq, sq, t, LANES), lambda b, g, j, i: (b, g, i, 0)),
                  pl.BlockSpec((t, LANES), row_blk),
                  pl.BlockSpec((t, LANES), lambda b, g, j, i: (b * nq + i, 0)),
                  pl.BlockSpec((t, LANES), row_blk)],
        out_specs=pl.BlockSpec((t, LANES), row_blk),
        out_shape=jax.ShapeDtypeStruct((m, NSA_W), jnp.bfloat16),
        scratch_shapes=[pltpu.VMEM((2, t, 2 * LANES), jnp.bfloat16),
                        pltpu.VMEM((2, ATT_TK, t), jnp.float32),
                        pltpu.VMEM((2, ATT_TK, t), jnp.float32),
                        pltpu.VMEM((2, 1, t), jnp.float32),
                        pltpu.VMEM((2, LANES, t), jnp.float32)],
        compiler_params=_cparams(("parallel", "parallel", "parallel", "arbitrary")),
        name="nsa_sel_win_attn",
    )(p['nsa_q'], p['nsa_k_sel'], p['nsa_v_sel_t'], p['nsa_k_win'], p['nsa_v_win_t'], negsel, o_cmp,
      p['nsa_gate'], p['nsa_z'])


def _fox_kernel(q_ref, k_ref, vt_ref, z_ref, o_ref, s0_ref, s1_ref, m_ref, acc_ref):
    tk = ATT_TK
    qi = pl.program_id(2)
    _init_state(m_ref, acc_ref)

    def qk(kt, s_ref):
        start = pl.multiple_of(kt * tk, tk)
        for hh in range(2):
            cat = slice(hh * FOX_CAT, (hh + 1) * FOX_CAT)
            s_ref[hh] = _dot_t(k_ref[pl.ds(start, tk), cat], q_ref[:, cat])

    def softmax_pv(kt, s_ref, off):
        start = pl.multiple_of(kt * tk, tk)
        for hh in range(2):
            _online_step_t(s_ref[hh], vt_ref[hh * LANES:(hh + 1) * LANES, pl.ds(start, tk)],
                           m_ref, acc_ref, hh, off)

    _pipelined_sweep(qi, qk, softmax_pv, s0_ref, s1_ref)
    o_ref[...] = (_pair_rows(acc_ref) * z_ref[...]).astype(o_ref.dtype)


def _fox_attn(p, qcat, kcat, batch, seq):
    m = qcat.shape[0]
    t = ATT_TQ
    nq = seq // t
    row_blk = lambda b, j, i: (b * nq + i, j)
    return pl.pallas_call(
        _fox_kernel,
        grid=(batch, FOX_HEADS // 2, nq),
        in_specs=[pl.BlockSpec((t, 2 * FOX_CAT), row_blk),
                  pl.BlockSpec((seq, 2 * FOX_CAT), lambda b, j, i: (b, j)),
                  pl.BlockSpec((None, 2 * LANES, seq), lambda b, j, i: (b, j, 0)),
                  pl.BlockSpec((t, LANES), row_blk)],
        out_specs=pl.BlockSpec((t, LANES), row_blk),
        out_shape=jax.ShapeDtypeStruct((m, FOX_W), jnp.bfloat16),
        scratch_shapes=[pltpu.VMEM((2, ATT_TK, t), jnp.float32),
                        pltpu.VMEM((2, ATT_TK, t), jnp.float32),
                        pltpu.VMEM((2, 1, t), jnp.float32),
                        pltpu.VMEM((2, LANES, t), jnp.float32)],
        compiler_params=_cparams(("parallel", "parallel", "arbitrary")),
        name="fox_attn",
    )(qcat, kcat, p['fox_vt'], p['fox_z'])


def _diff_kernel(lam_init, q_ref, k_ref, vt_ref, z_ref, lam_ref, g_ref, o_ref, qm_ref, s0_ref, s1_ref,
                 m_ref, acc_ref):
    tq, tk = ATT_TQ, ATT_TK
    qi = pl.program_id(2)
    q2 = q_ref[...]
    lane = lax.broadcasted_iota(jnp.int32, (tq, LANES), 1)
    for idx in range(4):
        lo = idx * DIFF_QK_DIM
        qm_ref[idx] = jnp.where((lane >= lo) & (lane < lo + DIFF_QK_DIM), q2, jnp.zeros_like(q2))
    _init_state(m_ref, acc_ref)

    def qk(kt, s_ref):
        start = pl.multiple_of(kt * tk, tk)
        k = k_ref[pl.ds(start, tk), :]
        for idx in range(4):
            s_ref[idx] = _dot_t(k, qm_ref[idx])

    def softmax_pv(kt, s_ref, off):
        start = pl.multiple_of(kt * tk, tk)
        for idx in range(4):
            hh = idx // 2
            _online_step_t(s_ref[idx], vt_ref[hh * LANES:(hh + 1) * LANES, pl.ds(start, tk)],
                           m_ref, acc_ref, idx, off)

    _pipelined_sweep(qi, qk, softmax_pv, s0_ref, s1_ref)

    lam_rows = lam_ref[...]
    lam = (jnp.exp(jnp.sum(lam_rows[0:1, :] * lam_rows[1:2, :], axis=1, keepdims=True))
           - jnp.exp(jnp.sum(lam_rows[2:3, :] * lam_rows[3:4, :], axis=1, keepdims=True)) + lam_init)
    o = [_normalized_t(acc_ref[idx]) for idx in range(4)]
    normed = []
    for hh in range(2):
        d = o[2 * hh] - lam * o[2 * hh + 1]
        ms = jnp.sum(d * d, axis=0, keepdims=True) / HEAD_DIM
        normed.append(d * lax.rsqrt(ms + RMS_EPS))
    out = jnp.concatenate(normed, axis=0).T * g_ref[...] * (1.0 - lam_init)
    o_ref[...] = (out * z_ref[...]).astype(o_ref.dtype)


def _diff_attn(p, lam_rows, g_row, lam_init, batch, seq):
    m = p['diff_q'].shape[0]
    t = ATT_TQ
    nq = seq // t
    row_blk = lambda b, j, i: (b * nq + i, j)
    return pl.pallas_call(
        partial(_diff_kernel, lam_init),
        grid=(batch, DIFF_HEADS // 2, nq),
        in_specs=[pl.BlockSpec((t, LANES), row_blk),
                  pl.BlockSpec((seq, LANES), lambda b, j, i: (b, j)),
                  pl.BlockSpec((None, 2 * LANES, seq), lambda b, j, i: (b, j, 0)),
                  pl.BlockSpec((t, LANES), row_blk),
                  pl.BlockSpec((8, LANES), lambda b, j, i: (0, 0)),
                  pl.BlockSpec((1, LANES), lambda b, j, i: (0, 0))],
        out_specs=pl.BlockSpec((t, LANES), row_blk),
        out_shape=jax.ShapeDtypeStruct((m, DIFF_W), jnp.bfloat16),
        scratch_shapes=[pltpu.VMEM((4, t, LANES), jnp.bfloat16),
                        pltpu.VMEM((4, ATT_TK, t), jnp.float32),
                        pltpu.VMEM((4, ATT_TK, t), jnp.float32),
                        pltpu.VMEM((4, 1, t), jnp.float32),
                        pltpu.VMEM((4, LANES, t), jnp.float32)],
        compiler_params=_cparams(("parallel", "parallel", "arbitrary")),
        name="diff_attn",
    )(p['diff_q'], p['diff_k'], p['diff_vt'], p['diff_z'], lam_rows, g_row)


def _out_kernel(of_ref, on_ref, od_ref, w_ref, x_ref, g_ref, b_ref, o_ref):
    y = (DEEPNORM_ALPHA * x_ref[...]
         + _dot(of_ref[...], w_ref[0:FOX_W, :])
         + _dot(on_ref[...], w_ref[FOX_W:FOX_W + NSA_W, :])
         + _dot(od_ref[...], w_ref[FOX_W + NSA_W:, :]))
    mu = jnp.mean(y, axis=1, keepdims=True)
    yc = y - mu
    var = jnp.mean(yc * yc, axis=1, keepdims=True)
    o_ref[...] = yc * lax.rsqrt(var + LN_EPS) * g_ref[...] + b_ref[...]


def _out_proj(o_fox, o_nsa, o_diff, w_out, x2d, g_row, b_row):
    m = x2d.shape[0]
    tm = PROJ_TM
    row = lambda i: (i, 0)
    const = lambda i: (0, 0)
    return pl.pallas_call(
        _out_kernel,
        grid=(m // tm,),
        in_specs=[pl.BlockSpec((tm, FOX_W), row), pl.BlockSpec((tm, NSA_W), row),
                  pl.BlockSpec((tm, DIFF_W), row),
                  pl.BlockSpec((D_MODEL, D_MODEL), const),
                  pl.BlockSpec((tm, D_MODEL), row),
                  pl.BlockSpec((1, D_MODEL), const), pl.BlockSpec((1, D_MODEL), const)],
        out_specs=pl.BlockSpec((tm, D_MODEL), row),
        out_shape=jax.ShapeDtypeStruct((m, D_MODEL), jnp.float32),
        compiler_params=_cparams(("parallel",)),
        name="out_proj_ln",
    )(o_fox, o_nsa, o_diff, w_out, x2d, g_row, b_row)


def _overlap_matrix(seq):
    n_slc = seq // SEL_BLOCK
    cmp_start = CMP_STRIDE * np.arange(N_CMP_PAD)
    sel_start = SEL_BLOCK * np.arange(n_slc)
    ov = np.clip(np.minimum(cmp_start[:, None] + CMP_LEN, sel_start[None, :] + SEL_BLOCK)
                 - np.maximum(cmp_start[:, None], sel_start[None, :]), 0, None).astype(np.float32) / CMP_LEN
    return ov


def kernel(x, w_in, b_fox_f, cmp_pos_k, cmp_pos_v, cmp_w1_k, cmp_w2_k, cmp_w1_v, cmp_w2_v,
           lam_q1, lam_k1, lam_q2, lam_k2, diff_subln_g, w_out, ln_g, ln_b):
    batch, seq, _ = x.shape
    assert seq % ATT_TQ == 0 and seq // SEL_BLOCK == LANES and (seq - CMP_LEN) // CMP_STRIDE + 1 < N_CMP_PAD
    assert WINDOW == ATT_TK and ATT_TQ % ATT_TK == 0
    m = batch * seq
    tables = _rope_tables(seq, HEAD_DIM) + _rope_tables(seq, DIFF_QK_DIM)
    overlap = jnp.asarray(_overlap_matrix(seq), jnp.bfloat16)

    x2d = x.reshape(m, D_MODEL)
    for l in range(DEPTH):
        w_perm = _permute_columns(w_in[l], PROJ_COLS).astype(jnp.bfloat16)
        wt_perm = _permute_columns(w_in[l], VT_COLS).astype(jnp.bfloat16).T
        bias_row = jnp.zeros((1, LANES), jnp.float32).at[0, :FOX_HEADS].set(b_fox_f[l])
        p = _project(x2d, w_perm, wt_perm, bias_row, tables, seq)

        fox_qcat, fox_kcat = _fox_prep(p, batch, seq)
        o_fox = _fox_attn(p, fox_qcat, fox_kcat, batch, seq)

        kc = _compress(p['nsa_k_cmp'], cmp_pos_k[l], cmp_w1_k[l], cmp_w2_k[l], batch, seq, False)
        vc = _compress(p['nsa_v_cmp'], cmp_pos_v[l], cmp_w1_v[l], cmp_w2_v[l], batch, seq, True)
        o_cmp, negsel = _cmp_topk(p['nsa_q'], kc, vc, overlap, batch, seq)
        o_nsa = _nsa_attn(p, negsel, o_cmp, batch, seq)

        lam_init = 0.8 - 0.6 * math.exp(-0.3 * l)
        lam_rows = jnp.zeros((8, LANES), jnp.float32)
        for r, v in enumerate((lam_q1[l], lam_k1[l], lam_q2[l], lam_k2[l])):
            lam_rows = lam_rows.at[r, :DIFF_QK_DIM].set(v.astype(jnp.float32))
        g_row = jnp.tile(diff_subln_g[l].reshape(1, HEAD_DIM), (1, 2))
        o_diff = _diff_attn(p, lam_rows, g_row, lam_init, batch, seq)

        x2d = _out_proj(o_fox, o_nsa, o_diff, w_out[l].astype(jnp.bfloat16), x2d,
                        ln_g[l].reshape(1, D_MODEL), ln_b[l].reshape(1, D_MODEL))
    return x2d.reshape(batch, seq, D_MODEL)
```

```python
import math
from functools import partial

import numpy as np
import jax
import jax.numpy as jnp
from jax import lax
from jax.experimental import pallas as pl
from jax.experimental.pallas import tpu as pltpu

D_MODEL = 1024
DEPTH = 2
HEAD_DIM = 64
FOX_HEADS = 4
NSA_HEADS = 8
DIFF_HEADS = 4
FOX_W = FOX_HEADS * HEAD_DIM
NSA_W = NSA_HEADS * HEAD_DIM
DIFF_QK_DIM = HEAD_DIM // 2
DIFF_W = DIFF_HEADS * HEAD_DIM
NSA_KV_GROUPS = 2
NSA_HPG = NSA_HEADS // NSA_KV_GROUPS
NSA_KV_W = NSA_KV_GROUPS * HEAD_DIM
CMP_LEN = 32
CMP_STRIDE = 16
CMP_HIDDEN = 2 * HEAD_DIM
SEL_BLOCK = 64
SEL_SHIFT = SEL_BLOCK.bit_length() - 1
SEL_TOPK = 16
WINDOW = 512
ROPE_THETA = 10000.0
LN_EPS = 1e-5
RMS_EPS = 1e-5
NEG_INF = -1e30
FORCE_SCORE = 1e30
DEEPNORM_ALPHA = (2 * DEPTH) ** 0.25
LOG2E = math.log2(math.e)

SPLITS = (
    ('fox_q', FOX_W), ('fox_k', FOX_W), ('fox_v', FOX_W), ('fox_f', FOX_HEADS), ('fox_z', FOX_W),
    ('nsa_q', NSA_W),
    ('nsa_k_cmp', NSA_KV_W), ('nsa_v_cmp', NSA_KV_W),
    ('nsa_k_sel', NSA_KV_W), ('nsa_v_sel', NSA_KV_W),
    ('nsa_k_win', NSA_KV_W), ('nsa_v_win', NSA_KV_W),
    ('nsa_gate', 3 * NSA_HEADS), ('nsa_z', NSA_W),
    ('diff_q', DIFF_W), ('diff_k', DIFF_W), ('diff_v', DIFF_W), ('diff_z', DIFF_W),
)
_OFF = {}
_acc = 0
for _n, _w in SPLITS:
    _OFF[_n] = _acc
    _acc += _w
IN_WIDTH = _acc

LANES = 128
VMEM_LIMIT = 56 * 1024 * 1024

PROJ_TM = 512
OUT_TM = 512
CUM_T = 512
CMP_TQ = 512
ATT_TQ = 1024
ATT_TK = 512
N_CMP_PAD = 512
FOX_CAT = 2 * LANES


def _cparams(sem):
    return pltpu.CompilerParams(dimension_semantics=sem, vmem_limit_bytes=VMEM_LIMIT)


def _dot(a, b):
    return jnp.dot(a, b, preferred_element_type=jnp.float32)


def _dot_t(a, b):
    return lax.dot_general(a, b, (((1,), (1,)), ((), ())), preferred_element_type=jnp.float32)


def _split3(x):
    hi = x.astype(jnp.bfloat16)
    r1 = x - hi.astype(jnp.float32)
    mid = r1.astype(jnp.bfloat16)
    lo = (r1 - mid.astype(jnp.float32)).astype(jnp.bfloat16)
    return hi, mid, lo


def _dup(name):
    base = _OFF[name]
    cols = []
    for g in range(NSA_KV_GROUPS):
        one = list(range(base + g * HEAD_DIM, base + (g + 1) * HEAD_DIM))
        cols += one + one
    return cols


def _pad(cols, width):
    return list(cols) + [-1] * (width - len(cols))


def _rng(name, width):
    return list(range(_OFF[name], _OFF[name] + width))


Q_SCALE = HEAD_DIM ** -0.5 * LOG2E
DIFF_Q_SCALE = DIFF_QK_DIM ** -0.5 * LOG2E

PROJ_SEGS = (
    ('fox_q', _rng('fox_q', FOX_W), 'none', jnp.bfloat16, Q_SCALE),
    ('fox_k', _rng('fox_k', FOX_W), 'none', jnp.bfloat16, None),
    ('fox_z', _rng('fox_z', FOX_W), 'silu', jnp.float32, None),
    ('fox_logf', _pad(_rng('fox_f', FOX_HEADS), LANES), 'logsig', jnp.float32, None),
    ('nsa_q', _rng('nsa_q', NSA_W), 'rope64', jnp.bfloat16, Q_SCALE),
    ('nsa_k_cmp', _rng('nsa_k_cmp', NSA_KV_W), 'rope64', jnp.float32, None),
    ('nsa_v_cmp', _rng('nsa_v_cmp', NSA_KV_W), 'none', jnp.float32, None),
    ('nsa_k_sel', _dup('nsa_k_sel'), 'rope64', jnp.bfloat16, None),
    ('nsa_k_win', _dup('nsa_k_win'), 'rope64', jnp.bfloat16, None),
    ('nsa_gate', _pad(_rng('nsa_gate', 3 * NSA_HEADS), LANES), 'sigmoid', jnp.float32, None),
    ('nsa_z', _rng('nsa_z', NSA_W), 'silu', jnp.float32, None),
    ('diff_q', _rng('diff_q', DIFF_W), 'rope32', jnp.bfloat16, DIFF_Q_SCALE),
    ('diff_k', _rng('diff_k', DIFF_W), 'rope32', jnp.bfloat16, None),
    ('diff_z', _rng('diff_z', DIFF_W), 'silu', jnp.float32, None),
)
PROJ_COLS = np.concatenate([np.asarray(seg[1], np.int32) for seg in PROJ_SEGS])
PROJ_N = int(PROJ_COLS.shape[0])

VT_SEGS = (
    ('fox_vt', _rng('fox_v', FOX_W)),
    ('nsa_v_sel_t', _rng('nsa_v_sel', NSA_KV_W)),
    ('nsa_v_win_t', _rng('nsa_v_win', NSA_KV_W)),
    ('diff_vt', _rng('diff_v', DIFF_W)),
)
VT_COLS = np.concatenate([np.asarray(seg[1], np.int32) for seg in VT_SEGS])
VT_N = int(VT_COLS.shape[0])


def _permute_columns(w, col_ids):
    pieces, start = [], 0
    cols = col_ids.tolist()
    while start < len(cols):
        end = start + 1
        if cols[start] < 0:
            while end < len(cols) and cols[end] < 0:
                end += 1
            pieces.append(jnp.zeros((w.shape[0], end - start), w.dtype))
        else:
            while end < len(cols) and cols[end] == cols[end - 1] + 1:
                end += 1
            pieces.append(w[:, cols[start]:cols[end - 1] + 1])
        start = end
    return jnp.concatenate(pieces, axis=1)


def _rope_chunk(y, cos, sin, half):
    lane = lax.broadcasted_iota(jnp.int32, y.shape, 1)
    first = (lane & (2 * half - 1)) < half
    partner = jnp.where(first, pltpu.roll(y, LANES - half, 1), pltpu.roll(y, half, 1))
    return y * cos + partner * sin


def _proj_kernel(x_ref, w_ref, wt_ref, bias_ref, cos64_ref, sin64_ref, cos32_ref, sin32_ref, *out_refs):
    xb = x_ref[...].astype(jnp.bfloat16)
    start = 0
    for (name, cols), o_ref in zip(VT_SEGS, out_refs[len(PROJ_SEGS):]):
        rows = len(cols)
        yt = _dot_t(wt_ref[start:start + rows, :], xb).astype(o_ref.dtype)
        start += rows
        ones = jnp.ones((HEAD_DIM, yt.shape[1]), o_ref.dtype)
        for h in range(rows // HEAD_DIM):
            o_ref[2 * h * HEAD_DIM:(2 * h + 1) * HEAD_DIM, :] = yt[h * HEAD_DIM:(h + 1) * HEAD_DIM, :]
            o_ref[(2 * h + 1) * HEAD_DIM:(2 * h + 2) * HEAD_DIM, :] = ones
    start = 0
    for (name, cols, epi, dtype, scale), o_ref in zip(PROJ_SEGS, out_refs):
        width = len(cols)
        y = _dot(xb, w_ref[:, start:start + width])
        start += width
        if epi == 'silu':
            y = y * jax.nn.sigmoid(y)
        elif epi == 'sigmoid':
            y = jax.nn.sigmoid(y)
        elif epi == 'logsig':
            y = y + bias_ref[...]
            y = jnp.minimum(y, 0.0) - jnp.log1p(jnp.exp(-jnp.abs(y)))
        elif epi in ('rope64', 'rope32'):
            half = DIFF_QK_DIM // 2 if epi == 'rope32' else HEAD_DIM // 2
            cos = cos32_ref[...] if epi == 'rope32' else cos64_ref[...]
            sin = sin32_ref[...] if epi == 'rope32' else sin64_ref[...]
            chunks = [_rope_chunk(y[:, c * LANES:(c + 1) * LANES], cos, sin, half)
                      for c in range(width // LANES)]
            y = chunks[0] if len(chunks) == 1 else jnp.concatenate(chunks, axis=1)
        if scale is not None:
            y = y * scale
        o_ref[...] = y.astype(dtype)


def _rope_tables(seq, group):
    half = group // 2
    inv = ROPE_THETA ** (-(jnp.arange(half, dtype=jnp.float32) * 2.0 / group))
    ang = jnp.arange(seq, dtype=jnp.float32)[:, None] * inv[None, :]
    cos, sin = jnp.cos(ang), jnp.sin(ang)
    reps = LANES // group
    cos_t = jnp.tile(jnp.concatenate([cos, cos], axis=1), (1, reps))
    sin_t = jnp.tile(jnp.concatenate([-sin, sin], axis=1), (1, reps))
    return cos_t, sin_t


def _project(x2d, w_perm, wt_perm, bias_row, tables, seq):
    m = x2d.shape[0]
    nt = seq // PROJ_TM
    tab_spec = pl.BlockSpec((PROJ_TM, LANES), lambda i: (i % nt, 0))
    out_shape = [jax.ShapeDtypeStruct((m, len(seg[1])), seg[3]) for seg in PROJ_SEGS]
    out_specs = [pl.BlockSpec((PROJ_TM, len(seg[1])), lambda i: (i, 0)) for seg in PROJ_SEGS]
    out_shape += [jax.ShapeDtypeStruct((m // seq, 2 * len(seg[1]), seq), jnp.bfloat16) for seg in VT_SEGS]
    out_specs += [pl.BlockSpec((None, 2 * len(seg[1]), PROJ_TM), lambda i: (i // nt, 0, i % nt))
                  for seg in VT_SEGS]
    outs = pl.pallas_call(
        _proj_kernel,
        grid=(m // PROJ_TM,),
        in_specs=[pl.BlockSpec((PROJ_TM, D_MODEL), lambda i: (i, 0)),
                  pl.BlockSpec((D_MODEL, PROJ_N), lambda i: (0, 0)),
                  pl.BlockSpec((VT_N, D_MODEL), lambda i: (0, 0)),
                  pl.BlockSpec((1, LANES), lambda i: (0, 0)),
                  tab_spec, tab_spec, tab_spec, tab_spec],
        out_specs=out_specs,
        out_shape=out_shape,
        compiler_params=_cparams(("parallel",)),
        name="in_proj",
    )(x2d, w_perm, wt_perm, bias_row, *tables)
    return {seg[0]: o for seg, o in zip(PROJ_SEGS + VT_SEGS, outs)}


def _fox_prep_kernel(lf_ref, q_ref, k_ref, qcat_ref, kcat_ref, carry_ref):
    @pl.when(pl.program_id(1) == 0)
    def _():
        carry_ref[...] = jnp.zeros_like(carry_ref)

    t = CUM_T
    row = lax.broadcasted_iota(jnp.int32, (t, t), 0)
    col = lax.broadcasted_iota(jnp.int32, (t, t), 1)
    tri = jnp.where(col <= row, 1.0, 0.0).astype(jnp.bfloat16)
    hi, mid, lo = _split3(lf_ref[...])
    c = _dot(tri, hi) + _dot(tri, mid) + _dot(tri, lo) + carry_ref[0:1, :]
    carry_ref[0:1, :] = c[t - 1:t, :]

    terms = _split3(c * LOG2E)
    src = lax.broadcasted_iota(jnp.int32, (LANES, FOX_HEADS * LANES), 0)
    dst = lax.broadcasted_iota(jnp.int32, (LANES, FOX_HEADS * LANES), 1)

    def spread(x, slot):
        place = jnp.where(dst == src * LANES + slot, 1.0, 0.0).astype(jnp.bfloat16)
        return _dot(x, place)

    n_terms = len(terms)
    qa = sum(spread(x, i) for i, x in enumerate(terms))
    ka = -sum(spread(x, n_terms + i) for i, x in enumerate(terms))
    lane = lax.broadcasted_iota(jnp.int32, qa.shape, 1) & (LANES - 1)
    qa = jnp.where((lane >= n_terms) & (lane < 2 * n_terms), 1.0, qa).astype(jnp.bfloat16)
    ka = jnp.where(lane < n_terms, 1.0, ka).astype(jnp.bfloat16)
    q = q_ref[...]
    k = k_ref[...]
    for h in range(FOX_HEADS):
        pair, hh = divmod(h, 2)
        q2 = q[:, pair * LANES:(pair + 1) * LANES]
        base = h * FOX_CAT
        qcat_ref[:, base:base + LANES] = jnp.where(_half_mask(q2.shape, hh), q2, jnp.zeros_like(q2))
        qcat_ref[:, base + LANES:base + FOX_CAT] = qa[:, h * LANES:(h + 1) * LANES]
        kcat_ref[:, base:base + LANES] = k[:, pair * LANES:(pair + 1) * LANES]
        kcat_ref[:, base + LANES:base + FOX_CAT] = ka[:, h * LANES:(h + 1) * LANES]


def _fox_prep(p, batch, seq):
    m = p['fox_q'].shape[0]
    nt = seq // CUM_T
    row = lambda b, j: (b * nt + j, 0)
    cat = jax.ShapeDtypeStruct((m, FOX_HEADS * FOX_CAT), jnp.bfloat16)
    return pl.pallas_call(
        _fox_prep_kernel,
        grid=(batch, nt),
        in_specs=[pl.BlockSpec((CUM_T, LANES), row), pl.BlockSpec((CUM_T, FOX_W), row),
                  pl.BlockSpec((CUM_T, FOX_W), row)],
        out_specs=[pl.BlockSpec((CUM_T, FOX_HEADS * FOX_CAT), row)] * 2,
        out_shape=[cat, cat],
        scratch_shapes=[pltpu.VMEM((8, LANES), jnp.float32)],
        compiler_params=_cparams(("arbitrary", "arbitrary")),
        name="fox_prep",
    )(p['fox_logf'], p['fox_q'], p['fox_k'])


def _compress_kernel(ones_half, ch_ref, pos_ref, w1_ref, w2_ref, o_ref):
    n_chunks = ch_ref.shape[0] // CMP_STRIDE
    ch = jnp.concatenate([ch_ref[pl.ds(i, n_chunks, stride=CMP_STRIDE), :] for i in range(CMP_STRIDE)], axis=1)
    a_in = (ch + pos_ref[0:1, :]).astype(jnp.bfloat16)
    b_in = (ch + pos_ref[1:2, :]).astype(jnp.bfloat16)
    row = lax.broadcasted_iota(jnp.int32, (n_chunks, LANES), 0)
    for g in range(NSA_KV_GROUPS):
        a = _dot(a_in, w1_ref[g, 0])
        b = _dot(b_in, w1_ref[g, 1])
        h = a + pltpu.roll(b, n_chunks - 1, 0)
        act = (h * jax.nn.sigmoid(h)).astype(jnp.bfloat16)
        out = jnp.where(row < n_chunks - 1, _dot(act, w2_ref[...]), 0.0)
        if ones_half:
            out_t = out.T
            row_t = lax.broadcasted_iota(jnp.int32, out_t.shape, 0)
            o_ref[g] = jnp.where(row_t >= HEAD_DIM, 1.0, out_t).astype(o_ref.dtype)
        else:
            o_ref[g] = out.astype(o_ref.dtype)


def _compress(tok, pos_emb, w1, w2, batch, seq, ones_half):
    n_chunks = seq // CMP_STRIDE
    width = CMP_STRIDE * NSA_KV_W
    halves = w1.reshape(2, CMP_STRIDE, HEAD_DIM, CMP_HIDDEN)
    w1s = jnp.zeros((NSA_KV_GROUPS, 2, CMP_STRIDE, NSA_KV_GROUPS, HEAD_DIM, CMP_HIDDEN), w1.dtype)
    for g in range(NSA_KV_GROUPS):
        w1s = w1s.at[g, :, :, g].set(halves)
    w1s = w1s.reshape(NSA_KV_GROUPS, 2, width, CMP_HIDDEN).astype(jnp.bfloat16)
    pos = jnp.broadcast_to(pos_emb.reshape(2, CMP_STRIDE, 1, HEAD_DIM),
                           (2, CMP_STRIDE, NSA_KV_GROUPS, HEAD_DIM)).reshape(2, width)
    w2d = jnp.concatenate([w2, jnp.zeros_like(w2) if ones_half else w2], axis=1).astype(jnp.bfloat16)
    sq = None
    out_dims = (LANES, n_chunks) if ones_half else (n_chunks, LANES)
    return pl.pallas_call(
        partial(_compress_kernel, ones_half),
        grid=(batch,),
        in_specs=[pl.BlockSpec((seq, NSA_KV_W), lambda b: (b, 0)),
                  pl.BlockSpec((2, width), lambda b: (0, 0)),
                  pl.BlockSpec((NSA_KV_GROUPS, 2, width, CMP_HIDDEN), lambda b: (0, 0, 0, 0)),
                  pl.BlockSpec((CMP_HIDDEN, LANES), lambda b: (0, 0))],
        out_specs=pl.BlockSpec((sq, NSA_KV_GROUPS) + out_dims, lambda b: (b, 0, 0, 0)),
        out_shape=jax.ShapeDtypeStruct((batch, NSA_KV_GROUPS) + out_dims, jnp.bfloat16),
        compiler_params=_cparams(("parallel",)),
        name="nsa_compress_v" if ones_half else "nsa_compress_k",
    )(tok, pos, w1s, w2d)


def _half_mask(shape, hh):
    lane = lax.broadcasted_iota(jnp.int32, shape, 1)
    return (lane >= hh * HEAD_DIM) & (lane < (hh + 1) * HEAD_DIM)


def _cmp_topk_kernel(q_ref, kc_ref, vct_ref, ovt_ref, tri_ref, oc_ref, negsel_ref, imp_ref):
    tq = q_ref.shape[0]
    step = pl.program_id(2)
    t = step * tq + lax.broadcasted_iota(jnp.int32, (1, tq), 1)
    any_visible = jnp.where(t >= CMP_LEN - 1, 1.0, 0.0)

    def attend(rows):
        n = lax.broadcasted_iota(jnp.int32, (rows, 1), 0)
        mask_c = (CMP_STRIDE * n + (CMP_LEN - 1)) <= t
        kc = kc_ref[0:rows, :]
        vct = vct_ref[:, 0:rows]
        psum = jnp.zeros((rows, tq), jnp.float32)
        for j in range(NSA_HPG // 2):
            q2 = q_ref[:, j * LANES:(j + 1) * LANES]
            o_pair = []
            for hh in range(2):
                qm = jnp.where(_half_mask(q2.shape, hh), q2, jnp.zeros_like(q2))
                st = jnp.where(mask_c, _dot_t(kc, qm), NEG_INF)
                e = jnp.exp2(st - _key_max(st))
                acc = _dot(vct, e.astype(jnp.bfloat16))
                inv = any_visible / acc[HEAD_DIM:HEAD_DIM + 1, :]
                psum = psum + e * inv
                o_pair.append(acc[:HEAD_DIM, :] * inv)
            oc_ref[:, j * LANES:(j + 1) * LANES] = jnp.concatenate(o_pair, axis=0).T
        p_hi = psum.astype(jnp.bfloat16)
        p_lo = (psum - p_hi.astype(jnp.float32)).astype(jnp.bfloat16)
        ovt = ovt_ref[:, 0:rows]
        imp_ref[...] = _dot(ovt, p_hi) + _dot(ovt, p_lo)

    steps_per_variant = (LANES * CMP_STRIDE) // tq
    for variant in range(N_CMP_PAD // LANES):
        @pl.when(step // steps_per_variant == variant)
        def _():
            attend((variant + 1) * LANES)

    blk = lax.broadcasted_iota(jnp.int32, (LANES, 1), 0)
    cur = jnp.right_shift(t, SEL_SHIFT)
    valid = blk * SEL_BLOCK <= t
    forced = (blk == 0) | (blk == cur) | (blk == cur - 1)
    score = jnp.where(valid, jnp.where(forced, FORCE_SCORE, imp_ref[...]), NEG_INF)
    tri = tri_ref[...]
    budget = jnp.full((1, tq), float(SEL_TOPK), jnp.float32)
    sel = jnp.zeros((LANES, tq), jnp.float32)
    for _ in range(SEL_TOPK):
        level = score == _key_max(score)
        score = jnp.where(level, -jnp.inf, score)
        rank = _dot(tri, jnp.where(level, 1.0, 0.0).astype(jnp.bfloat16))
        sel = jnp.where(level & (rank <= budget), 1.0, sel)
        budget = jnp.maximum(budget - rank[LANES - 1:LANES, :], 0.0)
    negsel_ref[...] = jnp.where(sel > 0.5, 0.0, NEG_INF).T.astype(negsel_ref.dtype)


def _cmp_topk(nsa_q, kc, vc, overlap, batch, seq):
    m = nsa_q.shape[0]
    nq = seq // CMP_TQ
    sq = None
    idx = np.arange(LANES)
    tri = jnp.asarray((idx[None, :] <= idx[:, None]).astype(np.float32), jnp.bfloat16)
    return pl.pallas_call(
        _cmp_topk_kernel,
        grid=(batch, NSA_KV_GROUPS, nq),
        in_specs=[pl.BlockSpec((CMP_TQ, NSA_HPG * HEAD_DIM), lambda b, g, i: (b * nq + i, g)),
                  pl.BlockSpec((sq, sq, N_CMP_PAD, LANES), lambda b, g, i: (b, g, 0, 0)),
                  pl.BlockSpec((sq, sq, LANES, N_CMP_PAD), lambda b, g, i: (b, g, 0, 0)),
                  pl.BlockSpec((LANES, N_CMP_PAD), lambda b, g, i: (0, 0)),
                  pl.BlockSpec((LANES, LANES), lambda b, g, i: (0, 0))],
        out_specs=[pl.BlockSpec((CMP_TQ, NSA_HPG * HEAD_DIM), lambda b, g, i: (b * nq + i, g)),
                   pl.BlockSpec((sq, sq, CMP_TQ, LANES), lambda b, g, i: (b, g, i, 0))],
        out_shape=[jax.ShapeDtypeStruct((m, NSA_W), jnp.float32),
                   jax.ShapeDtypeStruct((batch, NSA_KV_GROUPS, seq, LANES), jnp.bfloat16)],
        scratch_shapes=[pltpu.VMEM((LANES, CMP_TQ), jnp.float32)],
        compiler_params=_cparams(("parallel", "parallel", "parallel")),
        name="nsa_cmp_topk",
    )(nsa_q, kc, vc, overlap.T, tri)


def _init_state(m_ref, acc_ref):
    m_ref[...] = jnp.full(m_ref.shape, NEG_INF, jnp.float32)
    acc_ref[...] = jnp.zeros(acc_ref.shape, jnp.float32)


def _key_max(st):
    rows = st.shape[0]
    while rows > 8:
        rows //= 2
        st = jnp.maximum(st[:rows], st[rows:])
    return jnp.max(st, axis=0, keepdims=True)


def _online_step_t(st, vt, m_ref, acc_ref, idx, off=None, lo=None):
    if off is not None:
        st = jnp.where(_key_minus_query(*st.shape) <= off, st, NEG_INF)
    if lo is not None:
        st = jnp.where(_key_minus_query(*st.shape) >= lo, st, NEG_INF)
    m_old = m_ref[idx]
    m_new = jnp.maximum(m_old, _key_max(st))
    alpha = jnp.exp2(m_old - m_new)
    pt = jnp.exp2(st - m_new).astype(jnp.bfloat16)
    acc_ref[idx] = alpha * acc_ref[idx] + _dot(vt, pt)
    m_ref[idx] = m_new


def _normalized_t(acc):
    return acc[:HEAD_DIM, :] / acc[HEAD_DIM:, :]


def _key_minus_query(keys, queries):
    return (lax.broadcasted_iota(jnp.int32, (keys, queries), 0)
            - lax.broadcasted_iota(jnp.int32, (keys, queries), 1))


def _pipelined_sweep(qi, parts, qk, softmax_pv, s0_ref, s1_ref, before_tail=None):
    assert ATT_TQ == 2 * ATT_TK
    late = slice(ATT_TK, ATT_TQ)

    def step(kt_next, s_next, kt, s_cur, off, cols=None):
        for part in range(parts):
            qk(kt_next, s_next, part, cols)
            softmax_pv(kt, s_cur, off, part, None)

    for part in range(parts):
        qk(0, s0_ref, part, None)

    def body(i, carry):
        kt = 2 * i
        step(kt + 1, s1_ref, kt, s0_ref, None)
        step(kt + 2, s0_ref, kt + 1, s1_ref, None)
        return carry

    lax.fori_loop(0, qi, body, 0)
    kt = 2 * qi
    for part in range(parts):
        qk(kt + 1, s1_ref, part, late)
    if before_tail is not None:
        before_tail()
    for part in range(parts):
        softmax_pv(kt, s0_ref, 0, part, None)
    for part in range(parts):
        softmax_pv(kt + 1, s1_ref, 0, part, late)


def _at(part, cols):
    return part if cols is None else (part, slice(None), cols)


def _pair_rows(acc_ref):
    return jnp.concatenate([_normalized_t(acc_ref[hh]) for hh in range(2)], axis=0).T


def _pick_col(x, col):
    lane = lax.broadcasted_iota(jnp.int32, x.shape, 1)
    return jnp.sum(jnp.where(lane == col, x, 0.0), axis=1, keepdims=True)


def _nsa_attn_kernel(q_ref, ksel_ref, vselt_ref, kwin_ref, vwint_ref, negsel_ref, oc_ref, gate_ref, z_ref,
                     o_ref, qcat_ref, s0_ref, s1_ref, w_ref, m_ref, acc_ref, mw_ref, accw_ref):
    tq, tk = ATT_TQ, ATT_TK
    g = pl.program_id(1)
    j = pl.program_id(2)
    qi = pl.program_id(3)
    q2 = q_ref[...]
    negsel = negsel_ref[...]
    for hh in range(2):
        qm = jnp.where(_half_mask(q2.shape, hh), q2, jnp.zeros_like(q2))
        qcat_ref[hh] = jnp.concatenate([qm, negsel], axis=1)
    _init_state(m_ref, acc_ref)
    blocks_per_tile = tk // SEL_BLOCK

    def qk(kt, s_ref, hh, cols):
        start = pl.multiple_of(kt * tk, tk)
        key_block = (jnp.right_shift(lax.broadcasted_iota(jnp.int32, (tk, LANES), 0), SEL_SHIFT)
                     + kt * blocks_per_tile)
        onehot = jnp.where(key_block == lax.broadcasted_iota(jnp.int32, (tk, LANES), 1),
                           1.0, 0.0).astype(jnp.bfloat16)
        kcat = jnp.concatenate([ksel_ref[pl.ds(start, tk), :], onehot], axis=1)
        q = qcat_ref[hh] if cols is None else qcat_ref[hh, cols, :]
        s_ref[_at(hh, cols)] = _dot_t(kcat, q)

    def softmax_pv(kt, s_ref, off, hh, cols):
        start = pl.multiple_of(kt * tk, tk)
        _online_step_t(s_ref[_at(hh, cols)], vselt_ref[:, pl.ds(start, tk)], m_ref, acc_ref, _at(hh, cols), off)

    halves = tq // tk

    def win_tiles(half):
        wt = qi * halves + half
        return (pl.multiple_of(wt * tk, tk), pl.multiple_of(jnp.maximum(wt - 1, 0) * tk, tk),
                jnp.where(wt > 0, 1, tk))

    def window_qk():
        for half in range(halves):
            cols = slice(half * tk, (half + 1) * tk)
            d0, p0, _ = win_tiles(half)
            for hh in range(2):
                qm = qcat_ref[hh, cols, :LANES]
                w_ref[hh, 0, :, cols] = _dot_t(kwin_ref[pl.ds(d0, tk), :], qm)
                w_ref[hh, 1, :, cols] = _dot_t(kwin_ref[pl.ds(p0, tk), :], qm)

    _pipelined_sweep(qi, 2, qk, softmax_pv, s0_ref, s1_ref, before_tail=window_qk)
    _init_state(mw_ref, accw_ref)
    for half in range(halves):
        cols = slice(half * tk, (half + 1) * tk)
        d0, p0, prev_lo = win_tiles(half)
        for hh in range(2):
            _online_step_t(w_ref[hh, 0, :, cols], vwint_ref[:, pl.ds(d0, tk)], mw_ref, accw_ref, _at(hh, cols),
                           off=0)
            _online_step_t(w_ref[hh, 1, :, cols], vwint_ref[:, pl.ds(p0, tk)], mw_ref, accw_ref, _at(hh, cols),
                           lo=prev_lo)
    o_sel = _pair_rows(acc_ref)
    o_win = _pair_rows(accw_ref)

    gates = gate_ref[...]
    head0 = (g * NSA_HPG + j * 2) * 3
    low = lax.broadcasted_iota(jnp.int32, (tq, LANES), 1) < HEAD_DIM
    mix = [jnp.where(low, _pick_col(gates, head0 + i), _pick_col(gates, head0 + 3 + i)) for i in range(3)]
    out = mix[0] * oc_ref[...] + mix[1] * o_sel + mix[2] * o_win
    o_ref[...] = (out * z_ref[...]).astype(o_ref.dtype)


def _nsa_attn(p, negsel, o_cmp, batch, seq):
    m = p['nsa_q'].shape[0]
    t = ATT_TQ
    nq = seq // t
    sq = None
    row_blk = lambda b, g, j, i: (b * nq + i, g * 2 + j)
    k_spec = pl.BlockSpec((seq, LANES), lambda b, g, j, i: (b, g))
    vt_spec = pl.BlockSpec((sq, LANES, seq), lambda b, g, j, i: (b, g, 0))
    return pl.pallas_call(
        _nsa_attn_kernel,
        grid=(batch, NSA_KV_GROUPS, NSA_HPG // 2, nq),
        in_specs=[pl.BlockSpec((t, LANES), row_blk),
                  k_spec, vt_spec, k_spec, vt_spec,
                  pl.BlockSpec((sq, sq, t, LANES), lambda b, g, j, i: (b, g, i, 0)),
                  pl.BlockSpec((t, LANES), row_blk),
                  pl.BlockSpec((t, LANES), lambda b, g, j, i: (b * nq + i, 0)),
                  pl.BlockSpec((t, LANES), row_blk)],
        out_specs=pl.BlockSpec((t, LANES), row_blk),
        out_shape=jax.ShapeDtypeStruct((m, NSA_W), jnp.bfloat16),
        scratch_shapes=[pltpu.VMEM((2, t, 2 * LANES), jnp.bfloat16),
                        pltpu.VMEM((2, ATT_TK, t), jnp.float32),
                        pltpu.VMEM((2, ATT_TK, t), jnp.float32),
                        pltpu.VMEM((2, 2, ATT_TK, t), jnp.float32),
                        pltpu.VMEM((2, 1, t), jnp.float32),
                        pltpu.VMEM((2, LANES, t), jnp.float32),
                        pltpu.VMEM((2, 1, t), jnp.float32),
                        pltpu.VMEM((2, LANES, t), jnp.float32)],
        compiler_params=_cparams(("parallel", "parallel", "parallel", "arbitrary")),
        name="nsa_sel_win_attn",
    )(p['nsa_q'], p['nsa_k_sel'], p['nsa_v_sel_t'], p['nsa_k_win'], p['nsa_v_win_t'], negsel, o_cmp,
      p['nsa_gate'], p['nsa_z'])


def _fox_kernel(q_ref, k_ref, vt_ref, z_ref, o_ref, s0_ref, s1_ref, m_ref, acc_ref):
    tk = ATT_TK
    qi = pl.program_id(2)
    _init_state(m_ref, acc_ref)

    def qk(kt, s_ref, hh, cols):
        start = pl.multiple_of(kt * tk, tk)
        cat = slice(hh * FOX_CAT, (hh + 1) * FOX_CAT)
        q = q_ref[:, cat] if cols is None else q_ref[cols, cat]
        s_ref[_at(hh, cols)] = _dot_t(k_ref[pl.ds(start, tk), cat], q)

    def softmax_pv(kt, s_ref, off, hh, cols):
        start = pl.multiple_of(kt * tk, tk)
        _online_step_t(s_ref[_at(hh, cols)], vt_ref[hh * LANES:(hh + 1) * LANES, pl.ds(start, tk)],
                       m_ref, acc_ref, _at(hh, cols), off)

    _pipelined_sweep(qi, 2, qk, softmax_pv, s0_ref, s1_ref)
    o_ref[...] = (_pair_rows(acc_ref) * z_ref[...]).astype(o_ref.dtype)


def _fox_attn(p, qcat, kcat, batch, seq):
    m = qcat.shape[0]
    t = ATT_TQ
    nq = seq // t
    row_blk = lambda b, j, i: (b * nq + i, j)
    return pl.pallas_call(
        _fox_kernel,
        grid=(batch, FOX_HEADS // 2, nq),
        in_specs=[pl.BlockSpec((t, 2 * FOX_CAT), row_blk),
                  pl.BlockSpec((seq, 2 * FOX_CAT), lambda b, j, i: (b, j)),
                  pl.BlockSpec((None, 2 * LANES, seq), lambda b, j, i: (b, j, 0)),
                  pl.BlockSpec((t, LANES), row_blk)],
        out_specs=pl.BlockSpec((t, LANES), row_blk),
        out_shape=jax.ShapeDtypeStruct((m, FOX_W), jnp.bfloat16),
        scratch_shapes=[pltpu.VMEM((2, ATT_TK, t), jnp.float32),
                        pltpu.VMEM((2, ATT_TK, t), jnp.float32),
                        pltpu.VMEM((2, 1, t), jnp.float32),
                        pltpu.VMEM((2, LANES, t), jnp.float32)],
        compiler_params=_cparams(("parallel", "parallel", "arbitrary")),
        name="fox_attn",
    )(qcat, kcat, p['fox_vt'], p['fox_z'])


def _diff_kernel(lam_init, q_ref, k_ref, vt_ref, z_ref, lam_ref, g_ref, o_ref, qm_ref, s0_ref, s1_ref,
                 m_ref, acc_ref):
    tq, tk = ATT_TQ, ATT_TK
    qi = pl.program_id(2)
    q2 = q_ref[...]
    lane = lax.broadcasted_iota(jnp.int32, (tq, LANES), 1)
    for idx in range(4):
        lo = idx * DIFF_QK_DIM
        qm_ref[idx] = jnp.where((lane >= lo) & (lane < lo + DIFF_QK_DIM), q2, jnp.zeros_like(q2))
    _init_state(m_ref, acc_ref)

    def qk(kt, s_ref, idx, cols):
        start = pl.multiple_of(kt * tk, tk)
        q = qm_ref[idx] if cols is None else qm_ref[idx, cols, :]
        s_ref[_at(idx, cols)] = _dot_t(k_ref[pl.ds(start, tk), :], q)

    def softmax_pv(kt, s_ref, off, idx, cols):
        start = pl.multiple_of(kt * tk, tk)
        hh = idx // 2
        _online_step_t(s_ref[_at(idx, cols)], vt_ref[hh * LANES:(hh + 1) * LANES, pl.ds(start, tk)],
                       m_ref, acc_ref, _at(idx, cols), off)

    _pipelined_sweep(qi, 4, qk, softmax_pv, s0_ref, s1_ref)

    lam_rows = lam_ref[...]
    lam = (jnp.exp(jnp.sum(lam_rows[0:1, :] * lam_rows[1:2, :], axis=1, keepdims=True))
           - jnp.exp(jnp.sum(lam_rows[2:3, :] * lam_rows[3:4, :], axis=1, keepdims=True)) + lam_init)
    o = [_normalized_t(acc_ref[idx]) for idx in range(4)]
    normed = []
    for hh in range(2):
        d = o[2 * hh] - lam * o[2 * hh + 1]
        ms = jnp.sum(d * d, axis=0, keepdims=True) / HEAD_DIM
        normed.append(d * lax.rsqrt(ms + RMS_EPS))
    out = jnp.concatenate(normed, axis=0).T * g_ref[...] * (1.0 - lam_init)
    o_ref[...] = (out * z_ref[...]).astype(o_ref.dtype)


def _diff_attn(p, lam_rows, g_row, lam_init, batch, seq):
    m = p['diff_q'].shape[0]
    t = ATT_TQ
    nq = seq // t
    row_blk = lambda b, j, i: (b * nq + i, j)
    return pl.pallas_call(
        partial(_diff_kernel, lam_init),
        grid=(batch, DIFF_HEADS // 2, nq),
        in_specs=[pl.BlockSpec((t, LANES), row_blk),
                  pl.BlockSpec((seq, LANES), lambda b, j, i: (b, j)),
                  pl.BlockSpec((None, 2 * LANES, seq), lambda b, j, i: (b, j, 0)),
                  pl.BlockSpec((t, LANES), row_blk),
                  pl.BlockSpec((8, LANES), lambda b, j, i: (0, 0)),
                  pl.BlockSpec((1, LANES), lambda b, j, i: (0, 0))],
        out_specs=pl.BlockSpec((t, LANES), row_blk),
        out_shape=jax.ShapeDtypeStruct((m, DIFF_W), jnp.bfloat16),
        scratch_shapes=[pltpu.VMEM((4, t, LANES), jnp.bfloat16),
                        pltpu.VMEM((4, ATT_TK, t), jnp.float32),
                        pltpu.VMEM((4, ATT_TK, t), jnp.float32),
                        pltpu.VMEM((4, 1, t), jnp.float32),
                        pltpu.VMEM((4, LANES, t), jnp.float32)],
        compiler_params=_cparams(("parallel", "parallel", "arbitrary")),
        name="diff_attn",
    )(p['diff_q'], p['diff_k'], p['diff_vt'], p['diff_z'], lam_rows, g_row)


def _out_kernel(of_ref, on_ref, od_ref, w_ref, x_ref, g_ref, b_ref, o_ref):
    y = (DEEPNORM_ALPHA * x_ref[...]
         + _dot(of_ref[...], w_ref[0:FOX_W, :])
         + _dot(on_ref[...], w_ref[FOX_W:FOX_W + NSA_W, :])
         + _dot(od_ref[...], w_ref[FOX_W + NSA_W:, :]))
    mu = jnp.mean(y, axis=1, keepdims=True)
    yc = y - mu
    var = jnp.mean(yc * yc, axis=1, keepdims=True)
    o_ref[...] = yc * lax.rsqrt(var + LN_EPS) * g_ref[...] + b_ref[...]


def _out_proj(o_fox, o_nsa, o_diff, w_out, x2d, g_row, b_row):
    m = x2d.shape[0]
    tm = OUT_TM
    row = lambda i: (i, 0)
    const = lambda i: (0, 0)
    return pl.pallas_call(
        _out_kernel,
        grid=(m // tm,),
        in_specs=[pl.BlockSpec((tm, FOX_W), row), pl.BlockSpec((tm, NSA_W), row),
                  pl.BlockSpec((tm, DIFF_W), row),
                  pl.BlockSpec((D_MODEL, D_MODEL), const),
                  pl.BlockSpec((tm, D_MODEL), row),
                  pl.BlockSpec((1, D_MODEL), const), pl.BlockSpec((1, D_MODEL), const)],
        out_specs=pl.BlockSpec((tm, D_MODEL), row),
        out_shape=jax.ShapeDtypeStruct((m, D_MODEL), jnp.float32),
        compiler_params=_cparams(("parallel",)),
        name="out_proj_ln",
    )(o_fox, o_nsa, o_diff, w_out, x2d, g_row, b_row)


def _overlap_matrix(seq):
    n_slc = seq // SEL_BLOCK
    cmp_start = CMP_STRIDE * np.arange(N_CMP_PAD)
    sel_start = SEL_BLOCK * np.arange(n_slc)
    ov = np.clip(np.minimum(cmp_start[:, None] + CMP_LEN, sel_start[None, :] + SEL_BLOCK)
                 - np.maximum(cmp_start[:, None], sel_start[None, :]), 0, None).astype(np.float32) / CMP_LEN
    return ov


def kernel(x, w_in, b_fox_f, cmp_pos_k, cmp_pos_v, cmp_w1_k, cmp_w2_k, cmp_w1_v, cmp_w2_v,
           lam_q1, lam_k1, lam_q2, lam_k2, diff_subln_g, w_out, ln_g, ln_b):
    batch, seq, _ = x.shape
    assert seq % ATT_TQ == 0 and seq // SEL_BLOCK == LANES and (seq - CMP_LEN) // CMP_STRIDE + 1 < N_CMP_PAD
    assert WINDOW == ATT_TK and ATT_TQ % ATT_TK == 0
    m = batch * seq
    tables = _rope_tables(seq, HEAD_DIM) + _rope_tables(seq, DIFF_QK_DIM)
    overlap = jnp.asarray(_overlap_matrix(seq), jnp.bfloat16)

    x2d = x.reshape(m, D_MODEL)
    for l in range(DEPTH):
        w_perm = _permute_columns(w_in[l], PROJ_COLS).astype(jnp.bfloat16)
        wt_perm = _permute_columns(w_in[l], VT_COLS).astype(jnp.bfloat16).T
        bias_row = jnp.zeros((1, LANES), jnp.float32).at[0, :FOX_HEADS].set(b_fox_f[l])
        p = _project(x2d, w_perm, wt_perm, bias_row, tables, seq)

        fox_qcat, fox_kcat = _fox_prep(p, batch, seq)
        o_fox = _fox_attn(p, fox_qcat, fox_kcat, batch, seq)

        kc = _compress(p['nsa_k_cmp'], cmp_pos_k[l], cmp_w1_k[l], cmp_w2_k[l], batch, seq, False)
        vc = _compress(p['nsa_v_cmp'], cmp_pos_v[l], cmp_w1_v[l], cmp_w2_v[l], batch, seq, True)
        o_cmp, negsel = _cmp_topk(p['nsa_q'], kc, vc, overlap, batch, seq)
        o_nsa = _nsa_attn(p, negsel, o_cmp, batch, seq)

        lam_init = 0.8 - 0.6 * math.exp(-0.3 * l)
        lam_rows = jnp.zeros((8, LANES), jnp.float32)
        for r, v in enumerate((lam_q1[l], lam_k1[l], lam_q2[l], lam_k2[l])):
            lam_rows = lam_rows.at[r, :DIFF_QK_DIM].set(v.astype(jnp.float32))
        g_row = jnp.tile(diff_subln_g[l].reshape(1, HEAD_DIM), (1, 2))
        o_diff = _diff_attn(p, lam_rows, g_row, lam_init, batch, seq)

        x2d = _out_proj(o_fox, o_nsa, o_diff, w_out[l].astype(jnp.bfloat16), x2d,
                        ln_g[l].reshape(1, D_MODEL), ln_b[l].reshape(1, D_MODEL))
    return x2d.reshape(batch, seq, D_MODEL)
```

```python
import math
from functools import partial

import numpy as np
import jax
import jax.numpy as jnp
from jax import lax
from jax.experimental import pallas as pl
from jax.experimental.pallas import tpu as pltpu

D_MODEL = 1024
DEPTH = 2
HEAD_DIM = 64
FOX_HEADS = 4
NSA_HEADS = 8
DIFF_HEADS = 4
FOX_W = FOX_HEADS * HEAD_DIM
NSA_W = NSA_HEADS * HEAD_DIM
DIFF_QK_DIM = HEAD_DIM // 2
DIFF_W = DIFF_HEADS * HEAD_DIM
NSA_KV_GROUPS = 2
NSA_HPG = NSA_HEADS // NSA_KV_GROUPS
NSA_KV_W = NSA_KV_GROUPS * HEAD_DIM
CMP_LEN = 32
CMP_STRIDE = 16
CMP_HIDDEN = 2 * HEAD_DIM
SEL_BLOCK = 64
SEL_SHIFT = SEL_BLOCK.bit_length() - 1
SEL_TOPK = 16
WINDOW = 512
ROPE_THETA = 10000.0
LN_EPS = 1e-5
RMS_EPS = 1e-5
NEG_INF = -1e30
FORCE_SCORE = 1e30
DEEPNORM_ALPHA = (2 * DEPTH) ** 0.25
LOG2E = math.log2(math.e)

SPLITS = (
    ('fox_q', FOX_W), ('fox_k', FOX_W), ('fox_v', FOX_W), ('fox_f', FOX_HEADS), ('fox_z', FOX_W),
    ('nsa_q', NSA_W),
    ('nsa_k_cmp', NSA_KV_W), ('nsa_v_cmp', NSA_KV_W),
    ('nsa_k_sel', NSA_KV_W), ('nsa_v_sel', NSA_KV_W),
    ('nsa_k_win', NSA_KV_W), ('nsa_v_win', NSA_KV_W),
    ('nsa_gate', 3 * NSA_HEADS), ('nsa_z', NSA_W),
    ('diff_q', DIFF_W), ('diff_k', DIFF_W), ('diff_v', DIFF_W), ('diff_z', DIFF_W),
)
_OFF = {}
_acc = 0
for _n, _w in SPLITS:
    _OFF[_n] = _acc
    _acc += _w
IN_WIDTH = _acc

LANES = 128
VMEM_LIMIT = 56 * 1024 * 1024

PROJ_TM = 512
OUT_TM = 512
CUM_T = 512
CMP_TQ = 512
ATT_TQ = 1024
ATT_TK = 512
N_CMP_PAD = 512
FOX_CAT = 2 * LANES


def _cparams(sem):
    return pltpu.CompilerParams(dimension_semantics=sem, vmem_limit_bytes=VMEM_LIMIT)


def _dot(a, b):
    return jnp.dot(a, b, preferred_element_type=jnp.float32)


def _dot_t(a, b):
    return lax.dot_general(a, b, (((1,), (1,)), ((), ())), preferred_element_type=jnp.float32)


def _split3(x):
    hi = x.astype(jnp.bfloat16)
    r1 = x - hi.astype(jnp.float32)
    mid = r1.astype(jnp.bfloat16)
    lo = (r1 - mid.astype(jnp.float32)).astype(jnp.bfloat16)
    return hi, mid, lo


def _dup(name):
    base = _OFF[name]
    cols = []
    for g in range(NSA_KV_GROUPS):
        one = list(range(base + g * HEAD_DIM, base + (g + 1) * HEAD_DIM))
        cols += one + one
    return cols


def _pad(cols, width):
    return list(cols) + [-1] * (width - len(cols))


def _rng(name, width):
    return list(range(_OFF[name], _OFF[name] + width))


Q_SCALE = HEAD_DIM ** -0.5 * LOG2E
DIFF_Q_SCALE = DIFF_QK_DIM ** -0.5 * LOG2E

PROJ_SEGS = (
    ('fox_q', _rng('fox_q', FOX_W), 'none', jnp.bfloat16, Q_SCALE),
    ('fox_k', _rng('fox_k', FOX_W), 'none', jnp.bfloat16, None),
    ('fox_z', _rng('fox_z', FOX_W), 'silu', jnp.float32, None),
    ('fox_logf', _pad(_rng('fox_f', FOX_HEADS), LANES), 'logsig', jnp.float32, None),
    ('nsa_q', _rng('nsa_q', NSA_W), 'rope64', jnp.bfloat16, Q_SCALE),
    ('nsa_k_cmp', _rng('nsa_k_cmp', NSA_KV_W), 'rope64', jnp.float32, None),
    ('nsa_v_cmp', _rng('nsa_v_cmp', NSA_KV_W), 'none', jnp.float32, None),
    ('nsa_k_sel', _dup('nsa_k_sel'), 'rope64_blockhot', jnp.bfloat16, None),
    ('nsa_k_win', _dup('nsa_k_win'), 'rope64', jnp.bfloat16, None),
    ('nsa_gate', _pad(_rng('nsa_gate', 3 * NSA_HEADS), LANES), 'sigmoid', jnp.float32, None),
    ('nsa_z', _rng('nsa_z', NSA_W), 'silu', jnp.float32, None),
    ('diff_q', _rng('diff_q', DIFF_W), 'rope32', jnp.bfloat16, DIFF_Q_SCALE),
    ('diff_k', _rng('diff_k', DIFF_W), 'rope32', jnp.bfloat16, None),
    ('diff_z', _rng('diff_z', DIFF_W), 'silu', jnp.float32, None),
)
PROJ_COLS = np.concatenate([np.asarray(seg[1], np.int32) for seg in PROJ_SEGS])
PROJ_N = int(PROJ_COLS.shape[0])

VT_SEGS = (
    ('fox_vt', _rng('fox_v', FOX_W)),
    ('nsa_v_sel_t', _rng('nsa_v_sel', NSA_KV_W)),
    ('nsa_v_win_t', _rng('nsa_v_win', NSA_KV_W)),
    ('diff_vt', _rng('diff_v', DIFF_W)),
)
VT_COLS = np.concatenate([np.asarray(seg[1], np.int32) for seg in VT_SEGS])
VT_N = int(VT_COLS.shape[0])


def _permute_columns(w, col_ids):
    pieces, start = [], 0
    cols = col_ids.tolist()
    while start < len(cols):
        end = start + 1
        if cols[start] < 0:
            while end < len(cols) and cols[end] < 0:
                end += 1
            pieces.append(jnp.zeros((w.shape[0], end - start), w.dtype))
        else:
            while end < len(cols) and cols[end] == cols[end - 1] + 1:
                end += 1
            pieces.append(w[:, cols[start]:cols[end - 1] + 1])
        start = end
    return jnp.concatenate(pieces, axis=1)


def _rope_chunk(y, cos, sin, half):
    lane = lax.broadcasted_iota(jnp.int32, y.shape, 1)
    first = (lane & (2 * half - 1)) < half
    partner = jnp.where(first, pltpu.roll(y, LANES - half, 1), pltpu.roll(y, half, 1))
    return y * cos + partner * sin


def _out_width(seg):
    return 2 * len(seg[1]) if seg[2] == 'rope64_blockhot' else len(seg[1])


def _proj_kernel(seq, x_ref, w_ref, wt_ref, bias_ref, cos64_ref, sin64_ref, cos32_ref, sin32_ref, *out_refs):
    xb = x_ref[...].astype(jnp.bfloat16)
    start = 0
    for (name, cols), o_ref in zip(VT_SEGS, out_refs[len(PROJ_SEGS):]):
        rows = len(cols)
        yt = _dot_t(wt_ref[start:start + rows, :], xb).astype(o_ref.dtype)
        start += rows
        ones = jnp.ones((HEAD_DIM, yt.shape[1]), o_ref.dtype)
        for h in range(rows // HEAD_DIM):
            o_ref[2 * h * HEAD_DIM:(2 * h + 1) * HEAD_DIM, :] = yt[h * HEAD_DIM:(h + 1) * HEAD_DIM, :]
            o_ref[(2 * h + 1) * HEAD_DIM:(2 * h + 2) * HEAD_DIM, :] = ones
    start = 0
    for (name, cols, epi, dtype, scale), o_ref in zip(PROJ_SEGS, out_refs):
        width = len(cols)
        y = _dot(xb, w_ref[:, start:start + width])
        start += width
        if epi == 'silu':
            y = y * jax.nn.sigmoid(y)
        elif epi == 'sigmoid':
            y = jax.nn.sigmoid(y)
        elif epi == 'logsig':
            y = y + bias_ref[...]
            y = jnp.minimum(y, 0.0) - jnp.log1p(jnp.exp(-jnp.abs(y)))
        elif epi in ('rope64', 'rope32', 'rope64_blockhot'):
            half = DIFF_QK_DIM // 2 if epi == 'rope32' else HEAD_DIM // 2
            cos = cos32_ref[...] if epi == 'rope32' else cos64_ref[...]
            sin = sin32_ref[...] if epi == 'rope32' else sin64_ref[...]
            chunks = [_rope_chunk(y[:, c * LANES:(c + 1) * LANES], cos, sin, half)
                      for c in range(width // LANES)]
            if epi == 'rope64_blockhot':
                tm = y.shape[0]
                pos = (pl.program_id(0) % (seq // tm)) * tm + lax.broadcasted_iota(jnp.int32, (tm, LANES), 0)
                lane = lax.broadcasted_iota(jnp.int32, (tm, LANES), 1)
                hot = jnp.where(lane == jnp.right_shift(pos, SEL_SHIFT), 1.0, 0.0)
                chunks = [piece for chunk in chunks for piece in (chunk, hot)]
            y = chunks[0] if len(chunks) == 1 else jnp.concatenate(chunks, axis=1)
        if scale is not None:
            y = y * scale
        o_ref[...] = y.astype(dtype)


def _rope_tables(seq, group):
    half = group // 2
    inv = ROPE_THETA ** (-(jnp.arange(half, dtype=jnp.float32) * 2.0 / group))
    ang = jnp.arange(seq, dtype=jnp.float32)[:, None] * inv[None, :]
    cos, sin = jnp.cos(ang), jnp.sin(ang)
    reps = LANES // group
    cos_t = jnp.tile(jnp.concatenate([cos, cos], axis=1), (1, reps))
    sin_t = jnp.tile(jnp.concatenate([-sin, sin], axis=1), (1, reps))
    return cos_t, sin_t


def _project(x2d, w_perm, wt_perm, bias_row, tables, seq):
    m = x2d.shape[0]
    nt = seq // PROJ_TM
    tab_spec = pl.BlockSpec((PROJ_TM, LANES), lambda i: (i % nt, 0))
    out_shape = [jax.ShapeDtypeStruct((m, _out_width(seg)), seg[3]) for seg in PROJ_SEGS]
    out_specs = [pl.BlockSpec((PROJ_TM, _out_width(seg)), lambda i: (i, 0)) for seg in PROJ_SEGS]
    out_shape += [jax.ShapeDtypeStruct((m // seq, 2 * len(seg[1]), seq), jnp.bfloat16) for seg in VT_SEGS]
    out_specs += [pl.BlockSpec((None, 2 * len(seg[1]), PROJ_TM), lambda i: (i // nt, 0, i % nt))
                  for seg in VT_SEGS]
    outs = pl.pallas_call(
        partial(_proj_kernel, seq),
        grid=(m // PROJ_TM,),
        in_specs=[pl.BlockSpec((PROJ_TM, D_MODEL), lambda i: (i, 0)),
                  pl.BlockSpec((D_MODEL, PROJ_N), lambda i: (0, 0)),
                  pl.BlockSpec((VT_N, D_MODEL), lambda i: (0, 0)),
                  pl.BlockSpec((1, LANES), lambda i: (0, 0)),
                  tab_spec, tab_spec, tab_spec, tab_spec],
        out_specs=out_specs,
        out_shape=out_shape,
        compiler_params=_cparams(("parallel",)),
        name="in_proj",
    )(x2d, w_perm, wt_perm, bias_row, *tables)
    return {seg[0]: o for seg, o in zip(PROJ_SEGS + VT_SEGS, outs)}


def _fox_prep_kernel(lf_ref, q_ref, k_ref, qcat_ref, kcat_ref, carry_ref):
    @pl.when(pl.program_id(1) == 0)
    def _():
        carry_ref[...] = jnp.zeros_like(carry_ref)

    t = CUM_T
    row = lax.broadcasted_iota(jnp.int32, (t, t), 0)
    col = lax.broadcasted_iota(jnp.int32, (t, t), 1)
    tri = jnp.where(col <= row, 1.0, 0.0).astype(jnp.bfloat16)
    n_terms = 3
    lane1 = lax.broadcasted_iota(jnp.int32, (t, LANES), 1)

    def packed_terms(x):
        terms = [term.astype(jnp.float32) for term in _split3(x)]
        packed = jnp.where(lane1 < FOX_HEADS, terms[0],
                           jnp.where(lane1 < 2 * FOX_HEADS, pltpu.roll(terms[1], FOX_HEADS, 1),
                                     pltpu.roll(terms[2], 2 * FOX_HEADS, 1)))
        return packed.astype(jnp.bfloat16)

    parts = _dot(tri, packed_terms(lf_ref[...]))
    c = (parts + pltpu.roll(parts, LANES - FOX_HEADS, 1) + pltpu.roll(parts, LANES - 2 * FOX_HEADS, 1)
         + carry_ref[0:1, :])
    carry_ref[0:1, :] = c[t - 1:t, :]

    src = lax.broadcasted_iota(jnp.int32, (LANES, 2 * FOX_HEADS * LANES), 0)
    dst = lax.broadcasted_iota(jnp.int32, (LANES, 2 * FOX_HEADS * LANES), 1)
    assert FOX_HEADS == 4
    term, head = jnp.right_shift(src, 2), src & 3
    live = src < n_terms * FOX_HEADS
    place = (jnp.where(live & (dst == head * LANES + term), 1.0, 0.0)
             - jnp.where(live & (dst == (FOX_HEADS + head) * LANES + n_terms + term), 1.0, 0.0))
    placed = _dot(packed_terms(c * LOG2E), place.astype(jnp.bfloat16))
    lane = lax.broadcasted_iota(jnp.int32, (t, FOX_HEADS * LANES), 1) & (LANES - 1)
    qa = jnp.where((lane >= n_terms) & (lane < 2 * n_terms), 1.0,
                   placed[:, :FOX_HEADS * LANES]).astype(jnp.bfloat16)
    ka = jnp.where(lane < n_terms, 1.0, placed[:, FOX_HEADS * LANES:]).astype(jnp.bfloat16)
    q = q_ref[...]
    k = k_ref[...]
    for h in range(FOX_HEADS):
        pair, hh = divmod(h, 2)
        q2 = q[:, pair * LANES:(pair + 1) * LANES]
        base = h * FOX_CAT
        qcat_ref[:, base:base + LANES] = jnp.where(_half_mask(q2.shape, hh), q2, jnp.zeros_like(q2))
        qcat_ref[:, base + LANES:base + FOX_CAT] = qa[:, h * LANES:(h + 1) * LANES]
        kcat_ref[:, base:base + LANES] = k[:, pair * LANES:(pair + 1) * LANES]
        kcat_ref[:, base + LANES:base + FOX_CAT] = ka[:, h * LANES:(h + 1) * LANES]


def _fox_prep(p, batch, seq):
    m = p['fox_q'].shape[0]
    nt = seq // CUM_T
    row = lambda b, j: (b * nt + j, 0)
    cat = jax.ShapeDtypeStruct((m, FOX_HEADS * FOX_CAT), jnp.bfloat16)
    return pl.pallas_call(
        _fox_prep_kernel,
        grid=(batch, nt),
        in_specs=[pl.BlockSpec((CUM_T, LANES), row), pl.BlockSpec((CUM_T, FOX_W), row),
                  pl.BlockSpec((CUM_T, FOX_W), row)],
        out_specs=[pl.BlockSpec((CUM_T, FOX_HEADS * FOX_CAT), row)] * 2,
        out_shape=[cat, cat],
        scratch_shapes=[pltpu.VMEM((8, LANES), jnp.float32)],
        compiler_params=_cparams(("arbitrary", "arbitrary")),
        name="fox_prep",
    )(p['fox_logf'], p['fox_q'], p['fox_k'])


def _compress_kernel(ones_half, ch_ref, pos_ref, w1_ref, w2_ref, o_ref):
    n_chunks = ch_ref.shape[0] // CMP_STRIDE
    ch = jnp.concatenate([ch_ref[pl.ds(i, n_chunks, stride=CMP_STRIDE), :] for i in range(CMP_STRIDE)], axis=1)
    a_in = (ch + pos_ref[0:1, :]).astype(jnp.bfloat16)
    b_in = (ch + pos_ref[1:2, :]).astype(jnp.bfloat16)
    row = lax.broadcasted_iota(jnp.int32, (n_chunks, LANES), 0)
    for g in range(NSA_KV_GROUPS):
        a = _dot(a_in, w1_ref[g, 0])
        b = _dot(b_in, w1_ref[g, 1])
        h = a + pltpu.roll(b, n_chunks - 1, 0)
        act = (h * jax.nn.sigmoid(h)).astype(jnp.bfloat16)
        out = jnp.where(row < n_chunks - 1, _dot(act, w2_ref[...]), 0.0)
        if ones_half:
            out_t = out.T
            row_t = lax.broadcasted_iota(jnp.int32, out_t.shape, 0)
            o_ref[g] = jnp.where(row_t >= HEAD_DIM, 1.0, out_t).astype(o_ref.dtype)
        else:
            o_ref[g] = out.astype(o_ref.dtype)


def _compress(tok, pos_emb, w1, w2, batch, seq, ones_half):
    n_chunks = seq // CMP_STRIDE
    width = CMP_STRIDE * NSA_KV_W
    halves = w1.reshape(2, CMP_STRIDE, HEAD_DIM, CMP_HIDDEN)
    w1s = jnp.zeros((NSA_KV_GROUPS, 2, CMP_STRIDE, NSA_KV_GROUPS, HEAD_DIM, CMP_HIDDEN), w1.dtype)
    for g in range(NSA_KV_GROUPS):
        w1s = w1s.at[g, :, :, g].set(halves)
    w1s = w1s.reshape(NSA_KV_GROUPS, 2, width, CMP_HIDDEN).astype(jnp.bfloat16)
    pos = jnp.broadcast_to(pos_emb.reshape(2, CMP_STRIDE, 1, HEAD_DIM),
                           (2, CMP_STRIDE, NSA_KV_GROUPS, HEAD_DIM)).reshape(2, width)
    w2d = jnp.concatenate([w2, jnp.zeros_like(w2) if ones_half else w2], axis=1).astype(jnp.bfloat16)
    sq = None
    out_dims = (LANES, n_chunks) if ones_half else (n_chunks, LANES)
    return pl.pallas_call(
        partial(_compress_kernel, ones_half),
        grid=(batch,),
        in_specs=[pl.BlockSpec((seq, NSA_KV_W), lambda b: (b, 0)),
                  pl.BlockSpec((2, width), lambda b: (0, 0)),
                  pl.BlockSpec((NSA_KV_GROUPS, 2, width, CMP_HIDDEN), lambda b: (0, 0, 0, 0)),
                  pl.BlockSpec((CMP_HIDDEN, LANES), lambda b: (0, 0))],
        out_specs=pl.BlockSpec((sq, NSA_KV_GROUPS) + out_dims, lambda b: (b, 0, 0, 0)),
        out_shape=jax.ShapeDtypeStruct((batch, NSA_KV_GROUPS) + out_dims, jnp.bfloat16),
        compiler_params=_cparams(("parallel",)),
        name="nsa_compress_v" if ones_half else "nsa_compress_k",
    )(tok, pos, w1s, w2d)


def _half_mask(shape, hh):
    lane = lax.broadcasted_iota(jnp.int32, shape, 1)
    return (lane >= hh * HEAD_DIM) & (lane < (hh + 1) * HEAD_DIM)


def _cmp_topk_kernel(q_ref, kc_ref, vct_ref, ovt_ref, tri_ref, oc_ref, negsel_ref, imp_ref):
    tq = q_ref.shape[0]
    step = pl.program_id(2)
    t = step * tq + lax.broadcasted_iota(jnp.int32, (1, tq), 1)
    any_visible = jnp.where(t >= CMP_LEN - 1, 1.0, 0.0)

    def attend(rows):
        n = lax.broadcasted_iota(jnp.int32, (rows, 1), 0)
        mask_c = (CMP_STRIDE * n + (CMP_LEN - 1)) <= t
        kc = kc_ref[0:rows, :]
        vct = vct_ref[:, 0:rows]
        psum = jnp.zeros((rows, tq), jnp.float32)
        for j in range(NSA_HPG // 2):
            q2 = q_ref[:, j * LANES:(j + 1) * LANES]
            o_pair = []
            for hh in range(2):
                qm = jnp.where(_half_mask(q2.shape, hh), q2, jnp.zeros_like(q2))
                st = jnp.where(mask_c, _dot_t(kc, qm), NEG_INF)
                e = jnp.exp2(st - _key_max(st))
                acc = _dot(vct, e.astype(jnp.bfloat16))
                inv = any_visible / acc[HEAD_DIM:HEAD_DIM + 1, :]
                psum = psum + e * inv
                o_pair.append(acc[:HEAD_DIM, :] * inv)
            oc_ref[:, j * LANES:(j + 1) * LANES] = jnp.concatenate(o_pair, axis=0).T
        p_hi = psum.astype(jnp.bfloat16)
        p_lo = (psum - p_hi.astype(jnp.float32)).astype(jnp.bfloat16)
        ovt = ovt_ref[:, 0:rows]
        imp_ref[...] = _dot(ovt, p_hi) + _dot(ovt, p_lo)

    steps_per_variant = (LANES * CMP_STRIDE) // tq
    for variant in range(N_CMP_PAD // LANES):
        @pl.when(step // steps_per_variant == variant)
        def _():
            attend((variant + 1) * LANES)

    blk = lax.broadcasted_iota(jnp.int32, (LANES, 1), 0)
    cur = jnp.right_shift(t, SEL_SHIFT)
    valid = blk * SEL_BLOCK <= t
    forced = (blk == 0) | (blk == cur) | (blk == cur - 1)
    score = jnp.where(valid, jnp.where(forced, FORCE_SCORE, imp_ref[...]), NEG_INF)
    tri = tri_ref[...]
    budget = jnp.full((1, tq), float(SEL_TOPK), jnp.float32)
    sel = jnp.zeros((LANES, tq), jnp.float32)
    for _ in range(SEL_TOPK):
        level = score == _key_max(score)
        score = jnp.where(level, -jnp.inf, score)
        rank = _dot(tri, jnp.where(level, 1.0, 0.0).astype(jnp.bfloat16))
        sel = jnp.where(level & (rank <= budget), 1.0, sel)
        budget = jnp.maximum(budget - rank[LANES - 1:LANES, :], 0.0)
    negsel_ref[...] = jnp.where(sel > 0.5, 0.0, NEG_INF).T.astype(negsel_ref.dtype)


def _cmp_topk(nsa_q, kc, vc, overlap, batch, seq):
    m = nsa_q.shape[0]
    nq = seq // CMP_TQ
    sq = None
    idx = np.arange(LANES)
    tri = jnp.asarray((idx[None, :] <= idx[:, None]).astype(np.float32), jnp.bfloat16)
    return pl.pallas_call(
        _cmp_topk_kernel,
        grid=(batch, NSA_KV_GROUPS, nq),
        in_specs=[pl.BlockSpec((CMP_TQ, NSA_HPG * HEAD_DIM), lambda b, g, i: (b * nq + i, g)),
                  pl.BlockSpec((sq, sq, N_CMP_PAD, LANES), lambda b, g, i: (b, g, 0, 0)),
                  pl.BlockSpec((sq, sq, LANES, N_CMP_PAD), lambda b, g, i: (b, g, 0, 0)),
                  pl.BlockSpec((LANES, N_CMP_PAD), lambda b, g, i: (0, 0)),
                  pl.BlockSpec((LANES, LANES), lambda b, g, i: (0, 0))],
        out_specs=[pl.BlockSpec((CMP_TQ, NSA_HPG * HEAD_DIM), lambda b, g, i: (b * nq + i, g)),
                   pl.BlockSpec((sq, sq, CMP_TQ, LANES), lambda b, g, i: (b, g, i, 0))],
        out_shape=[jax.ShapeDtypeStruct((m, NSA_W), jnp.float32),
                   jax.ShapeDtypeStruct((batch, NSA_KV_GROUPS, seq, LANES), jnp.bfloat16)],
        scratch_shapes=[pltpu.VMEM((LANES, CMP_TQ), jnp.float32)],
        compiler_params=_cparams(("parallel", "parallel", "parallel")),
        name="nsa_cmp_topk",
    )(nsa_q, kc, vc, overlap.T, tri)


def _init_state(m_ref, acc_ref):
    m_ref[...] = jnp.full(m_ref.shape, NEG_INF, jnp.float32)
    acc_ref[...] = jnp.zeros(acc_ref.shape, jnp.float32)


def _key_max(st):
    rows = st.shape[0]
    while rows > 8:
        rows //= 2
        st = jnp.maximum(st[:rows], st[rows:])
    return jnp.max(st, axis=0, keepdims=True)


def _online_step_t(st, vt, m_ref, acc_ref, idx, off=None, lo=None):
    if off is not None:
        st = jnp.where(_key_minus_query(*st.shape) <= off, st, NEG_INF)
    if lo is not None:
        st = jnp.where(_key_minus_query(*st.shape) >= lo, st, NEG_INF)
    m_old = m_ref[idx]
    m_new = jnp.maximum(m_old, _key_max(st))
    alpha = jnp.exp2(m_old - m_new)
    pt = jnp.exp2(st - m_new).astype(jnp.bfloat16)
    acc_ref[idx] = alpha * acc_ref[idx] + _dot(vt, pt)
    m_ref[idx] = m_new


def _normalized_t(acc):
    return acc[:HEAD_DIM, :] / acc[HEAD_DIM:, :]


def _key_minus_query(keys, queries):
    return (lax.broadcasted_iota(jnp.int32, (keys, queries), 0)
            - lax.broadcasted_iota(jnp.int32, (keys, queries), 1))


def _pipelined_sweep(qi, parts, qk, softmax_pv, s0_ref, s1_ref, before_tail=None):
    assert ATT_TQ == 2 * ATT_TK
    late = slice(ATT_TK, ATT_TQ)

    def step(kt_next, s_next, kt, s_cur, off, cols=None):
        for part in range(parts):
            qk(kt_next, s_next, part, cols)
            softmax_pv(kt, s_cur, off, part, None)

    for part in range(parts):
        qk(0, s0_ref, part, None)

    def pair(kt):
        step(kt + 1, s1_ref, kt, s0_ref, None)
        step(kt + 2, s0_ref, kt + 1, s1_ref, None)

    def two_pairs(i, carry):
        pair(4 * i)
        pair(4 * i + 2)
        return carry

    def one_pair(i, carry):
        pair(2 * i)
        return carry

    lax.fori_loop(0, qi // 2, two_pairs, 0)
    lax.fori_loop(2 * (qi // 2), qi, one_pair, 0)
    kt = 2 * qi
    for part in range(parts):
        qk(kt + 1, s1_ref, part, late)
    if before_tail is not None:
        before_tail()
    for part in range(parts):
        softmax_pv(kt, s0_ref, 0, part, None)
    for part in range(parts):
        softmax_pv(kt + 1, s1_ref, 0, part, late)


def _at(part, cols):
    return part if cols is None else (part, slice(None), cols)


def _pair_rows(acc_ref):
    return jnp.concatenate([_normalized_t(acc_ref[hh]) for hh in range(2)], axis=0).T


def _pick_col(x, col):
    lane = lax.broadcasted_iota(jnp.int32, x.shape, 1)
    return jnp.sum(jnp.where(lane == col, x, 0.0), axis=1, keepdims=True)


def _nsa_attn_kernel(q_ref, ksel_ref, vselt_ref, kwin_ref, vwint_ref, negsel_ref, oc_ref, gate_ref, z_ref,
                     o_ref, qcat_ref, s0_ref, s1_ref, w_ref, m_ref, acc_ref, mw_ref, accw_ref):
    tq, tk = ATT_TQ, ATT_TK
    g = pl.program_id(1)
    j = pl.program_id(2)
    qi = pl.program_id(3)
    q2 = q_ref[...]
    negsel = negsel_ref[...]
    for hh in range(2):
        qm = jnp.where(_half_mask(q2.shape, hh), q2, jnp.zeros_like(q2))
        qcat_ref[hh] = jnp.concatenate([qm, negsel], axis=1)
    _init_state(m_ref, acc_ref)

    def qk(kt, s_ref, hh, cols):
        start = pl.multiple_of(kt * tk, tk)
        q = qcat_ref[hh] if cols is None else qcat_ref[hh, cols, :]
        s_ref[_at(hh, cols)] = _dot_t(ksel_ref[pl.ds(start, tk), :], q)

    def softmax_pv(kt, s_ref, off, hh, cols):
        start = pl.multiple_of(kt * tk, tk)
        _online_step_t(s_ref[_at(hh, cols)], vselt_ref[:, pl.ds(start, tk)], m_ref, acc_ref, _at(hh, cols), off)

    halves = tq // tk

    def win_tiles(half):
        wt = qi * halves + half
        return (pl.multiple_of(wt * tk, tk), pl.multiple_of(jnp.maximum(wt - 1, 0) * tk, tk),
                jnp.where(wt > 0, 1, tk))

    def window_qk():
        for half in range(halves):
            cols = slice(half * tk, (half + 1) * tk)
            d0, p0, _ = win_tiles(half)
            for hh in range(2):
                qm = qcat_ref[hh, cols, :LANES]
                w_ref[hh, 0, :, cols] = _dot_t(kwin_ref[pl.ds(d0, tk), :], qm)
                w_ref[hh, 1, :, cols] = _dot_t(kwin_ref[pl.ds(p0, tk), :], qm)

    _pipelined_sweep(qi, 2, qk, softmax_pv, s0_ref, s1_ref, before_tail=window_qk)
    _init_state(mw_ref, accw_ref)
    for half in range(halves):
        cols = slice(half * tk, (half + 1) * tk)
        d0, p0, prev_lo = win_tiles(half)
        for hh in range(2):
            _online_step_t(w_ref[hh, 0, :, cols], vwint_ref[:, pl.ds(d0, tk)], mw_ref, accw_ref, _at(hh, cols),
                           off=0)
            _online_step_t(w_ref[hh, 1, :, cols], vwint_ref[:, pl.ds(p0, tk)], mw_ref, accw_ref, _at(hh, cols),
                           lo=prev_lo)
    o_sel = _pair_rows(acc_ref)
    o_win = _pair_rows(accw_ref)

    gates = gate_ref[...]
    head0 = (g * NSA_HPG + j * 2) * 3
    low = lax.broadcasted_iota(jnp.int32, (tq, LANES), 1) < HEAD_DIM
    mix = [jnp.where(low, _pick_col(gates, head0 + i), _pick_col(gates, head0 + 3 + i)) for i in range(3)]
    out = mix[0] * oc_ref[...] + mix[1] * o_sel + mix[2] * o_win
    o_ref[...] = (out * z_ref[...]).astype(o_ref.dtype)


def _nsa_attn(p, negsel, o_cmp, batch, seq):
    m = p['nsa_q'].shape[0]
    t = ATT_TQ
    nq = seq // t
    sq = None
    row_blk = lambda b, g, j, i: (b * nq + i, g * 2 + j)
    k_spec = pl.BlockSpec((seq, LANES), lambda b, g, j, i: (b, g))
    kcat_spec = pl.BlockSpec((seq, 2 * LANES), lambda b, g, j, i: (b, g))
    vt_spec = pl.BlockSpec((sq, LANES, seq), lambda b, g, j, i: (b, g, 0))
    return pl.pallas_call(
        _nsa_attn_kernel,
        grid=(batch, NSA_KV_GROUPS, NSA_HPG // 2, nq),
        in_specs=[pl.BlockSpec((t, LANES), row_blk),
                  kcat_spec, vt_spec, k_spec, vt_spec,
                  pl.BlockSpec((sq, sq, t, LANES), lambda b, g, j, i: (b, g, i, 0)),
                  pl.BlockSpec((t, LANES), row_blk),
                  pl.BlockSpec((t, LANES), lambda b, g, j, i: (b * nq + i, 0)),
                  pl.BlockSpec((t, LANES), row_blk)],
        out_specs=pl.BlockSpec((t, LANES), row_blk),
        out_shape=jax.ShapeDtypeStruct((m, NSA_W), jnp.bfloat16),
        scratch_shapes=[pltpu.VMEM((2, t, 2 * LANES), jnp.bfloat16),
                        pltpu.VMEM((2, ATT_TK, t), jnp.float32),
                        pltpu.VMEM((2, ATT_TK, t), jnp.float32),
                        pltpu.VMEM((2, 2, ATT_TK, t), jnp.float32),
                        pltpu.VMEM((2, 1, t), jnp.float32),
                        pltpu.VMEM((2, LANES, t), jnp.float32),
                        pltpu.VMEM((2, 1, t), jnp.float32),
                        pltpu.VMEM((2, LANES, t), jnp.float32)],
        compiler_params=_cparams(("parallel", "parallel", "parallel", "arbitrary")),
        name="nsa_sel_win_attn",
    )(p['nsa_q'], p['nsa_k_sel'], p['nsa_v_sel_t'], p['nsa_k_win'], p['nsa_v_win_t'], negsel, o_cmp,
      p['nsa_gate'], p['nsa_z'])


def _fox_kernel(q_ref, k_ref, vt_ref, z_ref, o_ref, s0_ref, s1_ref, m_ref, acc_ref):
    tk = ATT_TK
    qi = pl.program_id(2)
    _init_state(m_ref, acc_ref)

    def qk(kt, s_ref, hh, cols):
        start = pl.multiple_of(kt * tk, tk)
        cat = slice(hh * FOX_CAT, (hh + 1) * FOX_CAT)
        q = q_ref[:, cat] if cols is None else q_ref[cols, cat]
        s_ref[_at(hh, cols)] = _dot_t(k_ref[pl.ds(start, tk), cat], q)

    def softmax_pv(kt, s_ref, off, hh, cols):
        start = pl.multiple_of(kt * tk, tk)
        _online_step_t(s_ref[_at(hh, cols)], vt_ref[hh * LANES:(hh + 1) * LANES, pl.ds(start, tk)],
                       m_ref, acc_ref, _at(hh, cols), off)

    _pipelined_sweep(qi, 2, qk, softmax_pv, s0_ref, s1_ref)
    o_ref[...] = (_pair_rows(acc_ref) * z_ref[...]).astype(o_ref.dtype)


def _fox_attn(p, qcat, kcat, batch, seq):
    m = qcat.shape[0]
    t = ATT_TQ
    nq = seq // t
    row_blk = lambda b, j, i: (b * nq + i, j)
    return pl.pallas_call(
        _fox_kernel,
        grid=(batch, FOX_HEADS // 2, nq),
        in_specs=[pl.BlockSpec((t, 2 * FOX_CAT), row_blk),
                  pl.BlockSpec((seq, 2 * FOX_CAT), lambda b, j, i: (b, j)),
                  pl.BlockSpec((None, 2 * LANES, seq), lambda b, j, i: (b, j, 0)),
                  pl.BlockSpec((t, LANES), row_blk)],
        out_specs=pl.BlockSpec((t, LANES), row_blk),
        out_shape=jax.ShapeDtypeStruct((m, FOX_W), jnp.bfloat16),
        scratch_shapes=[pltpu.VMEM((2, ATT_TK, t), jnp.float32),
                        pltpu.VMEM((2, ATT_TK, t), jnp.float32),
                        pltpu.VMEM((2, 1, t), jnp.float32),
                        pltpu.VMEM((2, LANES, t), jnp.float32)],
        compiler_params=_cparams(("parallel", "parallel", "arbitrary")),
        name="fox_attn",
    )(qcat, kcat, p['fox_vt'], p['fox_z'])


def _diff_kernel(lam_init, q_ref, k_ref, vt_ref, z_ref, lam_ref, g_ref, o_ref, qm_ref, s0_ref, s1_ref,
                 m_ref, acc_ref):
    tq, tk = ATT_TQ, ATT_TK
    qi = pl.program_id(2)
    q2 = q_ref[...]
    lane = lax.broadcasted_iota(jnp.int32, (tq, LANES), 1)
    for idx in range(4):
        lo = idx * DIFF_QK_DIM
        qm_ref[idx] = jnp.where((lane >= lo) & (lane < lo + DIFF_QK_DIM), q2, jnp.zeros_like(q2))
    _init_state(m_ref, acc_ref)

    def qk(kt, s_ref, idx, cols):
        start = pl.multiple_of(kt * tk, tk)
        q = qm_ref[idx] if cols is None else qm_ref[idx, cols, :]
        s_ref[_at(idx, cols)] = _dot_t(k_ref[pl.ds(start, tk), :], q)

    def softmax_pv(kt, s_ref, off, idx, cols):
        start = pl.multiple_of(kt * tk, tk)
        hh = idx // 2
        _online_step_t(s_ref[_at(idx, cols)], vt_ref[hh * LANES:(hh + 1) * LANES, pl.ds(start, tk)],
                       m_ref, acc_ref, _at(idx, cols), off)

    _pipelined_sweep(qi, 4, qk, softmax_pv, s0_ref, s1_ref)

    lam_rows = lam_ref[...]
    lam = (jnp.exp(jnp.sum(lam_rows[0:1, :] * lam_rows[1:2, :], axis=1, keepdims=True))
           - jnp.exp(jnp.sum(lam_rows[2:3, :] * lam_rows[3:4, :], axis=1, keepdims=True)) + lam_init)
    o = [_normalized_t(acc_ref[idx]) for idx in range(4)]
    normed = []
    for hh in range(2):
        d = o[2 * hh] - lam * o[2 * hh + 1]
        ms = jnp.sum(d * d, axis=0, keepdims=True) / HEAD_DIM
        normed.append(d * lax.rsqrt(ms + RMS_EPS))
    out = jnp.concatenate(normed, axis=0).T * g_ref[...] * (1.0 - lam_init)
    o_ref[...] = (out * z_ref[...]).astype(o_ref.dtype)


def _diff_attn(p, lam_rows, g_row, lam_init, batch, seq):
    m = p['diff_q'].shape[0]
    t = ATT_TQ
    nq = seq // t
    row_blk = lambda b, j, i: (b * nq + i, j)
    return pl.pallas_call(
        partial(_diff_kernel, lam_init),
        grid=(batch, DIFF_HEADS // 2, nq),
        in_specs=[pl.BlockSpec((t, LANES), row_blk),
                  pl.BlockSpec((seq, LANES), lambda b, j, i: (b, j)),
                  pl.BlockSpec((None, 2 * LANES, seq), lambda b, j, i: (b, j, 0)),
                  pl.BlockSpec((t, LANES), row_blk),
                  pl.BlockSpec((8, LANES), lambda b, j, i: (0, 0)),
                  pl.BlockSpec((1, LANES), lambda b, j, i: (0, 0))],
        out_specs=pl.BlockSpec((t, LANES), row_blk),
        out_shape=jax.ShapeDtypeStruct((m, DIFF_W), jnp.bfloat16),
        scratch_shapes=[pltpu.VMEM((4, t, LANES), jnp.bfloat16),
                        pltpu.VMEM((4, ATT_TK, t), jnp.float32),
                        pltpu.VMEM((4, ATT_TK, t), jnp.float32),
                        pltpu.VMEM((4, 1, t), jnp.float32),
                        pltpu.VMEM((4, LANES, t), jnp.float32)],
        compiler_params=_cparams(("parallel", "parallel", "arbitrary")),
        name="diff_attn",
    )(p['diff_q'], p['diff_k'], p['diff_vt'], p['diff_z'], lam_rows, g_row)


def _out_kernel(of_ref, on_ref, od_ref, w_ref, x_ref, g_ref, b_ref, o_ref):
    y = (DEEPNORM_ALPHA * x_ref[...]
         + _dot(of_ref[...], w_ref[0:FOX_W, :])
         + _dot(on_ref[...], w_ref[FOX_W:FOX_W + NSA_W, :])
         + _dot(od_ref[...], w_ref[FOX_W + NSA_W:, :]))
    mu = jnp.mean(y, axis=1, keepdims=True)
    yc = y - mu
    var = jnp.mean(yc * yc, axis=1, keepdims=True)
    o_ref[...] = yc * lax.rsqrt(var + LN_EPS) * g_ref[...] + b_ref[...]


def _out_proj(o_fox, o_nsa, o_diff, w_out, x2d, g_row, b_row):
    m = x2d.shape[0]
    tm = OUT_TM
    row = lambda i: (i, 0)
    const = lambda i: (0, 0)
    return pl.pallas_call(
        _out_kernel,
        grid=(m // tm,),
        in_specs=[pl.BlockSpec((tm, FOX_W), row), pl.BlockSpec((tm, NSA_W), row),
                  pl.BlockSpec((tm, DIFF_W), row),
                  pl.BlockSpec((D_MODEL, D_MODEL), const),
                  pl.BlockSpec((tm, D_MODEL), row),
                  pl.BlockSpec((1, D_MODEL), const), pl.BlockSpec((1, D_MODEL), const)],
        out_specs=pl.BlockSpec((tm, D_MODEL), row),
        out_shape=jax.ShapeDtypeStruct((m, D_MODEL), jnp.float32),
        compiler_params=_cparams(("parallel",)),
        name="out_proj_ln",
    )(o_fox, o_nsa, o_diff, w_out, x2d, g_row, b_row)


def _overlap_matrix(seq):
    n_slc = seq // SEL_BLOCK
    cmp_start = CMP_STRIDE * np.arange(N_CMP_PAD)
    sel_start = SEL_BLOCK * np.arange(n_slc)
    ov = np.clip(np.minimum(cmp_start[:, None] + CMP_LEN, sel_start[None, :] + SEL_BLOCK)
                 - np.maximum(cmp_start[:, None], sel_start[None, :]), 0, None).astype(np.float32) / CMP_LEN
    return ov


def kernel(x, w_in, b_fox_f, cmp_pos_k, cmp_pos_v, cmp_w1_k, cmp_w2_k, cmp_w1_v, cmp_w2_v,
           lam_q1, lam_k1, lam_q2, lam_k2, diff_subln_g, w_out, ln_g, ln_b):
    batch, seq, _ = x.shape
    assert seq % ATT_TQ == 0 and seq // SEL_BLOCK == LANES and (seq - CMP_LEN) // CMP_STRIDE + 1 < N_CMP_PAD
    assert WINDOW == ATT_TK and ATT_TQ % ATT_TK == 0
    m = batch * seq
    tables = _rope_tables(seq, HEAD_DIM) + _rope_tables(seq, DIFF_QK_DIM)
    overlap = jnp.asarray(_overlap_matrix(seq), jnp.bfloat16)

    x2d = x.reshape(m, D_MODEL)
    for l in range(DEPTH):
        w_perm = _permute_columns(w_in[l], PROJ_COLS).astype(jnp.bfloat16)
        wt_perm = _permute_columns(w_in[l], VT_COLS).astype(jnp.bfloat16).T
        bias_row = jnp.zeros((1, LANES), jnp.float32).at[0, :FOX_HEADS].set(b_fox_f[l])
        p = _project(x2d, w_perm, wt_perm, bias_row, tables, seq)

        fox_qcat, fox_kcat = _fox_prep(p, batch, seq)
        o_fox = _fox_attn(p, fox_qcat, fox_kcat, batch, seq)

        kc = _compress(p['nsa_k_cmp'], cmp_pos_k[l], cmp_w1_k[l], cmp_w2_k[l], batch, seq, False)
        vc = _compress(p['nsa_v_cmp'], cmp_pos_v[l], cmp_w1_v[l], cmp_w2_v[l], batch, seq, True)
        o_cmp, negsel = _cmp_topk(p['nsa_q'], kc, vc, overlap, batch, seq)
        o_nsa = _nsa_attn(p, negsel, o_cmp, batch, seq)

        lam_init = 0.8 - 0.6 * math.exp(-0.3 * l)
        lam_rows = jnp.zeros((8, LANES), jnp.float32)
        for r, v in enumerate((lam_q1[l], lam_k1[l], lam_q2[l], lam_k2[l])):
            lam_rows = lam_rows.at[r, :DIFF_QK_DIM].set(v.astype(jnp.float32))
        g_row = jnp.tile(diff_subln_g[l].reshape(1, HEAD_DIM), (1, 2))
        o_diff = _diff_attn(p, lam_rows, g_row, lam_init, batch, seq)

        x2d = _out_proj(o_fox, o_nsa, o_diff, w_out[l].astype(jnp.bfloat16), x2d,
                        ln_g[l].reshape(1, D_MODEL), ln_b[l].reshape(1, D_MODEL))
    return x2d.reshape(batch, seq, D_MODEL)
```

```python
import math
from functools import partial

import numpy as np
import jax
import jax.numpy as jnp
from jax import lax
from jax.experimental import pallas as pl
from jax.experimental.pallas import tpu as pltpu

D_MODEL = 1024
DEPTH = 2
HEAD_DIM = 64
FOX_HEADS = 4
NSA_HEADS = 8
DIFF_HEADS = 4
FOX_W = FOX_HEADS * HEAD_DIM
NSA_W = NSA_HEADS * HEAD_DIM
DIFF_QK_DIM = HEAD_DIM // 2
DIFF_W = DIFF_HEADS * HEAD_DIM
NSA_KV_GROUPS = 2
NSA_HPG = NSA_HEADS // NSA_KV_GROUPS
NSA_KV_W = NSA_KV_GROUPS * HEAD_DIM
CMP_LEN = 32
CMP_STRIDE = 16
CMP_HIDDEN = 2 * HEAD_DIM
SEL_BLOCK = 64
SEL_SHIFT = SEL_BLOCK.bit_length() - 1
SEL_TOPK = 16
WINDOW = 512
ROPE_THETA = 10000.0
LN_EPS = 1e-5
RMS_EPS = 1e-5
NEG_INF = -1e30
FORCE_SCORE = 1e30
DEEPNORM_ALPHA = (2 * DEPTH) ** 0.25
LOG2E = math.log2(math.e)

SPLITS = (
    ('fox_q', FOX_W), ('fox_k', FOX_W), ('fox_v', FOX_W), ('fox_f', FOX_HEADS), ('fox_z', FOX_W),
    ('nsa_q', NSA_W),
    ('nsa_k_cmp', NSA_KV_W), ('nsa_v_cmp', NSA_KV_W),
    ('nsa_k_sel', NSA_KV_W), ('nsa_v_sel', NSA_KV_W),
    ('nsa_k_win', NSA_KV_W), ('nsa_v_win', NSA_KV_W),
    ('nsa_gate', 3 * NSA_HEADS), ('nsa_z', NSA_W),
    ('diff_q', DIFF_W), ('diff_k', DIFF_W), ('diff_v', DIFF_W), ('diff_z', DIFF_W),
)
_OFF = {}
_acc = 0
for _n, _w in SPLITS:
    _OFF[_n] = _acc
    _acc += _w
IN_WIDTH = _acc

LANES = 128
VMEM_LIMIT = 56 * 1024 * 1024

PROJ_TM = 512
OUT_TM = 1024
CMP_TQ = 512
ATT_TQ = 1024
ATT_TK = 512
N_CMP_PAD = 512
FOX_CAT = 2 * LANES


def _cparams(sem):
    return pltpu.CompilerParams(dimension_semantics=sem, vmem_limit_bytes=VMEM_LIMIT)


def _dot(a, b):
    return jnp.dot(a, b, preferred_element_type=jnp.float32)


def _dot_t(a, b):
    return lax.dot_general(a, b, (((1,), (1,)), ((), ())), preferred_element_type=jnp.float32)


def _split3(x):
    hi = x.astype(jnp.bfloat16)
    r1 = x - hi.astype(jnp.float32)
    mid = r1.astype(jnp.bfloat16)
    lo = (r1 - mid.astype(jnp.float32)).astype(jnp.bfloat16)
    return hi, mid, lo


def _dup(name):
    base = _OFF[name]
    cols = []
    for g in range(NSA_KV_GROUPS):
        one = list(range(base + g * HEAD_DIM, base + (g + 1) * HEAD_DIM))
        cols += one + one
    return cols


def _pad(cols, width):
    return list(cols) + [-1] * (width - len(cols))


def _rng(name, width):
    return list(range(_OFF[name], _OFF[name] + width))


Q_SCALE = HEAD_DIM ** -0.5 * LOG2E
DIFF_Q_SCALE = DIFF_QK_DIM ** -0.5 * LOG2E

PROJ_SEGS = (
    ('fox_q', _rng('fox_q', FOX_W), 'none', jnp.bfloat16, Q_SCALE),
    ('fox_k', _rng('fox_k', FOX_W), 'none', jnp.bfloat16, None),
    ('fox_z', _rng('fox_z', FOX_W), 'silu', jnp.float32, None),
    ('fox_logf', _pad(_rng('fox_f', FOX_HEADS), LANES), 'logsig', jnp.float32, None),
    ('nsa_q', _rng('nsa_q', NSA_W), 'rope64', jnp.bfloat16, Q_SCALE),
    ('nsa_k_cmp', _rng('nsa_k_cmp', NSA_KV_W), 'rope64', jnp.float32, None),
    ('nsa_v_cmp', _rng('nsa_v_cmp', NSA_KV_W), 'none', jnp.float32, None),
    ('nsa_k_sel', _dup('nsa_k_sel'), 'rope64_blockhot', jnp.bfloat16, None),
    ('nsa_k_win', _dup('nsa_k_win'), 'rope64', jnp.bfloat16, None),
    ('nsa_gate', _pad(_rng('nsa_gate', 3 * NSA_HEADS), LANES), 'sigmoid', jnp.float32, None),
    ('nsa_z', _rng('nsa_z', NSA_W), 'silu', jnp.float32, None),
    ('diff_q', _rng('diff_q', DIFF_W), 'rope32', jnp.bfloat16, DIFF_Q_SCALE),
    ('diff_k', _rng('diff_k', DIFF_W), 'rope32', jnp.bfloat16, None),
    ('diff_z', _rng('diff_z', DIFF_W), 'silu', jnp.float32, None),
)
PROJ_COLS = np.concatenate([np.asarray(seg[1], np.int32) for seg in PROJ_SEGS])
PROJ_N = int(PROJ_COLS.shape[0])

VT_SEGS = (
    ('fox_vt', _rng('fox_v', FOX_W)),
    ('nsa_v_sel_t', _rng('nsa_v_sel', NSA_KV_W)),
    ('nsa_v_win_t', _rng('nsa_v_win', NSA_KV_W)),
    ('diff_vt', _rng('diff_v', DIFF_W)),
)
VT_COLS = np.concatenate([np.asarray(seg[1], np.int32) for seg in VT_SEGS])
VT_N = int(VT_COLS.shape[0])


def _permute_columns(w, col_ids):
    pieces, start = [], 0
    cols = col_ids.tolist()
    while start < len(cols):
        end = start + 1
        if cols[start] < 0:
            while end < len(cols) and cols[end] < 0:
                end += 1
            pieces.append(jnp.zeros((w.shape[0], end - start), w.dtype))
        else:
            while end < len(cols) and cols[end] == cols[end - 1] + 1:
                end += 1
            pieces.append(w[:, cols[start]:cols[end - 1] + 1])
        start = end
    return jnp.concatenate(pieces, axis=1)


def _rope_chunk(y, cos, sin, half):
    lane = lax.broadcasted_iota(jnp.int32, y.shape, 1)
    first = (lane & (2 * half - 1)) < half
    partner = jnp.where(first, pltpu.roll(y, LANES - half, 1), pltpu.roll(y, half, 1))
    return y * cos + partner * sin


PROJ_HELD = ('fox_q', 'fox_k', 'fox_logf')
PROJ_STORED = tuple(seg for seg in PROJ_SEGS if seg[0] not in PROJ_HELD)


def _out_width(seg):
    return 2 * len(seg[1]) if seg[2] == 'rope64_blockhot' else len(seg[1])


def _proj_kernel(seq, x_ref, w_ref, wt_ref, bias_ref, cos64_ref, sin64_ref, cos32_ref, sin32_ref, *refs):
    stored = dict(zip((seg[0] for seg in PROJ_STORED), refs))
    vt_refs = refs[len(PROJ_STORED):len(PROJ_STORED) + len(VT_SEGS)]
    qcat_ref, kcat_ref, carry_ref = refs[len(PROJ_STORED) + len(VT_SEGS):]
    xb = x_ref[...].astype(jnp.bfloat16)
    held = {}
    start = 0
    for (name, cols), o_ref in zip(VT_SEGS, vt_refs):
        rows = len(cols)
        yt = _dot_t(wt_ref[start:start + rows, :], xb).astype(o_ref.dtype)
        start += rows
        ones = jnp.ones((HEAD_DIM, yt.shape[1]), o_ref.dtype)
        for h in range(rows // HEAD_DIM):
            o_ref[2 * h * HEAD_DIM:(2 * h + 1) * HEAD_DIM, :] = yt[h * HEAD_DIM:(h + 1) * HEAD_DIM, :]
            o_ref[(2 * h + 1) * HEAD_DIM:(2 * h + 2) * HEAD_DIM, :] = ones
    start = 0
    for name, cols, epi, dtype, scale in PROJ_SEGS:
        width = len(cols)
        y = _dot(xb, w_ref[:, start:start + width])
        start += width
        if epi == 'silu':
            y = y * jax.nn.sigmoid(y)
        elif epi == 'sigmoid':
            y = jax.nn.sigmoid(y)
        elif epi == 'logsig':
            y = y + bias_ref[...]
            y = jnp.minimum(y, 0.0) - jnp.log1p(jnp.exp(-jnp.abs(y)))
        elif epi in ('rope64', 'rope32', 'rope64_blockhot'):
            half = DIFF_QK_DIM // 2 if epi == 'rope32' else HEAD_DIM // 2
            cos = cos32_ref[...] if epi == 'rope32' else cos64_ref[...]
            sin = sin32_ref[...] if epi == 'rope32' else sin64_ref[...]
            chunks = [_rope_chunk(y[:, c * LANES:(c + 1) * LANES], cos, sin, half)
                      for c in range(width // LANES)]
            if epi == 'rope64_blockhot':
                tm = y.shape[0]
                pos = (pl.program_id(0) % (seq // tm)) * tm + lax.broadcasted_iota(jnp.int32, (tm, LANES), 0)
                lane = lax.broadcasted_iota(jnp.int32, (tm, LANES), 1)
                hot = jnp.where(lane == jnp.right_shift(pos, SEL_SHIFT), 1.0, 0.0)
                chunks = [piece for chunk in chunks for piece in (chunk, hot)]
            y = chunks[0] if len(chunks) == 1 else jnp.concatenate(chunks, axis=1)
        if scale is not None:
            y = y * scale
        if name in PROJ_HELD:
            held[name] = y.astype(dtype)
        else:
            stored[name][...] = y.astype(dtype)

    @pl.when(pl.program_id(0) % (seq // x_ref.shape[0]) == 0)
    def _():
        carry_ref[...] = jnp.zeros_like(carry_ref)

    _fox_operands(held['fox_logf'], held['fox_q'], held['fox_k'], carry_ref, qcat_ref, kcat_ref)


def _rope_tables(seq, group):
    half = group // 2
    inv = ROPE_THETA ** (-(jnp.arange(half, dtype=jnp.float32) * 2.0 / group))
    ang = jnp.arange(seq, dtype=jnp.float32)[:, None] * inv[None, :]
    cos, sin = jnp.cos(ang), jnp.sin(ang)
    reps = LANES // group
    cos_t = jnp.tile(jnp.concatenate([cos, cos], axis=1), (1, reps))
    sin_t = jnp.tile(jnp.concatenate([-sin, sin], axis=1), (1, reps))
    return cos_t, sin_t


def _project(x2d, w_perm, wt_perm, bias_row, tables, seq):
    m = x2d.shape[0]
    nt = seq // PROJ_TM
    tab_spec = pl.BlockSpec((PROJ_TM, LANES), lambda i: (i % nt, 0))
    out_shape = [jax.ShapeDtypeStruct((m, _out_width(seg)), seg[3]) for seg in PROJ_STORED]
    out_specs = [pl.BlockSpec((PROJ_TM, _out_width(seg)), lambda i: (i, 0)) for seg in PROJ_STORED]
    out_shape += [jax.ShapeDtypeStruct((m // seq, 2 * len(seg[1]), seq), jnp.bfloat16) for seg in VT_SEGS]
    out_specs += [pl.BlockSpec((None, 2 * len(seg[1]), PROJ_TM), lambda i: (i // nt, 0, i % nt))
                  for seg in VT_SEGS]
    out_shape += [jax.ShapeDtypeStruct((m, FOX_HEADS * FOX_CAT), jnp.bfloat16)] * 2
    out_specs += [pl.BlockSpec((PROJ_TM, FOX_HEADS * FOX_CAT), lambda i: (i, 0))] * 2
    names = [seg[0] for seg in PROJ_STORED + VT_SEGS] + ['fox_qcat', 'fox_kcat']
    outs = pl.pallas_call(
        partial(_proj_kernel, seq),
        grid=(m // PROJ_TM,),
        in_specs=[pl.BlockSpec((PROJ_TM, D_MODEL), lambda i: (i, 0)),
                  pl.BlockSpec((D_MODEL, PROJ_N), lambda i: (0, 0)),
                  pl.BlockSpec((VT_N, D_MODEL), lambda i: (0, 0)),
                  pl.BlockSpec((1, LANES), lambda i: (0, 0)),
                  tab_spec, tab_spec, tab_spec, tab_spec],
        out_specs=out_specs,
        out_shape=out_shape,
        scratch_shapes=[pltpu.VMEM((8, LANES), jnp.float32)],
        compiler_params=_cparams(("arbitrary",)),
        name="in_proj",
    )(x2d, w_perm, wt_perm, bias_row, *tables)
    return dict(zip(names, outs))


def _fox_operands(lf, q, k, carry_ref, qcat_ref, kcat_ref):
    t = lf.shape[0]
    row = lax.broadcasted_iota(jnp.int32, (t, t), 0)
    col = lax.broadcasted_iota(jnp.int32, (t, t), 1)
    tri = jnp.where(col <= row, 1.0, 0.0).astype(jnp.bfloat16)
    n_terms = 3
    lane1 = lax.broadcasted_iota(jnp.int32, (t, LANES), 1)

    def packed_terms(x):
        terms = [term.astype(jnp.float32) for term in _split3(x)]
        packed = jnp.where(lane1 < FOX_HEADS, terms[0],
                           jnp.where(lane1 < 2 * FOX_HEADS, pltpu.roll(terms[1], FOX_HEADS, 1),
                                     pltpu.roll(terms[2], 2 * FOX_HEADS, 1)))
        return packed.astype(jnp.bfloat16)

    parts = _dot(tri, packed_terms(lf))
    c = (parts + pltpu.roll(parts, LANES - FOX_HEADS, 1) + pltpu.roll(parts, LANES - 2 * FOX_HEADS, 1)
         + carry_ref[0:1, :])
    carry_ref[0:1, :] = c[t - 1:t, :]

    src = lax.broadcasted_iota(jnp.int32, (LANES, 2 * FOX_HEADS * LANES), 0)
    dst = lax.broadcasted_iota(jnp.int32, (LANES, 2 * FOX_HEADS * LANES), 1)
    assert FOX_HEADS == 4
    term, head = jnp.right_shift(src, 2), src & 3
    live = src < n_terms * FOX_HEADS
    place = (jnp.where(live & (dst == head * LANES + term), 1.0, 0.0)
             - jnp.where(live & (dst == (FOX_HEADS + head) * LANES + n_terms + term), 1.0, 0.0))
    placed = _dot(packed_terms(c * LOG2E), place.astype(jnp.bfloat16))
    lane = lax.broadcasted_iota(jnp.int32, (t, FOX_HEADS * LANES), 1) & (LANES - 1)
    qa = jnp.where((lane >= n_terms) & (lane < 2 * n_terms), 1.0,
                   placed[:, :FOX_HEADS * LANES]).astype(jnp.bfloat16)
    ka = jnp.where(lane < n_terms, 1.0, placed[:, FOX_HEADS * LANES:]).astype(jnp.bfloat16)
    for h in range(FOX_HEADS):
        pair, hh = divmod(h, 2)
        q2 = q[:, pair * LANES:(pair + 1) * LANES]
        base = h * FOX_CAT
        qcat_ref[:, base:base + LANES] = jnp.where(_half_mask(q2.shape, hh), q2, jnp.zeros_like(q2))
        qcat_ref[:, base + LANES:base + FOX_CAT] = qa[:, h * LANES:(h + 1) * LANES]
        kcat_ref[:, base:base + LANES] = k[:, pair * LANES:(pair + 1) * LANES]
        kcat_ref[:, base + LANES:base + FOX_CAT] = ka[:, h * LANES:(h + 1) * LANES]


def _compress_kernel(ones_half, ch_ref, pos_ref, w1_ref, w2_ref, o_ref):
    n_chunks = ch_ref.shape[0] // CMP_STRIDE
    ch = jnp.concatenate([ch_ref[pl.ds(i, n_chunks, stride=CMP_STRIDE), :] for i in range(CMP_STRIDE)], axis=1)
    a_in = (ch + pos_ref[0:1, :]).astype(jnp.bfloat16)
    b_in = (ch + pos_ref[1:2, :]).astype(jnp.bfloat16)
    row = lax.broadcasted_iota(jnp.int32, (n_chunks, LANES), 0)
    for g in range(NSA_KV_GROUPS):
        a = _dot(a_in, w1_ref[g, 0])
        b = _dot(b_in, w1_ref[g, 1])
        h = a + pltpu.roll(b, n_chunks - 1, 0)
        act = (h * jax.nn.sigmoid(h)).astype(jnp.bfloat16)
        out = jnp.where(row < n_chunks - 1, _dot(act, w2_ref[...]), 0.0)
        if ones_half:
            out_t = out.T
            row_t = lax.broadcasted_iota(jnp.int32, out_t.shape, 0)
            o_ref[g] = jnp.where(row_t >= HEAD_DIM, 1.0, out_t).astype(o_ref.dtype)
        else:
            o_ref[g] = out.astype(o_ref.dtype)


def _compress(tok, pos_emb, w1, w2, batch, seq, ones_half):
    n_chunks = seq // CMP_STRIDE
    width = CMP_STRIDE * NSA_KV_W
    halves = w1.reshape(2, CMP_STRIDE, HEAD_DIM, CMP_HIDDEN)
    w1s = jnp.zeros((NSA_KV_GROUPS, 2, CMP_STRIDE, NSA_KV_GROUPS, HEAD_DIM, CMP_HIDDEN), w1.dtype)
    for g in range(NSA_KV_GROUPS):
        w1s = w1s.at[g, :, :, g].set(halves)
    w1s = w1s.reshape(NSA_KV_GROUPS, 2, width, CMP_HIDDEN).astype(jnp.bfloat16)
    pos = jnp.broadcast_to(pos_emb.reshape(2, CMP_STRIDE, 1, HEAD_DIM),
                           (2, CMP_STRIDE, NSA_KV_GROUPS, HEAD_DIM)).reshape(2, width)
    w2d = jnp.concatenate([w2, jnp.zeros_like(w2) if ones_half else w2], axis=1).astype(jnp.bfloat16)
    sq = None
    out_dims = (LANES, n_chunks) if ones_half else (n_chunks, LANES)
    return pl.pallas_call(
        partial(_compress_kernel, ones_half),
        grid=(batch,),
        in_specs=[pl.BlockSpec((seq, NSA_KV_W), lambda b: (b, 0)),
                  pl.BlockSpec((2, width), lambda b: (0, 0)),
                  pl.BlockSpec((NSA_KV_GROUPS, 2, width, CMP_HIDDEN), lambda b: (0, 0, 0, 0)),
                  pl.BlockSpec((CMP_HIDDEN, LANES), lambda b: (0, 0))],
        out_specs=pl.BlockSpec((sq, NSA_KV_GROUPS) + out_dims, lambda b: (b, 0, 0, 0)),
        out_shape=jax.ShapeDtypeStruct((batch, NSA_KV_GROUPS) + out_dims, jnp.bfloat16),
        compiler_params=_cparams(("parallel",)),
        name="nsa_compress_v" if ones_half else "nsa_compress_k",
    )(tok, pos, w1s, w2d)


def _half_mask(shape, hh):
    lane = lax.broadcasted_iota(jnp.int32, shape, 1)
    return (lane >= hh * HEAD_DIM) & (lane < (hh + 1) * HEAD_DIM)


def _cmp_topk_kernel(q_ref, kc_ref, vct_ref, ovt_ref, tri_ref, oc_ref, negsel_ref, imp_ref):
    tq = q_ref.shape[0]
    step = pl.program_id(2)
    t = step * tq + lax.broadcasted_iota(jnp.int32, (1, tq), 1)
    any_visible = jnp.where(t >= CMP_LEN - 1, 1.0, 0.0)

    def attend(rows):
        n = lax.broadcasted_iota(jnp.int32, (rows, 1), 0)
        mask_c = (CMP_STRIDE * n + (CMP_LEN - 1)) <= t
        kc = kc_ref[0:rows, :]
        vct = vct_ref[:, 0:rows]
        psum = jnp.zeros((rows, tq), jnp.float32)
        for j in range(NSA_HPG // 2):
            q2 = q_ref[:, j * LANES:(j + 1) * LANES]
            o_pair = []
            for hh in range(2):
                qm = jnp.where(_half_mask(q2.shape, hh), q2, jnp.zeros_like(q2))
                st = jnp.where(mask_c, _dot_t(kc, qm), NEG_INF)
                e = jnp.exp2(st - _key_max(st))
                acc = _dot(vct, e.astype(jnp.bfloat16))
                inv = any_visible / acc[HEAD_DIM:HEAD_DIM + 1, :]
                psum = psum + e * inv
                o_pair.append(acc[:HEAD_DIM, :] * inv)
            oc_ref[:, j * LANES:(j + 1) * LANES] = jnp.concatenate(o_pair, axis=0).T
        p_hi = psum.astype(jnp.bfloat16)
        p_lo = (psum - p_hi.astype(jnp.float32)).astype(jnp.bfloat16)
        ovt = ovt_ref[:, 0:rows]
        imp_ref[...] = _dot(ovt, p_hi) + _dot(ovt, p_lo)

    steps_per_variant = (LANES * CMP_STRIDE) // tq
    for variant in range(N_CMP_PAD // LANES):
        @pl.when(step // steps_per_variant == variant)
        def _():
            attend((variant + 1) * LANES)

    blk = lax.broadcasted_iota(jnp.int32, (LANES, 1), 0)
    cur = jnp.right_shift(t, SEL_SHIFT)
    valid = blk * SEL_BLOCK <= t
    forced = (blk == 0) | (blk == cur) | (blk == cur - 1)
    score = jnp.where(valid, jnp.where(forced, FORCE_SCORE, imp_ref[...]), NEG_INF)
    tri = tri_ref[...]
    budget = jnp.full((1, tq), float(SEL_TOPK), jnp.float32)
    sel = jnp.zeros((LANES, tq), jnp.float32)
    for _ in range(SEL_TOPK):
        level = score == _key_max(score)
        score = jnp.where(level, -jnp.inf, score)
        rank = _dot(tri, jnp.where(level, 1.0, 0.0).astype(jnp.bfloat16))
        sel = jnp.where(level & (rank <= budget), 1.0, sel)
        budget = jnp.maximum(budget - rank[LANES - 1:LANES, :], 0.0)
    negsel_ref[...] = jnp.where(sel > 0.5, 0.0, NEG_INF).T.astype(negsel_ref.dtype)


def _cmp_topk(nsa_q, kc, vc, overlap, batch, seq):
    m = nsa_q.shape[0]
    nq = seq // CMP_TQ
    sq = None
    idx = np.arange(LANES)
    tri = jnp.asarray((idx[None, :] <= idx[:, None]).astype(np.float32), jnp.bfloat16)
    return pl.pallas_call(
        _cmp_topk_kernel,
        grid=(batch, NSA_KV_GROUPS, nq),
        in_specs=[pl.BlockSpec((CMP_TQ, NSA_HPG * HEAD_DIM), lambda b, g, i: (b * nq + i, g)),
                  pl.BlockSpec((sq, sq, N_CMP_PAD, LANES), lambda b, g, i: (b, g, 0, 0)),
                  pl.BlockSpec((sq, sq, LANES, N_CMP_PAD), lambda b, g, i: (b, g, 0, 0)),
                  pl.BlockSpec((LANES, N_CMP_PAD), lambda b, g, i: (0, 0)),
                  pl.BlockSpec((LANES, LANES), lambda b, g, i: (0, 0))],
        out_specs=[pl.BlockSpec((CMP_TQ, NSA_HPG * HEAD_DIM), lambda b, g, i: (b * nq + i, g)),
                   pl.BlockSpec((sq, sq, CMP_TQ, LANES), lambda b, g, i: (b, g, i, 0))],
        out_shape=[jax.ShapeDtypeStruct((m, NSA_W), jnp.float32),
                   jax.ShapeDtypeStruct((batch, NSA_KV_GROUPS, seq, LANES), jnp.bfloat16)],
        scratch_shapes=[pltpu.VMEM((LANES, CMP_TQ), jnp.float32)],
        compiler_params=_cparams(("parallel", "parallel", "parallel")),
        name="nsa_cmp_topk",
    )(nsa_q, kc, vc, overlap.T, tri)


def _init_state(m_ref, acc_ref):
    m_ref[...] = jnp.full(m_ref.shape, NEG_INF, jnp.float32)
    acc_ref[...] = jnp.zeros(acc_ref.shape, jnp.float32)


def _key_max(st):
    rows = st.shape[0]
    while rows > 8:
        rows //= 2
        st = jnp.maximum(st[:rows], st[rows:])
    return jnp.max(st, axis=0, keepdims=True)


def _online_step_t(st, vt, m_ref, acc_ref, idx, off=None, lo=None):
    if off is not None:
        st = jnp.where(_key_minus_query(*st.shape) <= off, st, NEG_INF)
    if lo is not None:
        st = jnp.where(_key_minus_query(*st.shape) >= lo, st, NEG_INF)
    m_old = m_ref[idx]
    m_new = jnp.maximum(m_old, _key_max(st))
    alpha = jnp.exp2(m_old - m_new)
    pt = jnp.exp2(st - m_new).astype(jnp.bfloat16)
    acc_ref[idx] = alpha * acc_ref[idx] + _dot(vt, pt)
    m_ref[idx] = m_new


def _normalized_t(acc):
    return acc[:HEAD_DIM, :] / acc[HEAD_DIM:, :]


def _key_minus_query(keys, queries):
    return (lax.broadcasted_iota(jnp.int32, (keys, queries), 0)
            - lax.broadcasted_iota(jnp.int32, (keys, queries), 1))


def _pipelined_sweep(qi, parts, qk, softmax_pv, s0_ref, s1_ref, before_tail=None):
    assert ATT_TQ == 2 * ATT_TK
    late = slice(ATT_TK, ATT_TQ)

    def step(kt_next, s_next, kt, s_cur, off, cols=None):
        for part in range(parts):
            qk(kt_next, s_next, part, cols)
            softmax_pv(kt, s_cur, off, part, None)

    for part in range(parts):
        qk(0, s0_ref, part, None)

    def pair(kt):
        step(kt + 1, s1_ref, kt, s0_ref, None)
        step(kt + 2, s0_ref, kt + 1, s1_ref, None)

    def two_pairs(i, carry):
        pair(4 * i)
        pair(4 * i + 2)
        return carry

    def one_pair(i, carry):
        pair(2 * i)
        return carry

    lax.fori_loop(0, qi // 2, two_pairs, 0)
    lax.fori_loop(2 * (qi // 2), qi, one_pair, 0)
    kt = 2 * qi
    for part in range(parts):
        qk(kt + 1, s1_ref, part, late)
    if before_tail is not None:
        before_tail()
    for part in range(parts):
        softmax_pv(kt, s0_ref, 0, part, None)
    for part in range(parts):
        softmax_pv(kt + 1, s1_ref, 0, part, late)


def _at(part, cols):
    return part if cols is None else (part, slice(None), cols)


def _pair_rows(acc_ref):
    return jnp.concatenate([_normalized_t(acc_ref[hh]) for hh in range(2)], axis=0).T


def _pick_col(x, col):
    lane = lax.broadcasted_iota(jnp.int32, x.shape, 1)
    return jnp.sum(jnp.where(lane == col, x, 0.0), axis=1, keepdims=True)


def _nsa_attn_kernel(q_ref, ksel_ref, vselt_ref, kwin_ref, vwint_ref, negsel_ref, oc_ref, gate_ref, z_ref,
                     o_ref, qcat_ref, s0_ref, s1_ref, w_ref, m_ref, acc_ref, mw_ref, accw_ref):
    tq, tk = ATT_TQ, ATT_TK
    g = pl.program_id(1)
    j = pl.program_id(2)
    qi = pl.program_id(3)
    q2 = q_ref[...]
    negsel = negsel_ref[...]
    for hh in range(2):
        qm = jnp.where(_half_mask(q2.shape, hh), q2, jnp.zeros_like(q2))
        qcat_ref[hh] = jnp.concatenate([qm, negsel], axis=1)
    _init_state(m_ref, acc_ref)

    def qk(kt, s_ref, hh, cols):
        start = pl.multiple_of(kt * tk, tk)
        q = qcat_ref[hh] if cols is None else qcat_ref[hh, cols, :]
        s_ref[_at(hh, cols)] = _dot_t(ksel_ref[pl.ds(start, tk), :], q)

    def softmax_pv(kt, s_ref, off, hh, cols):
        start = pl.multiple_of(kt * tk, tk)
        _online_step_t(s_ref[_at(hh, cols)], vselt_ref[:, pl.ds(start, tk)], m_ref, acc_ref, _at(hh, cols), off)

    halves = tq // tk

    def win_tiles(half):
        wt = qi * halves + half
        return (pl.multiple_of(wt * tk, tk), pl.multiple_of(jnp.maximum(wt - 1, 0) * tk, tk),
                jnp.where(wt > 0, 1, tk))

    def window_qk():
        for half in range(halves):
            cols = slice(half * tk, (half + 1) * tk)
            d0, p0, _ = win_tiles(half)
            for hh in range(2):
                qm = qcat_ref[hh, cols, :LANES]
                w_ref[hh, 0, :, cols] = _dot_t(kwin_ref[pl.ds(d0, tk), :], qm)
                w_ref[hh, 1, :, cols] = _dot_t(kwin_ref[pl.ds(p0, tk), :], qm)

    _pipelined_sweep(qi, 2, qk, softmax_pv, s0_ref, s1_ref, before_tail=window_qk)
    _init_state(mw_ref, accw_ref)
    for half in range(halves):
        cols = slice(half * tk, (half + 1) * tk)
        d0, p0, prev_lo = win_tiles(half)
        for hh in range(2):
            _online_step_t(w_ref[hh, 0, :, cols], vwint_ref[:, pl.ds(d0, tk)], mw_ref, accw_ref, _at(hh, cols),
                           off=0)
            _online_step_t(w_ref[hh, 1, :, cols], vwint_ref[:, pl.ds(p0, tk)], mw_ref, accw_ref, _at(hh, cols),
                           lo=prev_lo)
    o_sel = _pair_rows(acc_ref)
    o_win = _pair_rows(accw_ref)

    gates = gate_ref[...]
    head0 = (g * NSA_HPG + j * 2) * 3
    low = lax.broadcasted_iota(jnp.int32, (tq, LANES), 1) < HEAD_DIM
    mix = [jnp.where(low, _pick_col(gates, head0 + i), _pick_col(gates, head0 + 3 + i)) for i in range(3)]
    out = mix[0] * oc_ref[...] + mix[1] * o_sel + mix[2] * o_win
    o_ref[...] = (out * z_ref[...]).astype(o_ref.dtype)


def _nsa_attn(p, negsel, o_cmp, batch, seq):
    m = p['nsa_q'].shape[0]
    t = ATT_TQ
    nq = seq // t
    sq = None
    row_blk = lambda b, g, j, i: (b * nq + i, g * 2 + j)
    k_spec = pl.BlockSpec((seq, LANES), lambda b, g, j, i: (b, g))
    kcat_spec = pl.BlockSpec((seq, 2 * LANES), lambda b, g, j, i: (b, g))
    vt_spec = pl.BlockSpec((sq, LANES, seq), lambda b, g, j, i: (b, g, 0))
    return pl.pallas_call(
        _nsa_attn_kernel,
        grid=(batch, NSA_KV_GROUPS, NSA_HPG // 2, nq),
        in_specs=[pl.BlockSpec((t, LANES), row_blk),
                  kcat_spec, vt_spec, k_spec, vt_spec,
                  pl.BlockSpec((sq, sq, t, LANES), lambda b, g, j, i: (b, g, i, 0)),
                  pl.BlockSpec((t, LANES), row_blk),
                  pl.BlockSpec((t, LANES), lambda b, g, j, i: (b * nq + i, 0)),
                  pl.BlockSpec((t, LANES), row_blk)],
        out_specs=pl.BlockSpec((t, LANES), row_blk),
        out_shape=jax.ShapeDtypeStruct((m, NSA_W), jnp.bfloat16),
        scratch_shapes=[pltpu.VMEM((2, t, 2 * LANES), jnp.bfloat16),
                        pltpu.VMEM((2, ATT_TK, t), jnp.float32),
                        pltpu.VMEM((2, ATT_TK, t), jnp.float32),
                        pltpu.VMEM((2, 2, ATT_TK, t), jnp.float32),
                        pltpu.VMEM((2, 1, t), jnp.float32),
                        pltpu.VMEM((2, LANES, t), jnp.float32),
                        pltpu.VMEM((2, 1, t), jnp.float32),
                        pltpu.VMEM((2, LANES, t), jnp.float32)],
        compiler_params=_cparams(("parallel", "parallel", "parallel", "arbitrary")),
        name="nsa_sel_win_attn",
    )(p['nsa_q'], p['nsa_k_sel'], p['nsa_v_sel_t'], p['nsa_k_win'], p['nsa_v_win_t'], negsel, o_cmp,
      p['nsa_gate'], p['nsa_z'])


def _fox_kernel(q_ref, k_ref, vt_ref, z_ref, o_ref, s0_ref, s1_ref, m_ref, acc_ref):
    tk = ATT_TK
    qi = pl.program_id(2)
    _init_state(m_ref, acc_ref)

    def qk(kt, s_ref, hh, cols):
        start = pl.multiple_of(kt * tk, tk)
        cat = slice(hh * FOX_CAT, (hh + 1) * FOX_CAT)
        q = q_ref[:, cat] if cols is None else q_ref[cols, cat]
        s_ref[_at(hh, cols)] = _dot_t(k_ref[pl.ds(start, tk), cat], q)

    def softmax_pv(kt, s_ref, off, hh, cols):
        start = pl.multiple_of(kt * tk, tk)
        _online_step_t(s_ref[_at(hh, cols)], vt_ref[hh * LANES:(hh + 1) * LANES, pl.ds(start, tk)],
                       m_ref, acc_ref, _at(hh, cols), off)

    _pipelined_sweep(qi, 2, qk, softmax_pv, s0_ref, s1_ref)
    o_ref[...] = (_pair_rows(acc_ref) * z_ref[...]).astype(o_ref.dtype)


def _fox_attn(p, qcat, kcat, batch, seq):
    m = qcat.shape[0]
    t = ATT_TQ
    nq = seq // t
    row_blk = lambda b, j, i: (b * nq + i, j)
    return pl.pallas_call(
        _fox_kernel,
        grid=(batch, FOX_HEADS // 2, nq),
        in_specs=[pl.BlockSpec((t, 2 * FOX_CAT), row_blk),
                  pl.BlockSpec((seq, 2 * FOX_CAT), lambda b, j, i: (b, j)),
                  pl.BlockSpec((None, 2 * LANES, seq), lambda b, j, i: (b, j, 0)),
                  pl.BlockSpec((t, LANES), row_blk)],
        out_specs=pl.BlockSpec((t, LANES), row_blk),
        out_shape=jax.ShapeDtypeStruct((m, FOX_W), jnp.bfloat16),
        scratch_shapes=[pltpu.VMEM((2, ATT_TK, t), jnp.float32),
                        pltpu.VMEM((2, ATT_TK, t), jnp.float32),
                        pltpu.VMEM((2, 1, t), jnp.float32),
                        pltpu.VMEM((2, LANES, t), jnp.float32)],
        compiler_params=_cparams(("parallel", "parallel", "arbitrary")),
        name="fox_attn",
    )(qcat, kcat, p['fox_vt'], p['fox_z'])


def _diff_kernel(lam_init, q_ref, k_ref, vt_ref, z_ref, lam_ref, g_ref, o_ref, qm_ref, s0_ref, s1_ref,
                 m_ref, acc_ref):
    tq, tk = ATT_TQ, ATT_TK
    qi = pl.program_id(2)
    q2 = q_ref[...]
    lane = lax.broadcasted_iota(jnp.int32, (tq, LANES), 1)
    for idx in range(4):
        lo = idx * DIFF_QK_DIM
        qm_ref[idx] = jnp.where((lane >= lo) & (lane < lo + DIFF_QK_DIM), q2, jnp.zeros_like(q2))
    _init_state(m_ref, acc_ref)

    def qk(kt, s_ref, idx, cols):
        start = pl.multiple_of(kt * tk, tk)
        q = qm_ref[idx] if cols is None else qm_ref[idx, cols, :]
        s_ref[_at(idx, cols)] = _dot_t(k_ref[pl.ds(start, tk), :], q)

    def softmax_pv(kt, s_ref, off, idx, cols):
        start = pl.multiple_of(kt * tk, tk)
        hh = idx // 2
        _online_step_t(s_ref[_at(idx, cols)], vt_ref[hh * LANES:(hh + 1) * LANES, pl.ds(start, tk)],
                       m_ref, acc_ref, _at(idx, cols), off)

    _pipelined_sweep(qi, 4, qk, softmax_pv, s0_ref, s1_ref)

    lam_rows = lam_ref[...]
    lam = (jnp.exp(jnp.sum(lam_rows[0:1, :] * lam_rows[1:2, :], axis=1, keepdims=True))
           - jnp.exp(jnp.sum(lam_rows[2:3, :] * lam_rows[3:4, :], axis=1, keepdims=True)) + lam_init)
    o = [_normalized_t(acc_ref[idx]) for idx in range(4)]
    normed = []
    for hh in range(2):
        d = o[2 * hh] - lam * o[2 * hh + 1]
        ms = jnp.sum(d * d, axis=0, keepdims=True) / HEAD_DIM
        normed.append(d * lax.rsqrt(ms + RMS_EPS))
    out = jnp.concatenate(normed, axis=0).T * g_ref[...] * (1.0 - lam_init)
    o_ref[...] = (out * z_ref[...]).astype(o_ref.dtype)


def _diff_attn(p, lam_rows, g_row, lam_init, batch, seq):
    m = p['diff_q'].shape[0]
    t = ATT_TQ
    nq = seq // t
    row_blk = lambda b, j, i: (b * nq + i, j)
    return pl.pallas_call(
        partial(_diff_kernel, lam_init),
        grid=(batch, DIFF_HEADS // 2, nq),
        in_specs=[pl.BlockSpec((t, LANES), row_blk),
                  pl.BlockSpec((seq, LANES), lambda b, j, i: (b, j)),
                  pl.BlockSpec((None, 2 * LANES, seq), lambda b, j, i: (b, j, 0)),
                  pl.BlockSpec((t, LANES), row_blk),
                  pl.BlockSpec((8, LANES), lambda b, j, i: (0, 0)),
                  pl.BlockSpec((1, LANES), lambda b, j, i: (0, 0))],
        out_specs=pl.BlockSpec((t, LANES), row_blk),
        out_shape=jax.ShapeDtypeStruct((m, DIFF_W), jnp.bfloat16),
        scratch_shapes=[pltpu.VMEM((4, t, LANES), jnp.bfloat16),
                        pltpu.VMEM((4, ATT_TK, t), jnp.float32),
                        pltpu.VMEM((4, ATT_TK, t), jnp.float32),
                        pltpu.VMEM((4, 1, t), jnp.float32),
                        pltpu.VMEM((4, LANES, t), jnp.float32)],
        compiler_params=_cparams(("parallel", "parallel", "arbitrary")),
        name="diff_attn",
    )(p['diff_q'], p['diff_k'], p['diff_vt'], p['diff_z'], lam_rows, g_row)


def _out_kernel(of_ref, on_ref, od_ref, w_ref, x_ref, g_ref, b_ref, o_ref):
    y = (DEEPNORM_ALPHA * x_ref[...]
         + _dot(of_ref[...], w_ref[0:FOX_W, :])
         + _dot(on_ref[...], w_ref[FOX_W:FOX_W + NSA_W, :])
         + _dot(od_ref[...], w_ref[FOX_W + NSA_W:, :]))
    mu = jnp.mean(y, axis=1, keepdims=True)
    yc = y - mu
    var = jnp.mean(yc * yc, axis=1, keepdims=True)
    o_ref[...] = yc * lax.rsqrt(var + LN_EPS) * g_ref[...] + b_ref[...]


def _out_proj(o_fox, o_nsa, o_diff, w_out, x2d, g_row, b_row):
    m = x2d.shape[0]
    tm = OUT_TM
    row = lambda i: (i, 0)
    const = lambda i: (0, 0)
    return pl.pallas_call(
        _out_kernel,
        grid=(m // tm,),
        in_specs=[pl.BlockSpec((tm, FOX_W), row), pl.BlockSpec((tm, NSA_W), row),
                  pl.BlockSpec((tm, DIFF_W), row),
                  pl.BlockSpec((D_MODEL, D_MODEL), const),
                  pl.BlockSpec((tm, D_MODEL), row),
                  pl.BlockSpec((1, D_MODEL), const), pl.BlockSpec((1, D_MODEL), const)],
        out_specs=pl.BlockSpec((tm, D_MODEL), row),
        out_shape=jax.ShapeDtypeStruct((m, D_MODEL), jnp.float32),
        compiler_params=_cparams(("parallel",)),
        name="out_proj_ln",
    )(o_fox, o_nsa, o_diff, w_out, x2d, g_row, b_row)


def _overlap_matrix(seq):
    n_slc = seq // SEL_BLOCK
    cmp_start = CMP_STRIDE * np.arange(N_CMP_PAD)
    sel_start = SEL_BLOCK * np.arange(n_slc)
    ov = np.clip(np.minimum(cmp_start[:, None] + CMP_LEN, sel_start[None, :] + SEL_BLOCK)
                 - np.maximum(cmp_start[:, None], sel_start[None, :]), 0, None).astype(np.float32) / CMP_LEN
    return ov


def kernel(x, w_in, b_fox_f, cmp_pos_k, cmp_pos_v, cmp_w1_k, cmp_w2_k, cmp_w1_v, cmp_w2_v,
           lam_q1, lam_k1, lam_q2, lam_k2, diff_subln_g, w_out, ln_g, ln_b):
    batch, seq, _ = x.shape
    assert seq % ATT_TQ == 0 and seq // SEL_BLOCK == LANES and (seq - CMP_LEN) // CMP_STRIDE + 1 < N_CMP_PAD
    assert WINDOW == ATT_TK and ATT_TQ % ATT_TK == 0
    m = batch * seq
    tables = _rope_tables(seq, HEAD_DIM) + _rope_tables(seq, DIFF_QK_DIM)
    overlap = jnp.asarray(_overlap_matrix(seq), jnp.bfloat16)

    x2d = x.reshape(m, D_MODEL)
    for l in range(DEPTH):
        w_perm = _permute_columns(w_in[l], PROJ_COLS).astype(jnp.bfloat16)
        wt_perm = _permute_columns(w_in[l], VT_COLS).astype(jnp.bfloat16).T
        bias_row = jnp.zeros((1, LANES), jnp.float32).at[0, :FOX_HEADS].set(b_fox_f[l])
        p = _project(x2d, w_perm, wt_perm, bias_row, tables, seq)

        o_fox = _fox_attn(p, p['fox_qcat'], p['fox_kcat'], batch, seq)

        kc = _compress(p['nsa_k_cmp'], cmp_pos_k[l], cmp_w1_k[l], cmp_w2_k[l], batch, seq, False)
        vc = _compress(p['nsa_v_cmp'], cmp_pos_v[l], cmp_w1_v[l], cmp_w2_v[l], batch, seq, True)
        o_cmp, negsel = _cmp_topk(p['nsa_q'], kc, vc, overlap, batch, seq)
        o_nsa = _nsa_attn(p, negsel, o_cmp, batch, seq)

        lam_init = 0.8 - 0.6 * math.exp(-0.3 * l)
        lam_rows = jnp.zeros((8, LANES), jnp.float32)
        for r, v in enumerate((lam_q1[l], lam_k1[l], lam_q2[l], lam_k2[l])):
            lam_rows = lam_rows.at[r, :DIFF_QK_DIM].set(v.astype(jnp.float32))
        g_row = jnp.tile(diff_subln_g[l].reshape(1, HEAD_DIM), (1, 2))
        o_diff = _diff_attn(p, lam_rows, g_row, lam_init, batch, seq)

        x2d = _out_proj(o_fox, o_nsa, o_diff, w_out[l].astype(jnp.bfloat16), x2d,
                        ln_g[l].reshape(1, D_MODEL), ln_b[l].reshape(1, D_MODEL))
    return x2d.reshape(batch, seq, D_MODEL)
```

```python
import math
from functools import partial

import numpy as np
import jax
import jax.numpy as jnp
from jax import lax
from jax.experimental import pallas as pl
from jax.experimental.pallas import tpu as pltpu

D_MODEL = 1024
DEPTH = 2
HEAD_DIM = 64
FOX_HEADS = 4
NSA_HEADS = 8
DIFF_HEADS = 4
FOX_W = FOX_HEADS * HEAD_DIM
NSA_W = NSA_HEADS * HEAD_DIM
DIFF_QK_DIM = HEAD_DIM // 2
DIFF_W = DIFF_HEADS * HEAD_DIM
NSA_KV_GROUPS = 2
NSA_HPG = NSA_HEADS // NSA_KV_GROUPS
NSA_KV_W = NSA_KV_GROUPS * HEAD_DIM
CMP_LEN = 32
CMP_STRIDE = 16
CMP_HIDDEN = 2 * HEAD_DIM
SEL_BLOCK = 64
SEL_SHIFT = SEL_BLOCK.bit_length() - 1
SEL_TOPK = 16
WINDOW = 512
ROPE_THETA = 10000.0
LN_EPS = 1e-5
RMS_EPS = 1e-5
NEG_INF = -1e30
FORCE_SCORE = 1e30
DEEPNORM_ALPHA = (2 * DEPTH) ** 0.25
LOG2E = math.log2(math.e)

SPLITS = (
    ('fox_q', FOX_W), ('fox_k', FOX_W), ('fox_v', FOX_W), ('fox_f', FOX_HEADS), ('fox_z', FOX_W),
    ('nsa_q', NSA_W),
    ('nsa_k_cmp', NSA_KV_W), ('nsa_v_cmp', NSA_KV_W),
    ('nsa_k_sel', NSA_KV_W), ('nsa_v_sel', NSA_KV_W),
    ('nsa_k_win', NSA_KV_W), ('nsa_v_win', NSA_KV_W),
    ('nsa_gate', 3 * NSA_HEADS), ('nsa_z', NSA_W),
    ('diff_q', DIFF_W), ('diff_k', DIFF_W), ('diff_v', DIFF_W), ('diff_z', DIFF_W),
)
_OFF = {}
_acc = 0
for _n, _w in SPLITS:
    _OFF[_n] = _acc
    _acc += _w
IN_WIDTH = _acc

LANES = 128
VMEM_LIMIT = 56 * 1024 * 1024

PROJ_TM = 512
OUT_TM = 1024
CMP_TQ = 512
ATT_TQ = 1024
ATT_TK = 512
N_CMP_PAD = 512
FOX_CAT = 2 * LANES


def _cparams(sem):
    return pltpu.CompilerParams(dimension_semantics=sem, vmem_limit_bytes=VMEM_LIMIT)


def _dot(a, b):
    return jnp.dot(a, b, preferred_element_type=jnp.float32)


def _dot_t(a, b):
    return lax.dot_general(a, b, (((1,), (1,)), ((), ())), preferred_element_type=jnp.float32)


def _split3(x):
    hi = x.astype(jnp.bfloat16)
    r1 = x - hi.astype(jnp.float32)
    mid = r1.astype(jnp.bfloat16)
    lo = (r1 - mid.astype(jnp.float32)).astype(jnp.bfloat16)
    return hi, mid, lo


def _dup(name):
    base = _OFF[name]
    cols = []
    for g in range(NSA_KV_GROUPS):
        one = list(range(base + g * HEAD_DIM, base + (g + 1) * HEAD_DIM))
        cols += one + one
    return cols


def _pad(cols, width):
    return list(cols) + [-1] * (width - len(cols))


def _rng(name, width):
    return list(range(_OFF[name], _OFF[name] + width))


Q_SCALE = HEAD_DIM ** -0.5 * LOG2E
DIFF_Q_SCALE = DIFF_QK_DIM ** -0.5 * LOG2E

PROJ_SEGS = (
    ('fox_q', _rng('fox_q', FOX_W), 'none', jnp.bfloat16, Q_SCALE),
    ('fox_k', _rng('fox_k', FOX_W), 'none', jnp.bfloat16, None),
    ('fox_z', _rng('fox_z', FOX_W), 'silu', jnp.float32, None),
    ('fox_logf', _pad(_rng('fox_f', FOX_HEADS), LANES), 'logsig', jnp.float32, None),
    ('nsa_q', _rng('nsa_q', NSA_W), 'rope64', jnp.bfloat16, Q_SCALE),
    ('nsa_k_cmp', _rng('nsa_k_cmp', NSA_KV_W), 'rope64', jnp.float32, None),
    ('nsa_v_cmp', _rng('nsa_v_cmp', NSA_KV_W), 'none', jnp.float32, None),
    ('nsa_k_sel', _dup('nsa_k_sel'), 'rope64_blockhot', jnp.bfloat16, None),
    ('nsa_k_win', _dup('nsa_k_win'), 'rope64', jnp.bfloat16, None),
    ('nsa_gate', _pad(_rng('nsa_gate', 3 * NSA_HEADS), LANES), 'sigmoid', jnp.float32, None),
    ('nsa_z', _rng('nsa_z', NSA_W), 'silu', jnp.float32, None),
    ('diff_q', _rng('diff_q', DIFF_W), 'rope32', jnp.bfloat16, DIFF_Q_SCALE),
    ('diff_k', _rng('diff_k', DIFF_W), 'rope32', jnp.bfloat16, None),
    ('diff_z', _rng('diff_z', DIFF_W), 'silu', jnp.float32, None),
)
PROJ_COLS = np.concatenate([np.asarray(seg[1], np.int32) for seg in PROJ_SEGS])
PROJ_N = int(PROJ_COLS.shape[0])

VT_SEGS = (
    ('fox_vt', _rng('fox_v', FOX_W)),
    ('nsa_v_sel_t', _rng('nsa_v_sel', NSA_KV_W)),
    ('nsa_v_win_t', _rng('nsa_v_win', NSA_KV_W)),
    ('diff_vt', _rng('diff_v', DIFF_W)),
)
VT_COLS = np.concatenate([np.asarray(seg[1], np.int32) for seg in VT_SEGS])
VT_N = int(VT_COLS.shape[0])


def _permute_columns(w, col_ids):
    pieces, start = [], 0
    cols = col_ids.tolist()
    while start < len(cols):
        end = start + 1
        if cols[start] < 0:
            while end < len(cols) and cols[end] < 0:
                end += 1
            pieces.append(jnp.zeros((w.shape[0], end - start), w.dtype))
        else:
            while end < len(cols) and cols[end] == cols[end - 1] + 1:
                end += 1
            pieces.append(w[:, cols[start]:cols[end - 1] + 1])
        start = end
    return jnp.concatenate(pieces, axis=1)


def _rope_chunk(y, cos, sin, half):
    lane = lax.broadcasted_iota(jnp.int32, y.shape, 1)
    first = (lane & (2 * half - 1)) < half
    partner = jnp.where(first, pltpu.roll(y, LANES - half, 1), pltpu.roll(y, half, 1))
    return y * cos + partner * sin


PROJ_HELD = ('fox_q', 'fox_k', 'fox_logf')
PROJ_STORED = tuple(seg for seg in PROJ_SEGS if seg[0] not in PROJ_HELD)


def _out_width(seg):
    return 2 * len(seg[1]) if seg[2] == 'rope64_blockhot' else len(seg[1])


def _proj_kernel(seq, x_ref, w_ref, wt_ref, bias_ref, cos64_ref, sin64_ref, cos32_ref, sin32_ref, *refs):
    stored = dict(zip((seg[0] for seg in PROJ_STORED), refs))
    vt_refs = refs[len(PROJ_STORED):len(PROJ_STORED) + len(VT_SEGS)]
    qcat_ref, kcat_ref, carry_ref = refs[len(PROJ_STORED) + len(VT_SEGS):]
    xb = x_ref[...].astype(jnp.bfloat16)
    held = {}
    start = 0
    for (name, cols), o_ref in zip(VT_SEGS, vt_refs):
        rows = len(cols)
        yt = _dot_t(wt_ref[start:start + rows, :], xb).astype(o_ref.dtype)
        start += rows
        ones = jnp.ones((HEAD_DIM, yt.shape[1]), o_ref.dtype)
        for h in range(rows // HEAD_DIM):
            o_ref[2 * h * HEAD_DIM:(2 * h + 1) * HEAD_DIM, :] = yt[h * HEAD_DIM:(h + 1) * HEAD_DIM, :]
            o_ref[(2 * h + 1) * HEAD_DIM:(2 * h + 2) * HEAD_DIM, :] = ones
    start = 0
    for name, cols, epi, dtype, scale in PROJ_SEGS:
        width = len(cols)
        y = _dot(xb, w_ref[:, start:start + width])
        start += width
        if epi == 'silu':
            y = y * jax.nn.sigmoid(y)
        elif epi == 'sigmoid':
            y = jax.nn.sigmoid(y)
        elif epi == 'logsig':
            y = y + bias_ref[...]
            y = jnp.minimum(y, 0.0) - jnp.log1p(jnp.exp(-jnp.abs(y)))
        elif epi in ('rope64', 'rope32', 'rope64_blockhot'):
            half = DIFF_QK_DIM // 2 if epi == 'rope32' else HEAD_DIM // 2
            cos = cos32_ref[...] if epi == 'rope32' else cos64_ref[...]
            sin = sin32_ref[...] if epi == 'rope32' else sin64_ref[...]
            chunks = [_rope_chunk(y[:, c * LANES:(c + 1) * LANES], cos, sin, half)
                      for c in range(width // LANES)]
            if epi == 'rope64_blockhot':
                tm = y.shape[0]
                pos = (pl.program_id(0) % (seq // tm)) * tm + lax.broadcasted_iota(jnp.int32, (tm, LANES), 0)
                lane = lax.broadcasted_iota(jnp.int32, (tm, LANES), 1)
                hot = jnp.where(lane == jnp.right_shift(pos, SEL_SHIFT), 1.0, 0.0)
                chunks = [piece for chunk in chunks for piece in (chunk, hot)]
            y = chunks[0] if len(chunks) == 1 else jnp.concatenate(chunks, axis=1)
        if scale is not None:
            y = y * scale
        if name in PROJ_HELD:
            held[name] = y.astype(dtype)
        else:
            stored[name][...] = y.astype(dtype)

    @pl.when(pl.program_id(0) % (seq // x_ref.shape[0]) == 0)
    def _():
        carry_ref[...] = jnp.zeros_like(carry_ref)

    _fox_operands(held['fox_logf'], held['fox_q'], held['fox_k'], carry_ref, qcat_ref, kcat_ref)


def _rope_tables(seq, group):
    half = group // 2
    inv = ROPE_THETA ** (-(jnp.arange(half, dtype=jnp.float32) * 2.0 / group))
    ang = jnp.arange(seq, dtype=jnp.float32)[:, None] * inv[None, :]
    cos, sin = jnp.cos(ang), jnp.sin(ang)
    reps = LANES // group
    cos_t = jnp.tile(jnp.concatenate([cos, cos], axis=1), (1, reps))
    sin_t = jnp.tile(jnp.concatenate([-sin, sin], axis=1), (1, reps))
    return cos_t, sin_t


def _project(x2d, w_perm, wt_perm, bias_row, tables, seq):
    m = x2d.shape[0]
    nt = seq // PROJ_TM
    tab_spec = pl.BlockSpec((PROJ_TM, LANES), lambda i: (i % nt, 0))
    out_shape = [jax.ShapeDtypeStruct((m, _out_width(seg)), seg[3]) for seg in PROJ_STORED]
    out_specs = [pl.BlockSpec((PROJ_TM, _out_width(seg)), lambda i: (i, 0)) for seg in PROJ_STORED]
    out_shape += [jax.ShapeDtypeStruct((m // seq, 2 * len(seg[1]), seq), jnp.bfloat16) for seg in VT_SEGS]
    out_specs += [pl.BlockSpec((None, 2 * len(seg[1]), PROJ_TM), lambda i: (i // nt, 0, i % nt))
                  for seg in VT_SEGS]
    out_shape += [jax.ShapeDtypeStruct((m, FOX_HEADS * FOX_CAT), jnp.bfloat16)] * 2
    out_specs += [pl.BlockSpec((PROJ_TM, FOX_HEADS * FOX_CAT), lambda i: (i, 0))] * 2
    names = [seg[0] for seg in PROJ_STORED + VT_SEGS] + ['fox_qcat', 'fox_kcat']
    outs = pl.pallas_call(
        partial(_proj_kernel, seq),
        grid=(m // PROJ_TM,),
        in_specs=[pl.BlockSpec((PROJ_TM, D_MODEL), lambda i: (i, 0)),
                  pl.BlockSpec((D_MODEL, PROJ_N), lambda i: (0, 0)),
                  pl.BlockSpec((VT_N, D_MODEL), lambda i: (0, 0)),
                  pl.BlockSpec((1, LANES), lambda i: (0, 0)),
                  tab_spec, tab_spec, tab_spec, tab_spec],
        out_specs=out_specs,
        out_shape=out_shape,
        scratch_shapes=[pltpu.VMEM((8, LANES), jnp.float32)],
        compiler_params=_cparams(("arbitrary",)),
        name="in_proj",
    )(x2d, w_perm, wt_perm, bias_row, *tables)
    return dict(zip(names, outs))


def _fox_operands(lf, q, k, carry_ref, qcat_ref, kcat_ref):
    t = lf.shape[0]
    row = lax.broadcasted_iota(jnp.int32, (t, t), 0)
    col = lax.broadcasted_iota(jnp.int32, (t, t), 1)
    tri = jnp.where(col <= row, 1.0, 0.0).astype(jnp.bfloat16)
    n_terms = 3
    lane1 = lax.broadcasted_iota(jnp.int32, (t, LANES), 1)

    def packed_terms(x):
        terms = [term.astype(jnp.float32) for term in _split3(x)]
        packed = jnp.where(lane1 < FOX_HEADS, terms[0],
                           jnp.where(lane1 < 2 * FOX_HEADS, pltpu.roll(terms[1], FOX_HEADS, 1),
                                     pltpu.roll(terms[2], 2 * FOX_HEADS, 1)))
        return packed.astype(jnp.bfloat16)

    parts = _dot(tri, packed_terms(lf))
    c = (parts + pltpu.roll(parts, LANES - FOX_HEADS, 1) + pltpu.roll(parts, LANES - 2 * FOX_HEADS, 1)
         + carry_ref[0:1, :])
    carry_ref[0:1, :] = c[t - 1:t, :]

    src = lax.broadcasted_iota(jnp.int32, (LANES, 2 * FOX_HEADS * LANES), 0)
    dst = lax.broadcasted_iota(jnp.int32, (LANES, 2 * FOX_HEADS * LANES), 1)
    assert FOX_HEADS == 4
    term, head = jnp.right_shift(src, 2), src & 3
    live = src < n_terms * FOX_HEADS
    place = (jnp.where(live & (dst == head * LANES + term), 1.0, 0.0)
             - jnp.where(live & (dst == (FOX_HEADS + head) * LANES + n_terms + term), 1.0, 0.0))
    placed = _dot(packed_terms(c * LOG2E), place.astype(jnp.bfloat16))
    lane = lax.broadcasted_iota(jnp.int32, (t, FOX_HEADS * LANES), 1) & (LANES - 1)
    qa = jnp.where((lane >= n_terms) & (lane < 2 * n_terms), 1.0,
                   placed[:, :FOX_HEADS * LANES]).astype(jnp.bfloat16)
    ka = jnp.where(lane < n_terms, 1.0, placed[:, FOX_HEADS * LANES:]).astype(jnp.bfloat16)
    for h in range(FOX_HEADS):
        pair, hh = divmod(h, 2)
        q2 = q[:, pair * LANES:(pair + 1) * LANES]
        base = h * FOX_CAT
        qcat_ref[:, base:base + LANES] = jnp.where(_half_mask(q2.shape, hh), q2, jnp.zeros_like(q2))
        qcat_ref[:, base + LANES:base + FOX_CAT] = qa[:, h * LANES:(h + 1) * LANES]
        kcat_ref[:, base:base + LANES] = k[:, pair * LANES:(pair + 1) * LANES]
        kcat_ref[:, base + LANES:base + FOX_CAT] = ka[:, h * LANES:(h + 1) * LANES]


def _compress_kernel(ones_half, ch_ref, pos_ref, w1_ref, w2_ref, o_ref):
    n_chunks = ch_ref.shape[0] // CMP_STRIDE
    ch = jnp.concatenate([ch_ref[pl.ds(i, n_chunks, stride=CMP_STRIDE), :] for i in range(CMP_STRIDE)], axis=1)
    a_in = (ch + pos_ref[0:1, :]).astype(jnp.bfloat16)
    b_in = (ch + pos_ref[1:2, :]).astype(jnp.bfloat16)
    row = lax.broadcasted_iota(jnp.int32, (n_chunks, LANES), 0)
    for g in range(NSA_KV_GROUPS):
        a = _dot(a_in, w1_ref[g, 0])
        b = _dot(b_in, w1_ref[g, 1])
        h = a + pltpu.roll(b, n_chunks - 1, 0)
        act = (h * jax.nn.sigmoid(h)).astype(jnp.bfloat16)
        out = jnp.where(row < n_chunks - 1, _dot(act, w2_ref[...]), 0.0)
        if ones_half:
            out_t = out.T
            row_t = lax.broadcasted_iota(jnp.int32, out_t.shape, 0)
            o_ref[g] = jnp.where(row_t >= HEAD_DIM, 1.0, out_t).astype(o_ref.dtype)
        else:
            o_ref[g] = out.astype(o_ref.dtype)


def _compress(tok, pos_emb, w1, w2, batch, seq, ones_half):
    n_chunks = seq // CMP_STRIDE
    width = CMP_STRIDE * NSA_KV_W
    halves = w1.reshape(2, CMP_STRIDE, HEAD_DIM, CMP_HIDDEN)
    w1s = jnp.zeros((NSA_KV_GROUPS, 2, CMP_STRIDE, NSA_KV_GROUPS, HEAD_DIM, CMP_HIDDEN), w1.dtype)
    for g in range(NSA_KV_GROUPS):
        w1s = w1s.at[g, :, :, g].set(halves)
    w1s = w1s.reshape(NSA_KV_GROUPS, 2, width, CMP_HIDDEN).astype(jnp.bfloat16)
    pos = jnp.broadcast_to(pos_emb.reshape(2, CMP_STRIDE, 1, HEAD_DIM),
                           (2, CMP_STRIDE, NSA_KV_GROUPS, HEAD_DIM)).reshape(2, width)
    w2d = jnp.concatenate([w2, jnp.zeros_like(w2) if ones_half else w2], axis=1).astype(jnp.bfloat16)
    sq = None
    out_dims = (LANES, n_chunks) if ones_half else (n_chunks, LANES)
    return pl.pallas_call(
        partial(_compress_kernel, ones_half),
        grid=(batch,),
        in_specs=[pl.BlockSpec((seq, NSA_KV_W), lambda b: (b, 0)),
                  pl.BlockSpec((2, width), lambda b: (0, 0)),
                  pl.BlockSpec((NSA_KV_GROUPS, 2, width, CMP_HIDDEN), lambda b: (0, 0, 0, 0)),
                  pl.BlockSpec((CMP_HIDDEN, LANES), lambda b: (0, 0))],
        out_specs=pl.BlockSpec((sq, NSA_KV_GROUPS) + out_dims, lambda b: (b, 0, 0, 0)),
        out_shape=jax.ShapeDtypeStruct((batch, NSA_KV_GROUPS) + out_dims, jnp.bfloat16),
        compiler_params=_cparams(("parallel",)),
        name="nsa_compress_v" if ones_half else "nsa_compress_k",
    )(tok, pos, w1s, w2d)


def _half_mask(shape, hh):
    lane = lax.broadcasted_iota(jnp.int32, shape, 1)
    return (lane >= hh * HEAD_DIM) & (lane < (hh + 1) * HEAD_DIM)


def _cmp_topk_kernel(q_ref, kc_ref, vct_ref, ovt_ref, tri_ref, oc_ref, negsel_ref, imp_ref):
    tq = q_ref.shape[0]
    step = pl.program_id(2)
    t = step * tq + lax.broadcasted_iota(jnp.int32, (1, tq), 1)
    any_visible = jnp.where(t >= CMP_LEN - 1, 1.0, 0.0)

    def attend(rows):
        n = lax.broadcasted_iota(jnp.int32, (rows, 1), 0)
        mask_c = (CMP_STRIDE * n + (CMP_LEN - 1)) <= t
        kc = kc_ref[0:rows, :]
        vct_ovt = jnp.concatenate([vct_ref[:, 0:rows], ovt_ref[:, 0:rows]], axis=0)
        def scores(head):
            j, hh = divmod(head, 2)
            q2 = q_ref[:, j * LANES:(j + 1) * LANES]
            return _dot_t(kc, jnp.where(_half_mask(q2.shape, hh), q2, jnp.zeros_like(q2)))

        imp = jnp.zeros((LANES, tq), jnp.float32)
        outs = []
        st_next = scores(0)
        for head in range(NSA_HPG):
            st = jnp.where(mask_c, st_next, NEG_INF)
            if head + 1 < NSA_HPG:
                st_next = scores(head + 1)
            e = jnp.exp2(st - _key_max(st)).astype(jnp.bfloat16)
            res = _dot(vct_ovt, e)
            inv = any_visible / res[HEAD_DIM:HEAD_DIM + 1, :]
            imp = imp + res[LANES:, :] * inv
            outs.append(res[:HEAD_DIM, :] * inv)
        for j in range(NSA_HPG // 2):
            oc_ref[:, j * LANES:(j + 1) * LANES] = jnp.concatenate(outs[2 * j:2 * j + 2], axis=0).T
        imp_ref[...] = imp

    steps_per_variant = (LANES * CMP_STRIDE) // tq
    for variant in range(N_CMP_PAD // LANES):
        @pl.when(step // steps_per_variant == variant)
        def _():
            attend((variant + 1) * LANES)

    blk = lax.broadcasted_iota(jnp.int32, (LANES, 1), 0)
    cur = jnp.right_shift(t, SEL_SHIFT)
    valid = blk * SEL_BLOCK <= t
    forced = (blk == 0) | (blk == cur) | (blk == cur - 1)
    score = jnp.where(valid, jnp.where(forced, FORCE_SCORE, imp_ref[...]), NEG_INF)
    tri = tri_ref[...]
    budget = jnp.full((1, tq), float(SEL_TOPK), jnp.float32)
    sel = jnp.zeros((LANES, tq), jnp.float32)
    for _ in range(SEL_TOPK):
        level = score == _key_max(score)
        score = jnp.where(level, -jnp.inf, score)
        rank = _dot(tri, jnp.where(level, 1.0, 0.0).astype(jnp.bfloat16))
        sel = jnp.where(level & (rank <= budget), 1.0, sel)
        budget = jnp.maximum(budget - rank[LANES - 1:LANES, :], 0.0)
    negsel_ref[...] = jnp.where(sel > 0.5, 0.0, NEG_INF).T.astype(negsel_ref.dtype)


def _cmp_topk(nsa_q, kc, vc, overlap, batch, seq):
    m = nsa_q.shape[0]
    nq = seq // CMP_TQ
    sq = None
    idx = np.arange(LANES)
    tri = jnp.asarray((idx[None, :] <= idx[:, None]).astype(np.float32), jnp.bfloat16)
    return pl.pallas_call(
        _cmp_topk_kernel,
        grid=(batch, NSA_KV_GROUPS, nq),
        in_specs=[pl.BlockSpec((CMP_TQ, NSA_HPG * HEAD_DIM), lambda b, g, i: (b * nq + i, g)),
                  pl.BlockSpec((sq, sq, N_CMP_PAD, LANES), lambda b, g, i: (b, g, 0, 0)),
                  pl.BlockSpec((sq, sq, LANES, N_CMP_PAD), lambda b, g, i: (b, g, 0, 0)),
                  pl.BlockSpec((LANES, N_CMP_PAD), lambda b, g, i: (0, 0)),
                  pl.BlockSpec((LANES, LANES), lambda b, g, i: (0, 0))],
        out_specs=[pl.BlockSpec((CMP_TQ, NSA_HPG * HEAD_DIM), lambda b, g, i: (b * nq + i, g)),
                   pl.BlockSpec((sq, sq, CMP_TQ, LANES), lambda b, g, i: (b, g, i, 0))],
        out_shape=[jax.ShapeDtypeStruct((m, NSA_W), jnp.float32),
                   jax.ShapeDtypeStruct((batch, NSA_KV_GROUPS, seq, LANES), jnp.bfloat16)],
        scratch_shapes=[pltpu.VMEM((LANES, CMP_TQ), jnp.float32)],
        compiler_params=_cparams(("parallel", "parallel", "parallel")),
        name="nsa_cmp_topk",
    )(nsa_q, kc, vc, overlap.T, tri)


def _init_state(m_ref, acc_ref):
    m_ref[...] = jnp.full(m_ref.shape, NEG_INF, jnp.float32)
    acc_ref[...] = jnp.zeros(acc_ref.shape, jnp.float32)


def _key_max(st):
    rows = st.shape[0]
    while rows > 8:
        rows //= 2
        st = jnp.maximum(st[:rows], st[rows:])
    return jnp.max(st, axis=0, keepdims=True)


def _online_step_t(st, vt, m_ref, acc_ref, idx, off=None, lo=None):
    if off is not None:
        st = jnp.where(_key_minus_query(*st.shape) <= off, st, NEG_INF)
    if lo is not None:
        st = jnp.where(_key_minus_query(*st.shape) >= lo, st, NEG_INF)
    m_old = m_ref[idx]
    m_new = jnp.maximum(m_old, _key_max(st))
    alpha = jnp.exp2(m_old - m_new)
    pt = jnp.exp2(st - m_new).astype(jnp.bfloat16)
    acc_ref[idx] = alpha * acc_ref[idx] + _dot(vt, pt)
    m_ref[idx] = m_new


def _normalized_t(acc):
    return acc[:HEAD_DIM, :] / acc[HEAD_DIM:, :]


def _key_minus_query(keys, queries):
    return (lax.broadcasted_iota(jnp.int32, (keys, queries), 0)
            - lax.broadcasted_iota(jnp.int32, (keys, queries), 1))


def _pipelined_sweep(qi, parts, qk, softmax_pv, s0_ref, s1_ref, before_tail=None):
    assert ATT_TQ == 2 * ATT_TK
    late = slice(ATT_TK, ATT_TQ)

    def step(kt_next, s_next, kt, s_cur, off, cols=None):
        for part in range(parts):
            qk(kt_next, s_next, part, cols)
            softmax_pv(kt, s_cur, off, part, None)

    for part in range(parts):
        qk(0, s0_ref, part, None)

    def pair(kt):
        step(kt + 1, s1_ref, kt, s0_ref, None)
        step(kt + 2, s0_ref, kt + 1, s1_ref, None)

    def two_pairs(i, carry):
        pair(4 * i)
        pair(4 * i + 2)
        return carry

    def one_pair(i, carry):
        pair(2 * i)
        return carry

    lax.fori_loop(0, qi // 2, two_pairs, 0)
    lax.fori_loop(2 * (qi // 2), qi, one_pair, 0)
    kt = 2 * qi
    for part in range(parts):
        qk(kt + 1, s1_ref, part, late)
    if before_tail is not None:
        before_tail()
    for part in range(parts):
        softmax_pv(kt, s0_ref, 0, part, None)
    for part in range(parts):
        softmax_pv(kt + 1, s1_ref, 0, part, late)


def _at(part, cols):
    return part if cols is None else (part, slice(None), cols)


def _pair_rows(acc_ref):
    return jnp.concatenate([_normalized_t(acc_ref[hh]) for hh in range(2)], axis=0).T


def _pick_col(x, col):
    lane = lax.broadcasted_iota(jnp.int32, x.shape, 1)
    return jnp.sum(jnp.where(lane == col, x, 0.0), axis=1, keepdims=True)


def _nsa_attn_kernel(q_ref, ksel_ref, vselt_ref, kwin_ref, vwint_ref, negsel_ref, oc_ref, gate_ref, z_ref,
                     o_ref, qcat_ref, s0_ref, s1_ref, w_ref, m_ref, acc_ref, mw_ref, accw_ref):
    tq, tk = ATT_TQ, ATT_TK
    g = pl.program_id(1)
    j = pl.program_id(2)
    qi = pl.program_id(3)
    q2 = q_ref[...]
    negsel = negsel_ref[...]
    for hh in range(2):
        qm = jnp.where(_half_mask(q2.shape, hh), q2, jnp.zeros_like(q2))
        qcat_ref[hh] = jnp.concatenate([qm, negsel], axis=1)
    _init_state(m_ref, acc_ref)

    def qk(kt, s_ref, hh, cols):
        start = pl.multiple_of(kt * tk, tk)
        q = qcat_ref[hh] if cols is None else qcat_ref[hh, cols, :]
        s_ref[_at(hh, cols)] = _dot_t(ksel_ref[pl.ds(start, tk), :], q)

    def softmax_pv(kt, s_ref, off, hh, cols):
        start = pl.multiple_of(kt * tk, tk)
        _online_step_t(s_ref[_at(hh, cols)], vselt_ref[:, pl.ds(start, tk)], m_ref, acc_ref, _at(hh, cols), off)

    halves = tq // tk

    def win_tiles(half):
        wt = qi * halves + half
        return (pl.multiple_of(wt * tk, tk), pl.multiple_of(jnp.maximum(wt - 1, 0) * tk, tk),
                jnp.where(wt > 0, 1, tk))

    def window_qk():
        for half in range(halves):
            cols = slice(half * tk, (half + 1) * tk)
            d0, p0, _ = win_tiles(half)
            for hh in range(2):
                qm = qcat_ref[hh, cols, :LANES]
                w_ref[hh, 0, :, cols] = _dot_t(kwin_ref[pl.ds(d0, tk), :], qm)
                w_ref[hh, 1, :, cols] = _dot_t(kwin_ref[pl.ds(p0, tk), :], qm)

    _pipelined_sweep(qi, 2, qk, softmax_pv, s0_ref, s1_ref, before_tail=window_qk)
    _init_state(mw_ref, accw_ref)
    for half in range(halves):
        cols = slice(half * tk, (half + 1) * tk)
        d0, p0, prev_lo = win_tiles(half)
        for hh in range(2):
            _online_step_t(w_ref[hh, 0, :, cols], vwint_ref[:, pl.ds(d0, tk)], mw_ref, accw_ref, _at(hh, cols),
                           off=0)
            _online_step_t(w_ref[hh, 1, :, cols], vwint_ref[:, pl.ds(p0, tk)], mw_ref, accw_ref, _at(hh, cols),
                           lo=prev_lo)
    o_sel = _pair_rows(acc_ref)
    o_win = _pair_rows(accw_ref)

    gates = gate_ref[...]
    head0 = (g * NSA_HPG + j * 2) * 3
    low = lax.broadcasted_iota(jnp.int32, (tq, LANES), 1) < HEAD_DIM
    mix = [jnp.where(low, _pick_col(gates, head0 + i), _pick_col(gates, head0 + 3 + i)) for i in range(3)]
    out = mix[0] * oc_ref[...] + mix[1] * o_sel + mix[2] * o_win
    o_ref[...] = (out * z_ref[...]).astype(o_ref.dtype)


def _nsa_attn(p, negsel, o_cmp, batch, seq):
    m = p['nsa_q'].shape[0]
    t = ATT_TQ
    nq = seq // t
    sq = None
    row_blk = lambda b, g, j, i: (b * nq + i, g * 2 + j)
    k_spec = pl.BlockSpec((seq, LANES), lambda b, g, j, i: (b, g))
    kcat_spec = pl.BlockSpec((seq, 2 * LANES), lambda b, g, j, i: (b, g))
    vt_spec = pl.BlockSpec((sq, LANES, seq), lambda b, g, j, i: (b, g, 0))
    return pl.pallas_call(
        _nsa_attn_kernel,
        grid=(batch, NSA_KV_GROUPS, NSA_HPG // 2, nq),
        in_specs=[pl.BlockSpec((t, LANES), row_blk),
                  kcat_spec, vt_spec, k_spec, vt_spec,
                  pl.BlockSpec((sq, sq, t, LANES), lambda b, g, j, i: (b, g, i, 0)),
                  pl.BlockSpec((t, LANES), row_blk),
                  pl.BlockSpec((t, LANES), lambda b, g, j, i: (b * nq + i, 0)),
                  pl.BlockSpec((t, LANES), row_blk)],
        out_specs=pl.BlockSpec((t, LANES), row_blk),
        out_shape=jax.ShapeDtypeStruct((m, NSA_W), jnp.bfloat16),
        scratch_shapes=[pltpu.VMEM((2, t, 2 * LANES), jnp.bfloat16),
                        pltpu.VMEM((2, ATT_TK, t), jnp.float32),
                        pltpu.VMEM((2, ATT_TK, t), jnp.float32),
                        pltpu.VMEM((2, 2, ATT_TK, t), jnp.float32),
                        pltpu.VMEM((2, 1, t), jnp.float32),
                        pltpu.VMEM((2, LANES, t), jnp.float32),
                        pltpu.VMEM((2, 1, t), jnp.float32),
                        pltpu.VMEM((2, LANES, t), jnp.float32)],
        compiler_params=_cparams(("parallel", "parallel", "parallel", "arbitrary")),
        name="nsa_sel_win_attn",
    )(p['nsa_q'], p['nsa_k_sel'], p['nsa_v_sel_t'], p['nsa_k_win'], p['nsa_v_win_t'], negsel, o_cmp,
      p['nsa_gate'], p['nsa_z'])


def _fox_kernel(q_ref, k_ref, vt_ref, z_ref, o_ref, s0_ref, s1_ref, m_ref, acc_ref):
    tk = ATT_TK
    qi = pl.program_id(2)
    _init_state(m_ref, acc_ref)

    def qk(kt, s_ref, hh, cols):
        start = pl.multiple_of(kt * tk, tk)
        cat = slice(hh * FOX_CAT, (hh + 1) * FOX_CAT)
        q = q_ref[:, cat] if cols is None else q_ref[cols, cat]
        s_ref[_at(hh, cols)] = _dot_t(k_ref[pl.ds(start, tk), cat], q)

    def softmax_pv(kt, s_ref, off, hh, cols):
        start = pl.multiple_of(kt * tk, tk)
        _online_step_t(s_ref[_at(hh, cols)], vt_ref[hh * LANES:(hh + 1) * LANES, pl.ds(start, tk)],
                       m_ref, acc_ref, _at(hh, cols), off)

    _pipelined_sweep(qi, 2, qk, softmax_pv, s0_ref, s1_ref)
    o_ref[...] = (_pair_rows(acc_ref) * z_ref[...]).astype(o_ref.dtype)


def _fox_attn(p, qcat, kcat, batch, seq):
    m = qcat.shape[0]
    t = ATT_TQ
    nq = seq // t
    row_blk = lambda b, j, i: (b * nq + i, j)
    return pl.pallas_call(
        _fox_kernel,
        grid=(batch, FOX_HEADS // 2, nq),
        in_specs=[pl.BlockSpec((t, 2 * FOX_CAT), row_blk),
                  pl.BlockSpec((seq, 2 * FOX_CAT), lambda b, j, i: (b, j)),
                  pl.BlockSpec((None, 2 * LANES, seq), lambda b, j, i: (b, j, 0)),
                  pl.BlockSpec((t, LANES), row_blk)],
        out_specs=pl.BlockSpec((t, LANES), row_blk),
        out_shape=jax.ShapeDtypeStruct((m, FOX_W), jnp.bfloat16),
        scratch_shapes=[pltpu.VMEM((2, ATT_TK, t), jnp.float32),
                        pltpu.VMEM((2, ATT_TK, t), jnp.float32),
                        pltpu.VMEM((2, 1, t), jnp.float32),
                        pltpu.VMEM((2, LANES, t), jnp.float32)],
        compiler_params=_cparams(("parallel", "parallel", "arbitrary")),
        name="fox_attn",
    )(qcat, kcat, p['fox_vt'], p['fox_z'])


def _diff_kernel(lam_init, q_ref, k_ref, vt_ref, z_ref, lam_ref, g_ref, o_ref, qm_ref, s0_ref, s1_ref,
                 m_ref, acc_ref):
    tq, tk = ATT_TQ, ATT_TK
    qi = pl.program_id(2)
    q2 = q_ref[...]
    lane = lax.broadcasted_iota(jnp.int32, (tq, LANES), 1)
    for idx in range(4):
        lo = idx * DIFF_QK_DIM
        qm_ref[idx] = jnp.where((lane >= lo) & (lane < lo + DIFF_QK_DIM), q2, jnp.zeros_like(q2))
    _init_state(m_ref, acc_ref)

    def qk(kt, s_ref, idx, cols):
        start = pl.multiple_of(kt * tk, tk)
        q = qm_ref[idx] if cols is None else qm_ref[idx, cols, :]
        s_ref[_at(idx, cols)] = _dot_t(k_ref[pl.ds(start, tk), :], q)

    def softmax_pv(kt, s_ref, off, idx, cols):
        start = pl.multiple_of(kt * tk, tk)
        hh = idx // 2
        _online_step_t(s_ref[_at(idx, cols)], vt_ref[hh * LANES:(hh + 1) * LANES, pl.ds(start, tk)],
                       m_ref, acc_ref, _at(idx, cols), off)

    _pipelined_sweep(qi, 4, qk, softmax_pv, s0_ref, s1_ref)

    lam_rows = lam_ref[...]
    lam = (jnp.exp(jnp.sum(lam_rows[0:1, :] * lam_rows[1:2, :], axis=1, keepdims=True))
           - jnp.exp(jnp.sum(lam_rows[2:3, :] * lam_rows[3:4, :], axis=1, keepdims=True)) + lam_init)
    o = [_normalized_t(acc_ref[idx]) for idx in range(4)]
    normed = []
    for hh in range(2):
        d = o[2 * hh] - lam * o[2 * hh + 1]
        ms = jnp.sum(d * d, axis=0, keepdims=True) / HEAD_DIM
        normed.append(d * lax.rsqrt(ms + RMS_EPS))
    out = jnp.concatenate(normed, axis=0).T * g_ref[...] * (1.0 - lam_init)
    o_ref[...] = (out * z_ref[...]).astype(o_ref.dtype)


def _diff_attn(p, lam_rows, g_row, lam_init, batch, seq):
    m = p['diff_q'].shape[0]
    t = ATT_TQ
    nq = seq // t
    row_blk = lambda b, j, i: (b * nq + i, j)
    return pl.pallas_call(
        partial(_diff_kernel, lam_init),
        grid=(batch, DIFF_HEADS // 2, nq),
        in_specs=[pl.BlockSpec((t, LANES), row_blk),
                  pl.BlockSpec((seq, LANES), lambda b, j, i: (b, j)),
                  pl.BlockSpec((None, 2 * LANES, seq), lambda b, j, i: (b, j, 0)),
                  pl.BlockSpec((t, LANES), row_blk),
                  pl.BlockSpec((8, LANES), lambda b, j, i: (0, 0)),
                  pl.BlockSpec((1, LANES), lambda b, j, i: (0, 0))],
        out_specs=pl.BlockSpec((t, LANES), row_blk),
        out_shape=jax.ShapeDtypeStruct((m, DIFF_W), jnp.bfloat16),
        scratch_shapes=[pltpu.VMEM((4, t, LANES), jnp.bfloat16),
                        pltpu.VMEM((4, ATT_TK, t), jnp.float32),
                        pltpu.VMEM((4, ATT_TK, t), jnp.float32),
                        pltpu.VMEM((4, 1, t), jnp.float32),
                        pltpu.VMEM((4, LANES, t), jnp.float32)],
        compiler_params=_cparams(("parallel", "parallel", "arbitrary")),
        name="diff_attn",
    )(p['diff_q'], p['diff_k'], p['diff_vt'], p['diff_z'], lam_rows, g_row)


def _out_kernel(of_ref, on_ref, od_ref, w_ref, x_ref, g_ref, b_ref, o_ref):
    y = (DEEPNORM_ALPHA * x_ref[...]
         + _dot(of_ref[...], w_ref[0:FOX_W, :])
         + _dot(on_ref[...], w_ref[FOX_W:FOX_W + NSA_W, :])
         + _dot(od_ref[...], w_ref[FOX_W + NSA_W:, :]))
    mu = jnp.mean(y, axis=1, keepdims=True)
    yc = y - mu
    var = jnp.mean(yc * yc, axis=1, keepdims=True)
    o_ref[...] = yc * lax.rsqrt(var + LN_EPS) * g_ref[...] + b_ref[...]


def _out_proj(o_fox, o_nsa, o_diff, w_out, x2d, g_row, b_row):
    m = x2d.shape[0]
    tm = OUT_TM
    row = lambda i: (i, 0)
    const = lambda i: (0, 0)
    return pl.pallas_call(
        _out_kernel,
        grid=(m // tm,),
        in_specs=[pl.BlockSpec((tm, FOX_W), row), pl.BlockSpec((tm, NSA_W), row),
                  pl.BlockSpec((tm, DIFF_W), row),
                  pl.BlockSpec((D_MODEL, D_MODEL), const),
                  pl.BlockSpec((tm, D_MODEL), row),
                  pl.BlockSpec((1, D_MODEL), const), pl.BlockSpec((1, D_MODEL), const)],
        out_specs=pl.BlockSpec((tm, D_MODEL), row),
        out_shape=jax.ShapeDtypeStruct((m, D_MODEL), jnp.float32),
        compiler_params=_cparams(("parallel",)),
        name="out_proj_ln",
    )(o_fox, o_nsa, o_diff, w_out, x2d, g_row, b_row)


def _overlap_matrix(seq):
    n_slc = seq // SEL_BLOCK
    cmp_start = CMP_STRIDE * np.arange(N_CMP_PAD)
    sel_start = SEL_BLOCK * np.arange(n_slc)
    ov = np.clip(np.minimum(cmp_start[:, None] + CMP_LEN, sel_start[None, :] + SEL_BLOCK)
                 - np.maximum(cmp_start[:, None], sel_start[None, :]), 0, None).astype(np.float32) / CMP_LEN
    return ov


def kernel(x, w_in, b_fox_f, cmp_pos_k, cmp_pos_v, cmp_w1_k, cmp_w2_k, cmp_w1_v, cmp_w2_v,
           lam_q1, lam_k1, lam_q2, lam_k2, diff_subln_g, w_out, ln_g, ln_b):
    batch, seq, _ = x.shape
    assert seq % ATT_TQ == 0 and seq // SEL_BLOCK == LANES and (seq - CMP_LEN) // CMP_STRIDE + 1 < N_CMP_PAD
    assert WINDOW == ATT_TK and ATT_TQ % ATT_TK == 0
    m = batch * seq
    tables = _rope_tables(seq, HEAD_DIM) + _rope_tables(seq, DIFF_QK_DIM)
    overlap = jnp.asarray(_overlap_matrix(seq), jnp.bfloat16)

    x2d = x.reshape(m, D_MODEL)
    for l in range(DEPTH):
        w_perm = _permute_columns(w_in[l], PROJ_COLS).astype(jnp.bfloat16)
        wt_perm = _permute_columns(w_in[l], VT_COLS).astype(jnp.bfloat16).T
        bias_row = jnp.zeros((1, LANES), jnp.float32).at[0, :FOX_HEADS].set(b_fox_f[l])
        p = _project(x2d, w_perm, wt_perm, bias_row, tables, seq)

        o_fox = _fox_attn(p, p['fox_qcat'], p['fox_kcat'], batch, seq)

        kc = _compress(p['nsa_k_cmp'], cmp_pos_k[l], cmp_w1_k[l], cmp_w2_k[l], batch, seq, False)
        vc = _compress(p['nsa_v_cmp'], cmp_pos_v[l], cmp_w1_v[l], cmp_w2_v[l], batch, seq, True)
        o_cmp, negsel = _cmp_topk(p['nsa_q'], kc, vc, overlap, batch, seq)
        o_nsa = _nsa_attn(p, negsel, o_cmp, batch, seq)

        lam_init = 0.8 - 0.6 * math.exp(-0.3 * l)
        lam_rows = jnp.zeros((8, LANES), jnp.float32)
        for r, v in enumerate((lam_q1[l], lam_k1[l], lam_q2[l], lam_k2[l])):
            lam_rows = lam_rows.at[r, :DIFF_QK_DIM].set(v.astype(jnp.float32))
        g_row = jnp.tile(diff_subln_g[l].reshape(1, HEAD_DIM), (1, 2))
        o_diff = _diff_attn(p, lam_rows, g_row, lam_init, batch, seq)

        x2d = _out_proj(o_fox, o_nsa, o_diff, w_out[l].astype(jnp.bfloat16), x2d,
                        ln_g[l].reshape(1, D_MODEL), ln_b[l].reshape(1, D_MODEL))
    return x2d.reshape(batch, seq, D_MODEL)
```

```python
import math
from functools import partial

import numpy as np
import jax
import jax.numpy as jnp
from jax import lax
from jax.experimental import pallas as pl
from jax.experimental.pallas import tpu as pltpu

D_MODEL = 1024
DEPTH = 2
HEAD_DIM = 64
FOX_HEADS = 4
NSA_HEADS = 8
DIFF_HEADS = 4
FOX_W = FOX_HEADS * HEAD_DIM
NSA_W = NSA_HEADS * HEAD_DIM
DIFF_QK_DIM = HEAD_DIM // 2
DIFF_W = DIFF_HEADS * HEAD_DIM
NSA_KV_GROUPS = 2
NSA_HPG = NSA_HEADS // NSA_KV_GROUPS
NSA_KV_W = NSA_KV_GROUPS * HEAD_DIM
CMP_LEN = 32
CMP_STRIDE = 16
CMP_HIDDEN = 2 * HEAD_DIM
SEL_BLOCK = 64
SEL_SHIFT = SEL_BLOCK.bit_length() - 1
SEL_TOPK = 16
WINDOW = 512
ROPE_THETA = 10000.0
LN_EPS = 1e-5
RMS_EPS = 1e-5
NEG_INF = -1e30
FORCE_SCORE = 1e30
DEEPNORM_ALPHA = (2 * DEPTH) ** 0.25
LOG2E = math.log2(math.e)

SPLITS = (
    ('fox_q', FOX_W), ('fox_k', FOX_W), ('fox_v', FOX_W), ('fox_f', FOX_HEADS), ('fox_z', FOX_W),
    ('nsa_q', NSA_W),
    ('nsa_k_cmp', NSA_KV_W), ('nsa_v_cmp', NSA_KV_W),
    ('nsa_k_sel', NSA_KV_W), ('nsa_v_sel', NSA_KV_W),
    ('nsa_k_win', NSA_KV_W), ('nsa_v_win', NSA_KV_W),
    ('nsa_gate', 3 * NSA_HEADS), ('nsa_z', NSA_W),
    ('diff_q', DIFF_W), ('diff_k', DIFF_W), ('diff_v', DIFF_W), ('diff_z', DIFF_W),
)
_OFF = {}
_acc = 0
for _n, _w in SPLITS:
    _OFF[_n] = _acc
    _acc += _w
IN_WIDTH = _acc

LANES = 128
VMEM_LIMIT = 56 * 1024 * 1024

PROJ_TM = 512
OUT_TM = 1024
CMP_TQ = 512
ATT_TQ = 1024
ATT_TK = 512
N_CMP_PAD = 512
FOX_CAT = 2 * LANES


def _cparams(sem):
    return pltpu.CompilerParams(dimension_semantics=sem, vmem_limit_bytes=VMEM_LIMIT)


def _dot(a, b):
    return jnp.dot(a, b, preferred_element_type=jnp.float32)


def _dot_t(a, b):
    return lax.dot_general(a, b, (((1,), (1,)), ((), ())), preferred_element_type=jnp.float32)


def _split3(x):
    hi = x.astype(jnp.bfloat16)
    r1 = x - hi.astype(jnp.float32)
    mid = r1.astype(jnp.bfloat16)
    lo = (r1 - mid.astype(jnp.float32)).astype(jnp.bfloat16)
    return hi, mid, lo


def _dup(name):
    base = _OFF[name]
    cols = []
    for g in range(NSA_KV_GROUPS):
        one = list(range(base + g * HEAD_DIM, base + (g + 1) * HEAD_DIM))
        cols += one + one
    return cols


def _pad(cols, width):
    return list(cols) + [-1] * (width - len(cols))


def _rng(name, width):
    return list(range(_OFF[name], _OFF[name] + width))


Q_SCALE = HEAD_DIM ** -0.5 * LOG2E
DIFF_Q_SCALE = DIFF_QK_DIM ** -0.5 * LOG2E

PROJ_SEGS = (
    ('fox_q', _rng('fox_q', FOX_W), 'none', jnp.bfloat16, Q_SCALE),
    ('fox_k', _rng('fox_k', FOX_W), 'none', jnp.bfloat16, None),
    ('fox_z', _rng('fox_z', FOX_W), 'silu', jnp.float32, None),
    ('fox_logf', _pad(_rng('fox_f', FOX_HEADS), LANES), 'logsig', jnp.float32, None),
    ('nsa_q', _rng('nsa_q', NSA_W), 'rope64', jnp.bfloat16, Q_SCALE),
    ('nsa_k_cmp', _rng('nsa_k_cmp', NSA_KV_W), 'rope64', jnp.float32, None),
    ('nsa_v_cmp', _rng('nsa_v_cmp', NSA_KV_W), 'none', jnp.float32, None),
    ('nsa_k_sel', _dup('nsa_k_sel'), 'rope64_blockhot', jnp.bfloat16, None),
    ('nsa_k_win', _dup('nsa_k_win'), 'rope64', jnp.bfloat16, None),
    ('nsa_gate', _pad(_rng('nsa_gate', 3 * NSA_HEADS), LANES), 'sigmoid', jnp.float32, None),
    ('nsa_z', _rng('nsa_z', NSA_W), 'silu', jnp.float32, None),
    ('diff_q', _rng('diff_q', DIFF_W), 'rope32', jnp.bfloat16, DIFF_Q_SCALE),
    ('diff_k', _rng('diff_k', DIFF_W), 'rope32', jnp.bfloat16, None),
    ('diff_z', _rng('diff_z', DIFF_W), 'silu', jnp.float32, None),
)
PROJ_COLS = np.concatenate([np.asarray(seg[1], np.int32) for seg in PROJ_SEGS])
PROJ_N = int(PROJ_COLS.shape[0])

VT_SEGS = (
    ('fox_vt', _rng('fox_v', FOX_W)),
    ('nsa_v_sel_t', _rng('nsa_v_sel', NSA_KV_W)),
    ('nsa_v_win_t', _rng('nsa_v_win', NSA_KV_W)),
    ('diff_vt', _rng('diff_v', DIFF_W)),
)
VT_COLS = np.concatenate([np.asarray(seg[1], np.int32) for seg in VT_SEGS])
VT_N = int(VT_COLS.shape[0])


def _permute_columns(w, col_ids):
    pieces, start = [], 0
    cols = col_ids.tolist()
    while start < len(cols):
        end = start + 1
        if cols[start] < 0:
            while end < len(cols) and cols[end] < 0:
                end += 1
            pieces.append(jnp.zeros((w.shape[0], end - start), w.dtype))
        else:
            while end < len(cols) and cols[end] == cols[end - 1] + 1:
                end += 1
            pieces.append(w[:, cols[start]:cols[end - 1] + 1])
        start = end
    return jnp.concatenate(pieces, axis=1)


def _rope_chunk(y, cos, sin, half):
    lane = lax.broadcasted_iota(jnp.int32, y.shape, 1)
    first = (lane & (2 * half - 1)) < half
    partner = jnp.where(first, pltpu.roll(y, LANES - half, 1), pltpu.roll(y, half, 1))
    return y * cos + partner * sin


PROJ_HELD = ('fox_q', 'fox_k', 'fox_logf')
PROJ_STORED = tuple(seg for seg in PROJ_SEGS if seg[0] not in PROJ_HELD)


def _out_width(seg):
    return 2 * len(seg[1]) if seg[2] == 'rope64_blockhot' else len(seg[1])


def _proj_kernel(seq, x_ref, w_ref, wt_ref, bias_ref, cos64_ref, sin64_ref, cos32_ref, sin32_ref, *refs):
    stored = dict(zip((seg[0] for seg in PROJ_STORED), refs))
    vt_refs = refs[len(PROJ_STORED):len(PROJ_STORED) + len(VT_SEGS)]
    qcat_ref, kcat_ref, carry_ref = refs[len(PROJ_STORED) + len(VT_SEGS):]
    xb = x_ref[...].astype(jnp.bfloat16)
    held = {}
    start = 0
    for name, cols, epi, dtype, scale in PROJ_SEGS:
        width = len(cols)
        y = _dot(xb, w_ref[:, start:start + width])
        start += width
        if epi == 'silu':
            y = y * jax.nn.sigmoid(y)
        elif epi == 'sigmoid':
            y = jax.nn.sigmoid(y)
        elif epi == 'logsig':
            y = y + bias_ref[...]
            y = jnp.minimum(y, 0.0) - jnp.log1p(jnp.exp(-jnp.abs(y)))
        elif epi in ('rope64', 'rope32', 'rope64_blockhot'):
            half = DIFF_QK_DIM // 2 if epi == 'rope32' else HEAD_DIM // 2
            cos = cos32_ref[...] if epi == 'rope32' else cos64_ref[...]
            sin = sin32_ref[...] if epi == 'rope32' else sin64_ref[...]
            chunks = [_rope_chunk(y[:, c * LANES:(c + 1) * LANES], cos, sin, half)
                      for c in range(width // LANES)]
            if epi == 'rope64_blockhot':
                tm = y.shape[0]
                pos = (pl.program_id(0) % (seq // tm)) * tm + lax.broadcasted_iota(jnp.int32, (tm, LANES), 0)
                lane = lax.broadcasted_iota(jnp.int32, (tm, LANES), 1)
                hot = jnp.where(lane == jnp.right_shift(pos, SEL_SHIFT), 1.0, 0.0)
                chunks = [piece for chunk in chunks for piece in (chunk, hot)]
            y = chunks[0] if len(chunks) == 1 else jnp.concatenate(chunks, axis=1)
        if scale is not None:
            y = y * scale
        if name in PROJ_HELD:
            held[name] = y.astype(dtype)
        else:
            stored[name][...] = y.astype(dtype)
    start = 0
    for (name, cols), o_ref in zip(VT_SEGS, vt_refs):
        rows = len(cols)
        yt = _dot_t(wt_ref[start:start + rows, :], xb).astype(o_ref.dtype)
        start += rows
        ones = jnp.ones((HEAD_DIM, yt.shape[1]), o_ref.dtype)
        for h in range(rows // HEAD_DIM):
            o_ref[2 * h * HEAD_DIM:(2 * h + 1) * HEAD_DIM, :] = yt[h * HEAD_DIM:(h + 1) * HEAD_DIM, :]
            o_ref[(2 * h + 1) * HEAD_DIM:(2 * h + 2) * HEAD_DIM, :] = ones

    @pl.when(pl.program_id(0) % (seq // x_ref.shape[0]) == 0)
    def _():
        carry_ref[...] = jnp.zeros_like(carry_ref)

    _fox_operands(held['fox_logf'], held['fox_q'], held['fox_k'], carry_ref, qcat_ref, kcat_ref)


def _rope_tables(seq, group):
    half = group // 2
    inv = ROPE_THETA ** (-(jnp.arange(half, dtype=jnp.float32) * 2.0 / group))
    ang = jnp.arange(seq, dtype=jnp.float32)[:, None] * inv[None, :]
    cos, sin = jnp.cos(ang), jnp.sin(ang)
    reps = LANES // group
    cos_t = jnp.tile(jnp.concatenate([cos, cos], axis=1), (1, reps))
    sin_t = jnp.tile(jnp.concatenate([-sin, sin], axis=1), (1, reps))
    return cos_t, sin_t


def _project(x2d, w_perm, wt_perm, bias_row, tables, seq):
    m = x2d.shape[0]
    nt = seq // PROJ_TM
    tab_spec = pl.BlockSpec((PROJ_TM, LANES), lambda i: (i % nt, 0))
    out_shape = [jax.ShapeDtypeStruct((m, _out_width(seg)), seg[3]) for seg in PROJ_STORED]
    out_specs = [pl.BlockSpec((PROJ_TM, _out_width(seg)), lambda i: (i, 0)) for seg in PROJ_STORED]
    out_shape += [jax.ShapeDtypeStruct((m // seq, 2 * len(seg[1]), seq), jnp.bfloat16) for seg in VT_SEGS]
    out_specs += [pl.BlockSpec((None, 2 * len(seg[1]), PROJ_TM), lambda i: (i // nt, 0, i % nt))
                  for seg in VT_SEGS]
    out_shape += [jax.ShapeDtypeStruct((m, FOX_HEADS * FOX_CAT), jnp.bfloat16)] * 2
    out_specs += [pl.BlockSpec((PROJ_TM, FOX_HEADS * FOX_CAT), lambda i: (i, 0))] * 2
    names = [seg[0] for seg in PROJ_STORED + VT_SEGS] + ['fox_qcat', 'fox_kcat']
    outs = pl.pallas_call(
        partial(_proj_kernel, seq),
        grid=(m // PROJ_TM,),
        in_specs=[pl.BlockSpec((PROJ_TM, D_MODEL), lambda i: (i, 0)),
                  pl.BlockSpec((D_MODEL, PROJ_N), lambda i: (0, 0)),
                  pl.BlockSpec((VT_N, D_MODEL), lambda i: (0, 0)),
                  pl.BlockSpec((1, LANES), lambda i: (0, 0)),
                  tab_spec, tab_spec, tab_spec, tab_spec],
        out_specs=out_specs,
        out_shape=out_shape,
        scratch_shapes=[pltpu.VMEM((8, LANES), jnp.float32)],
        compiler_params=_cparams(("arbitrary",)),
        name="in_proj",
    )(x2d, w_perm, wt_perm, bias_row, *tables)
    return dict(zip(names, outs))


def _fox_operands(lf, q, k, carry_ref, qcat_ref, kcat_ref):
    t = lf.shape[0]
    row = lax.broadcasted_iota(jnp.int32, (t, t), 0)
    col = lax.broadcasted_iota(jnp.int32, (t, t), 1)
    tri = jnp.where(col <= row, 1.0, 0.0).astype(jnp.bfloat16)
    n_terms = 3
    lane1 = lax.broadcasted_iota(jnp.int32, (t, LANES), 1)

    def packed_terms(x):
        terms = [term.astype(jnp.float32) for term in _split3(x)]
        packed = jnp.where(lane1 < FOX_HEADS, terms[0],
                           jnp.where(lane1 < 2 * FOX_HEADS, pltpu.roll(terms[1], FOX_HEADS, 1),
                                     pltpu.roll(terms[2], 2 * FOX_HEADS, 1)))
        return packed.astype(jnp.bfloat16)

    parts = _dot(tri, packed_terms(lf))
    c = (parts + pltpu.roll(parts, LANES - FOX_HEADS, 1) + pltpu.roll(parts, LANES - 2 * FOX_HEADS, 1)
         + carry_ref[0:1, :])
    carry_ref[0:1, :] = c[t - 1:t, :]

    src = lax.broadcasted_iota(jnp.int32, (LANES, 2 * FOX_HEADS * LANES), 0)
    dst = lax.broadcasted_iota(jnp.int32, (LANES, 2 * FOX_HEADS * LANES), 1)
    assert FOX_HEADS == 4
    term, head = jnp.right_shift(src, 2), src & 3
    live = src < n_terms * FOX_HEADS
    place = (jnp.where(live & (dst == head * LANES + term), 1.0, 0.0)
             - jnp.where(live & (dst == (FOX_HEADS + head) * LANES + n_terms + term), 1.0, 0.0))
    placed = _dot(packed_terms(c * LOG2E), place.astype(jnp.bfloat16))
    lane = lax.broadcasted_iota(jnp.int32, (t, FOX_HEADS * LANES), 1) & (LANES - 1)
    qa = jnp.where((lane >= n_terms) & (lane < 2 * n_terms), 1.0,
                   placed[:, :FOX_HEADS * LANES]).astype(jnp.bfloat16)
    ka = jnp.where(lane < n_terms, 1.0, placed[:, FOX_HEADS * LANES:]).astype(jnp.bfloat16)
    for h in range(FOX_HEADS):
        pair, hh = divmod(h, 2)
        q2 = q[:, pair * LANES:(pair + 1) * LANES]
        base = h * FOX_CAT
        qcat_ref[:, base:base + LANES] = jnp.where(_half_mask(q2.shape, hh), q2, jnp.zeros_like(q2))
        qcat_ref[:, base + LANES:base + FOX_CAT] = qa[:, h * LANES:(h + 1) * LANES]
        kcat_ref[:, base:base + LANES] = k[:, pair * LANES:(pair + 1) * LANES]
        kcat_ref[:, base + LANES:base + FOX_CAT] = ka[:, h * LANES:(h + 1) * LANES]


def _compress_kernel(ones_half, ch_ref, pos_ref, w1_ref, w2_ref, o_ref):
    n_chunks = ch_ref.shape[0] // CMP_STRIDE
    ch = jnp.concatenate([ch_ref[pl.ds(i, n_chunks, stride=CMP_STRIDE), :] for i in range(CMP_STRIDE)], axis=1)
    a_in = (ch + pos_ref[0:1, :]).astype(jnp.bfloat16)
    b_in = (ch + pos_ref[1:2, :]).astype(jnp.bfloat16)
    row = lax.broadcasted_iota(jnp.int32, (n_chunks, LANES), 0)
    for g in range(NSA_KV_GROUPS):
        a = _dot(a_in, w1_ref[g, 0])
        b = _dot(b_in, w1_ref[g, 1])
        h = a + pltpu.roll(b, n_chunks - 1, 0)
        act = (h * jax.nn.sigmoid(h)).astype(jnp.bfloat16)
        out = jnp.where(row < n_chunks - 1, _dot(act, w2_ref[...]), 0.0)
        if ones_half:
            out_t = out.T
            row_t = lax.broadcasted_iota(jnp.int32, out_t.shape, 0)
            o_ref[g] = jnp.where(row_t >= HEAD_DIM, 1.0, out_t).astype(o_ref.dtype)
        else:
            o_ref[g] = out.astype(o_ref.dtype)


def _compress(tok, pos_emb, w1, w2, batch, seq, ones_half):
    n_chunks = seq // CMP_STRIDE
    width = CMP_STRIDE * NSA_KV_W
    halves = w1.reshape(2, CMP_STRIDE, HEAD_DIM, CMP_HIDDEN)
    w1s = jnp.zeros((NSA_KV_GROUPS, 2, CMP_STRIDE, NSA_KV_GROUPS, HEAD_DIM, CMP_HIDDEN), w1.dtype)
    for g in range(NSA_KV_GROUPS):
        w1s = w1s.at[g, :, :, g].set(halves)
    w1s = w1s.reshape(NSA_KV_GROUPS, 2, width, CMP_HIDDEN).astype(jnp.bfloat16)
    pos = jnp.broadcast_to(pos_emb.reshape(2, CMP_STRIDE, 1, HEAD_DIM),
                           (2, CMP_STRIDE, NSA_KV_GROUPS, HEAD_DIM)).reshape(2, width)
    w2d = jnp.concatenate([w2, jnp.zeros_like(w2) if ones_half else w2], axis=1).astype(jnp.bfloat16)
    sq = None
    out_dims = (LANES, n_chunks) if ones_half else (n_chunks, LANES)
    return pl.pallas_call(
        partial(_compress_kernel, ones_half),
        grid=(batch,),
        in_specs=[pl.BlockSpec((seq, NSA_KV_W), lambda b: (b, 0)),
                  pl.BlockSpec((2, width), lambda b: (0, 0)),
                  pl.BlockSpec((NSA_KV_GROUPS, 2, width, CMP_HIDDEN), lambda b: (0, 0, 0, 0)),
                  pl.BlockSpec((CMP_HIDDEN, LANES), lambda b: (0, 0))],
        out_specs=pl.BlockSpec((sq, NSA_KV_GROUPS) + out_dims, lambda b: (b, 0, 0, 0)),
        out_shape=jax.ShapeDtypeStruct((batch, NSA_KV_GROUPS) + out_dims, jnp.bfloat16),
        compiler_params=_cparams(("parallel",)),
        name="nsa_compress_v" if ones_half else "nsa_compress_k",
    )(tok, pos, w1s, w2d)


def _half_mask(shape, hh):
    lane = lax.broadcasted_iota(jnp.int32, shape, 1)
    return (lane >= hh * HEAD_DIM) & (lane < (hh + 1) * HEAD_DIM)


def _cmp_topk_kernel(q_ref, kc_ref, vct_ref, ovt_ref, tri_ref, oc_ref, negsel_ref, imp_ref):
    tq = q_ref.shape[0]
    step = pl.program_id(2)
    t = step * tq + lax.broadcasted_iota(jnp.int32, (1, tq), 1)
    any_visible = jnp.where(t >= CMP_LEN - 1, 1.0, 0.0)

    def attend(rows):
        n = lax.broadcasted_iota(jnp.int32, (rows, 1), 0)
        mask_c = (CMP_STRIDE * n + (CMP_LEN - 1)) <= t
        kc = kc_ref[0:rows, :]
        vct_ovt = jnp.concatenate([vct_ref[:, 0:rows], ovt_ref[:, 0:rows]], axis=0)
        def scores(head):
            j, hh = divmod(head, 2)
            q2 = q_ref[:, j * LANES:(j + 1) * LANES]
            return _dot_t(kc, jnp.where(_half_mask(q2.shape, hh), q2, jnp.zeros_like(q2)))

        imp = jnp.zeros((LANES, tq), jnp.float32)
        outs = []
        st_next = scores(0)
        for head in range(NSA_HPG):
            st = jnp.where(mask_c, st_next, NEG_INF)
            if head + 1 < NSA_HPG:
                st_next = scores(head + 1)
            e = jnp.exp2(st - _key_max(st)).astype(jnp.bfloat16)
            res = _dot(vct_ovt, e)
            inv = any_visible / res[HEAD_DIM:HEAD_DIM + 1, :]
            imp = imp + res[LANES:, :] * inv
            outs.append(res[:HEAD_DIM, :] * inv)
        for j in range(NSA_HPG // 2):
            oc_ref[:, j * LANES:(j + 1) * LANES] = jnp.concatenate(outs[2 * j:2 * j + 2], axis=0).T
        imp_ref[...] = imp

    steps_per_variant = (LANES * CMP_STRIDE) // tq
    for variant in range(N_CMP_PAD // LANES):
        @pl.when(step // steps_per_variant == variant)
        def _():
            attend((variant + 1) * LANES)

    blk = lax.broadcasted_iota(jnp.int32, (LANES, 1), 0)
    cur = jnp.right_shift(t, SEL_SHIFT)
    valid = blk * SEL_BLOCK <= t
    forced = (blk == 0) | (blk == cur) | (blk == cur - 1)
    score = jnp.where(valid, jnp.where(forced, FORCE_SCORE, imp_ref[...]), NEG_INF)
    tri = tri_ref[...]
    budget = jnp.full((1, tq), float(SEL_TOPK), jnp.float32)
    sel = jnp.zeros((LANES, tq), jnp.float32)
    for _ in range(SEL_TOPK):
        level = score == _key_max(score)
        score = jnp.where(level, -jnp.inf, score)
        rank = _dot(tri, jnp.where(level, 1.0, 0.0).astype(jnp.bfloat16))
        sel = jnp.where(level & (rank <= budget), 1.0, sel)
        budget = jnp.maximum(budget - rank[LANES - 1:LANES, :], 0.0)
    negsel_ref[...] = jnp.where(sel > 0.5, 0.0, NEG_INF).T.astype(negsel_ref.dtype)


def _cmp_topk(nsa_q, kc, vc, overlap, batch, seq):
    m = nsa_q.shape[0]
    nq = seq // CMP_TQ
    sq = None
    idx = np.arange(LANES)
    tri = jnp.asarray((idx[None, :] <= idx[:, None]).astype(np.float32), jnp.bfloat16)
    return pl.pallas_call(
        _cmp_topk_kernel,
        grid=(batch, NSA_KV_GROUPS, nq),
        in_specs=[pl.BlockSpec((CMP_TQ, NSA_HPG * HEAD_DIM), lambda b, g, i: (b * nq + i, g)),
                  pl.BlockSpec((sq, sq, N_CMP_PAD, LANES), lambda b, g, i: (b, g, 0, 0)),
                  pl.BlockSpec((sq, sq, LANES, N_CMP_PAD), lambda b, g, i: (b, g, 0, 0)),
                  pl.BlockSpec((LANES, N_CMP_PAD), lambda b, g, i: (0, 0)),
                  pl.BlockSpec((LANES, LANES), lambda b, g, i: (0, 0))],
        out_specs=[pl.BlockSpec((CMP_TQ, NSA_HPG * HEAD_DIM), lambda b, g, i: (b * nq + i, g)),
                   pl.BlockSpec((sq, sq, CMP_TQ, LANES), lambda b, g, i: (b, g, i, 0))],
        out_shape=[jax.ShapeDtypeStruct((m, NSA_W), jnp.float32),
                   jax.ShapeDtypeStruct((batch, NSA_KV_GROUPS, seq, LANES), jnp.bfloat16)],
        scratch_shapes=[pltpu.VMEM((LANES, CMP_TQ), jnp.float32)],
        compiler_params=_cparams(("parallel", "parallel", "parallel")),
        name="nsa_cmp_topk",
    )(nsa_q, kc, vc, overlap.T, tri)


def _init_state(m_ref, acc_ref):
    m_ref[...] = jnp.full(m_ref.shape, NEG_INF, jnp.float32)
    acc_ref[...] = jnp.zeros(acc_ref.shape, jnp.float32)


def _key_max(st):
    rows = st.shape[0]
    while rows > 8:
        rows //= 2
        st = jnp.maximum(st[:rows], st[rows:])
    return jnp.max(st, axis=0, keepdims=True)


def _online_step_t(st, vt, m_ref, acc_ref, idx, off=None, lo=None):
    if off is not None:
        st = jnp.where(_key_minus_query(*st.shape) <= off, st, NEG_INF)
    if lo is not None:
        st = jnp.where(_key_minus_query(*st.shape) >= lo, st, NEG_INF)
    m_old = m_ref[idx]
    m_new = jnp.maximum(m_old, _key_max(st))
    alpha = jnp.exp2(m_old - m_new)
    pt = jnp.exp2(st - m_new).astype(jnp.bfloat16)
    acc_ref[idx] = alpha * acc_ref[idx] + _dot(vt, pt)
    m_ref[idx] = m_new


def _normalized_t(acc):
    return acc[:HEAD_DIM, :] / acc[HEAD_DIM:, :]


def _key_minus_query(keys, queries):
    return (lax.broadcasted_iota(jnp.int32, (keys, queries), 0)
            - lax.broadcasted_iota(jnp.int32, (keys, queries), 1))


def _pipelined_sweep(qi, parts, qk, softmax_pv, s0_ref, s1_ref, before_tail=None):
    assert ATT_TQ == 2 * ATT_TK
    late = slice(ATT_TK, ATT_TQ)

    def step(kt_next, s_next, kt, s_cur, off, cols=None):
        for part in range(parts):
            qk(kt_next, s_next, part, cols)
            softmax_pv(kt, s_cur, off, part, None)

    for part in range(parts):
        qk(0, s0_ref, part, None)

    def pair(kt):
        step(kt + 1, s1_ref, kt, s0_ref, None)
        step(kt + 2, s0_ref, kt + 1, s1_ref, None)

    def two_pairs(i, carry):
        pair(4 * i)
        pair(4 * i + 2)
        return carry

    def one_pair(i, carry):
        pair(2 * i)
        return carry

    lax.fori_loop(0, qi // 2, two_pairs, 0)
    lax.fori_loop(2 * (qi // 2), qi, one_pair, 0)
    kt = 2 * qi
    for part in range(parts):
        qk(kt + 1, s1_ref, part, late)
    if before_tail is not None:
        before_tail()
    for part in range(parts):
        softmax_pv(kt, s0_ref, 0, part, None)
    for part in range(parts):
        softmax_pv(kt + 1, s1_ref, 0, part, late)


def _at(part, cols):
    return part if cols is None else (part, slice(None), cols)


def _pair_rows(acc_ref):
    return jnp.concatenate([_normalized_t(acc_ref[hh]) for hh in range(2)], axis=0).T


def _pick_col(x, col):
    lane = lax.broadcasted_iota(jnp.int32, x.shape, 1)
    return jnp.sum(jnp.where(lane == col, x, 0.0), axis=1, keepdims=True)


def _nsa_attn_kernel(q_ref, ksel_ref, vselt_ref, kwin_ref, vwint_ref, negsel_ref, oc_ref, gate_ref, z_ref,
                     o_ref, qcat_ref, s0_ref, s1_ref, w_ref, m_ref, acc_ref, mw_ref, accw_ref):
    tq, tk = ATT_TQ, ATT_TK
    g = pl.program_id(1)
    j = pl.program_id(2)
    qi = pl.program_id(3)
    q2 = q_ref[...]
    negsel = negsel_ref[...]
    for hh in range(2):
        qm = jnp.where(_half_mask(q2.shape, hh), q2, jnp.zeros_like(q2))
        qcat_ref[hh] = jnp.concatenate([qm, negsel], axis=1)
    _init_state(m_ref, acc_ref)

    def qk(kt, s_ref, hh, cols):
        start = pl.multiple_of(kt * tk, tk)
        q = qcat_ref[hh] if cols is None else qcat_ref[hh, cols, :]
        s_ref[_at(hh, cols)] = _dot_t(ksel_ref[pl.ds(start, tk), :], q)

    def softmax_pv(kt, s_ref, off, hh, cols):
        start = pl.multiple_of(kt * tk, tk)
        _online_step_t(s_ref[_at(hh, cols)], vselt_ref[:, pl.ds(start, tk)], m_ref, acc_ref, _at(hh, cols), off)

    halves = tq // tk

    def win_tiles(half):
        wt = qi * halves + half
        return (pl.multiple_of(wt * tk, tk), pl.multiple_of(jnp.maximum(wt - 1, 0) * tk, tk),
                jnp.where(wt > 0, 1, tk))

    def window_qk():
        for half in range(halves):
            cols = slice(half * tk, (half + 1) * tk)
            d0, p0, _ = win_tiles(half)
            for hh in range(2):
                qm = qcat_ref[hh, cols, :LANES]
                w_ref[hh, 0, :, cols] = _dot_t(kwin_ref[pl.ds(d0, tk), :], qm)
                w_ref[hh, 1, :, cols] = _dot_t(kwin_ref[pl.ds(p0, tk), :], qm)

    _pipelined_sweep(qi, 2, qk, softmax_pv, s0_ref, s1_ref, before_tail=window_qk)
    _init_state(mw_ref, accw_ref)
    for tile in range(2):
        for half in range(halves):
            cols = slice(half * tk, (half + 1) * tk)
            d0, p0, prev_lo = win_tiles(half)
            for hh in range(2):
                if tile == 0:
                    _online_step_t(w_ref[hh, 0, :, cols], vwint_ref[:, pl.ds(d0, tk)], mw_ref, accw_ref,
                                   _at(hh, cols), off=0)
                else:
                    _online_step_t(w_ref[hh, 1, :, cols], vwint_ref[:, pl.ds(p0, tk)], mw_ref, accw_ref,
                                   _at(hh, cols), lo=prev_lo)
    o_sel = _pair_rows(acc_ref)
    o_win = _pair_rows(accw_ref)

    gates = gate_ref[...]
    head0 = (g * NSA_HPG + j * 2) * 3
    low = lax.broadcasted_iota(jnp.int32, (tq, LANES), 1) < HEAD_DIM
    mix = [jnp.where(low, _pick_col(gates, head0 + i), _pick_col(gates, head0 + 3 + i)) for i in range(3)]
    out = mix[0] * oc_ref[...] + mix[1] * o_sel + mix[2] * o_win
    o_ref[...] = (out * z_ref[...]).astype(o_ref.dtype)


def _nsa_attn(p, negsel, o_cmp, batch, seq):
    m = p['nsa_q'].shape[0]
    t = ATT_TQ
    nq = seq // t
    sq = None
    row_blk = lambda b, g, j, i: (b * nq + i, g * 2 + j)
    k_spec = pl.BlockSpec((seq, LANES), lambda b, g, j, i: (b, g))
    kcat_spec = pl.BlockSpec((seq, 2 * LANES), lambda b, g, j, i: (b, g))
    vt_spec = pl.BlockSpec((sq, LANES, seq), lambda b, g, j, i: (b, g, 0))
    return pl.pallas_call(
        _nsa_attn_kernel,
        grid=(batch, NSA_KV_GROUPS, NSA_HPG // 2, nq),
        in_specs=[pl.BlockSpec((t, LANES), row_blk),
                  kcat_spec, vt_spec, k_spec, vt_spec,
                  pl.BlockSpec((sq, sq, t, LANES), lambda b, g, j, i: (b, g, i, 0)),
                  pl.BlockSpec((t, LANES), row_blk),
                  pl.BlockSpec((t, LANES), lambda b, g, j, i: (b * nq + i, 0)),
                  pl.BlockSpec((t, LANES), row_blk)],
        out_specs=pl.BlockSpec((t, LANES), row_blk),
        out_shape=jax.ShapeDtypeStruct((m, NSA_W), jnp.bfloat16),
        scratch_shapes=[pltpu.VMEM((2, t, 2 * LANES), jnp.bfloat16),
                        pltpu.VMEM((2, ATT_TK, t), jnp.float32),
                        pltpu.VMEM((2, ATT_TK, t), jnp.float32),
                        pltpu.VMEM((2, 2, ATT_TK, t), jnp.float32),
                        pltpu.VMEM((2, 1, t), jnp.float32),
                        pltpu.VMEM((2, LANES, t), jnp.float32),
                        pltpu.VMEM((2, 1, t), jnp.float32),
                        pltpu.VMEM((2, LANES, t), jnp.float32)],
        compiler_params=_cparams(("parallel", "parallel", "parallel", "arbitrary")),
        name="nsa_sel_win_attn",
    )(p['nsa_q'], p['nsa_k_sel'], p['nsa_v_sel_t'], p['nsa_k_win'], p['nsa_v_win_t'], negsel, o_cmp,
      p['nsa_gate'], p['nsa_z'])


def _fox_kernel(q_ref, k_ref, vt_ref, z_ref, o_ref, s0_ref, s1_ref, m_ref, acc_ref):
    tk = ATT_TK
    qi = pl.program_id(2)
    _init_state(m_ref, acc_ref)

    def qk(kt, s_ref, hh, cols):
        start = pl.multiple_of(kt * tk, tk)
        cat = slice(hh * FOX_CAT, (hh + 1) * FOX_CAT)
        q = q_ref[:, cat] if cols is None else q_ref[cols, cat]
        s_ref[_at(hh, cols)] = _dot_t(k_ref[pl.ds(start, tk), cat], q)

    def softmax_pv(kt, s_ref, off, hh, cols):
        start = pl.multiple_of(kt * tk, tk)
        _online_step_t(s_ref[_at(hh, cols)], vt_ref[hh * LANES:(hh + 1) * LANES, pl.ds(start, tk)],
                       m_ref, acc_ref, _at(hh, cols), off)

    _pipelined_sweep(qi, 2, qk, softmax_pv, s0_ref, s1_ref)
    o_ref[...] = (_pair_rows(acc_ref) * z_ref[...]).astype(o_ref.dtype)


def _fox_attn(p, qcat, kcat, batch, seq):
    m = qcat.shape[0]
    t = ATT_TQ
    nq = seq // t
    row_blk = lambda b, j, i: (b * nq + i, j)
    return pl.pallas_call(
        _fox_kernel,
        grid=(batch, FOX_HEADS // 2, nq),
        in_specs=[pl.BlockSpec((t, 2 * FOX_CAT), row_blk),
                  pl.BlockSpec((seq, 2 * FOX_CAT), lambda b, j, i: (b, j)),
                  pl.BlockSpec((None, 2 * LANES, seq), lambda b, j, i: (b, j, 0)),
                  pl.BlockSpec((t, LANES), row_blk)],
        out_specs=pl.BlockSpec((t, LANES), row_blk),
        out_shape=jax.ShapeDtypeStruct((m, FOX_W), jnp.bfloat16),
        scratch_shapes=[pltpu.VMEM((2, ATT_TK, t), jnp.float32),
                        pltpu.VMEM((2, ATT_TK, t), jnp.float32),
                        pltpu.VMEM((2, 1, t), jnp.float32),
                        pltpu.VMEM((2, LANES, t), jnp.float32)],
        compiler_params=_cparams(("parallel", "parallel", "arbitrary")),
        name="fox_attn",
    )(qcat, kcat, p['fox_vt'], p['fox_z'])


def _diff_kernel(lam_init, q_ref, k_ref, vt_ref, z_ref, lam_ref, g_ref, o_ref, qm_ref, s0_ref, s1_ref,
                 m_ref, acc_ref):
    tq, tk = ATT_TQ, ATT_TK
    qi = pl.program_id(2)
    q2 = q_ref[...]
    lane = lax.broadcasted_iota(jnp.int32, (tq, LANES), 1)
    for idx in range(4):
        lo = idx * DIFF_QK_DIM
        qm_ref[idx] = jnp.where((lane >= lo) & (lane < lo + DIFF_QK_DIM), q2, jnp.zeros_like(q2))
    _init_state(m_ref, acc_ref)

    def qk(kt, s_ref, idx, cols):
        start = pl.multiple_of(kt * tk, tk)
        q = qm_ref[idx] if cols is None else qm_ref[idx, cols, :]
        s_ref[_at(idx, cols)] = _dot_t(k_ref[pl.ds(start, tk), :], q)

    def softmax_pv(kt, s_ref, off, idx, cols):
        start = pl.multiple_of(kt * tk, tk)
        hh = idx // 2
        _online_step_t(s_ref[_at(idx, cols)], vt_ref[hh * LANES:(hh + 1) * LANES, pl.ds(start, tk)],
                       m_ref, acc_ref, _at(idx, cols), off)

    _pipelined_sweep(qi, 4, qk, softmax_pv, s0_ref, s1_ref)

    lam_rows = lam_ref[...]
    lam = (jnp.exp(jnp.sum(lam_rows[0:1, :] * lam_rows[1:2, :], axis=1, keepdims=True))
           - jnp.exp(jnp.sum(lam_rows[2:3, :] * lam_rows[3:4, :], axis=1, keepdims=True)) + lam_init)
    o = [_normalized_t(acc_ref[idx]) for idx in range(4)]
    normed = []
    for hh in range(2):
        d = o[2 * hh] - lam * o[2 * hh + 1]
        ms = jnp.sum(d * d, axis=0, keepdims=True) / HEAD_DIM
        normed.append(d * lax.rsqrt(ms + RMS_EPS))
    out = jnp.concatenate(normed, axis=0).T * g_ref[...] * (1.0 - lam_init)
    o_ref[...] = (out * z_ref[...]).astype(o_ref.dtype)


def _diff_attn(p, lam_rows, g_row, lam_init, batch, seq):
    m = p['diff_q'].shape[0]
    t = ATT_TQ
    nq = seq // t
    row_blk = lambda b, j, i: (b * nq + i, j)
    return pl.pallas_call(
        partial(_diff_kernel, lam_init),
        grid=(batch, DIFF_HEADS // 2, nq),
        in_specs=[pl.BlockSpec((t, LANES), row_blk),
                  pl.BlockSpec((seq, LANES), lambda b, j, i: (b, j)),
                  pl.BlockSpec((None, 2 * LANES, seq), lambda b, j, i: (b, j, 0)),
                  pl.BlockSpec((t, LANES), row_blk),
                  pl.BlockSpec((8, LANES), lambda b, j, i: (0, 0)),
                  pl.BlockSpec((1, LANES), lambda b, j, i: (0, 0))],
        out_specs=pl.BlockSpec((t, LANES), row_blk),
        out_shape=jax.ShapeDtypeStruct((m, DIFF_W), jnp.bfloat16),
        scratch_shapes=[pltpu.VMEM((4, t, LANES), jnp.bfloat16),
                        pltpu.VMEM((4, ATT_TK, t), jnp.float32),
                        pltpu.VMEM((4, ATT_TK, t), jnp.float32),
                        pltpu.VMEM((4, 1, t), jnp.float32),
                        pltpu.VMEM((4, LANES, t), jnp.float32)],
        compiler_params=_cparams(("parallel", "parallel", "arbitrary")),
        name="diff_attn",
    )(p['diff_q'], p['diff_k'], p['diff_vt'], p['diff_z'], lam_rows, g_row)


def _out_kernel(of_ref, on_ref, od_ref, w_ref, x_ref, g_ref, b_ref, o_ref):
    y = (DEEPNORM_ALPHA * x_ref[...]
         + _dot(of_ref[...], w_ref[0:FOX_W, :])
         + _dot(on_ref[...], w_ref[FOX_W:FOX_W + NSA_W, :])
         + _dot(od_ref[...], w_ref[FOX_W + NSA_W:, :]))
    mu = jnp.mean(y, axis=1, keepdims=True)
    yc = y - mu
    var = jnp.mean(yc * yc, axis=1, keepdims=True)
    o_ref[...] = yc * lax.rsqrt(var + LN_EPS) * g_ref[...] + b_ref[...]


def _out_proj(o_fox, o_nsa, o_diff, w_out, x2d, g_row, b_row):
    m = x2d.shape[0]
    tm = OUT_TM
    row = lambda i: (i, 0)
    const = lambda i: (0, 0)
    return pl.pallas_call(
        _out_kernel,
        grid=(m // tm,),
        in_specs=[pl.BlockSpec((tm, FOX_W), row), pl.BlockSpec((tm, NSA_W), row),
                  pl.BlockSpec((tm, DIFF_W), row),
                  pl.BlockSpec((D_MODEL, D_MODEL), const),
                  pl.BlockSpec((tm, D_MODEL), row),
                  pl.BlockSpec((1, D_MODEL), const), pl.BlockSpec((1, D_MODEL), const)],
        out_specs=pl.BlockSpec((tm, D_MODEL), row),
        out_shape=jax.ShapeDtypeStruct((m, D_MODEL), jnp.float32),
        compiler_params=_cparams(("parallel",)),
        name="out_proj_ln",
    )(o_fox, o_nsa, o_diff, w_out, x2d, g_row, b_row)


def _overlap_matrix(seq):
    n_slc = seq // SEL_BLOCK
    cmp_start = CMP_STRIDE * np.arange(N_CMP_PAD)
    sel_start = SEL_BLOCK * np.arange(n_slc)
    ov = np.clip(np.minimum(cmp_start[:, None] + CMP_LEN, sel_start[None, :] + SEL_BLOCK)
                 - np.maximum(cmp_start[:, None], sel_start[None, :]), 0, None).astype(np.float32) / CMP_LEN
    return ov


def kernel(x, w_in, b_fox_f, cmp_pos_k, cmp_pos_v, cmp_w1_k, cmp_w2_k, cmp_w1_v, cmp_w2_v,
           lam_q1, lam_k1, lam_q2, lam_k2, diff_subln_g, w_out, ln_g, ln_b):
    batch, seq, _ = x.shape
    assert seq % ATT_TQ == 0 and seq // SEL_BLOCK == LANES and (seq - CMP_LEN) // CMP_STRIDE + 1 < N_CMP_PAD
    assert WINDOW == ATT_TK and ATT_TQ % ATT_TK == 0
    m = batch * seq
    tables = _rope_tables(seq, HEAD_DIM) + _rope_tables(seq, DIFF_QK_DIM)
    overlap = jnp.asarray(_overlap_matrix(seq), jnp.bfloat16)

    x2d = x.reshape(m, D_MODEL)
    for l in range(DEPTH):
        w_perm = _permute_columns(w_in[l], PROJ_COLS).astype(jnp.bfloat16)
        wt_perm = _permute_columns(w_in[l], VT_COLS).astype(jnp.bfloat16).T
        bias_row = jnp.zeros((1, LANES), jnp.float32).at[0, :FOX_HEADS].set(b_fox_f[l])
        p = _project(x2d, w_perm, wt_perm, bias_row, tables, seq)

        o_fox = _fox_attn(p, p['fox_qcat'], p['fox_kcat'], batch, seq)

        kc = _compress(p['nsa_k_cmp'], cmp_pos_k[l], cmp_w1_k[l], cmp_w2_k[l], batch, seq, False)
        vc = _compress(p['nsa_v_cmp'], cmp_pos_v[l], cmp_w1_v[l], cmp_w2_v[l], batch, seq, True)
        o_cmp, negsel = _cmp_topk(p['nsa_q'], kc, vc, overlap, batch, seq)
        o_nsa = _nsa_attn(p, negsel, o_cmp, batch, seq)

        lam_init = 0.8 - 0.6 * math.exp(-0.3 * l)
        lam_rows = jnp.zeros((8, LANES), jnp.float32)
        for r, v in enumerate((lam_q1[l], lam_k1[l], lam_q2[l], lam_k2[l])):
            lam_rows = lam_rows.at[r, :DIFF_QK_DIM].set(v.astype(jnp.float32))
        g_row = jnp.tile(diff_subln_g[l].reshape(1, HEAD_DIM), (1, 2))
        o_diff = _diff_attn(p, lam_rows, g_row, lam_init, batch, seq)

        x2d = _out_proj(o_fox, o_nsa, o_diff, w_out[l].astype(jnp.bfloat16), x2d,
                        ln_g[l].reshape(1, D_MODEL), ln_b[l].reshape(1, D_MODEL))
    return x2d.reshape(batch, seq, D_MODEL)
```
